```python
import jax
import jax.numpy as jnp
from jax import lax
import numpy as np

D_MODEL = 1024
BATCH = 8
SEQ = 2048
DEPTH = 2

GRID_W = 64
CTX_LEN = 256
N_EVEN = (DEPTH + 1) // 2
N_ODD = DEPTH // 2
NORM_EPS = 1e-6

HEAD_DIM = 128
N_Q_HEADS = 8
N_KV_HEADS = 2
Q_PER_KV = N_Q_HEADS // N_KV_HEADS
ATTN_W = N_Q_HEADS * HEAD_DIM
KV_W = N_KV_HEADS * HEAD_DIM
ROPE_THETA = 10000.0
ROPE_PAIRS_PER_AXIS = HEAD_DIM // 4
Q_BLOCK = 128

CONV_W = D_MODEL
CONV_TAPS = 3

PROJ_SPLITS = (ATTN_W, ATTN_W + KV_W, ATTN_W + 2 * KV_W,
               ATTN_W + 2 * KV_W + CONV_W, ATTN_W + 2 * KV_W + 2 * CONV_W)
PROJ_W = ATTN_W + 2 * KV_W + 3 * CONV_W
MIX_OUT_W = ATTN_W + CONV_W

RW_N = 64
RW_H = D_MODEL // RW_N
DECAY_LORA = 64
ICLR_LORA = 64
GATE_LORA = 128
GN_EPS = 64e-5

N_EXPERTS = 16
D_EXPERT = 1024
EC_CAPACITY = 2

kernel_name = "hybrid_diffusion_attnconv_rwkv7_ecmoe"


def _rms(t, g):
    tf = t.astype(jnp.float32)
    y = tf * lax.rsqrt(jnp.mean(tf * tf, axis=-1, keepdims=True) + NORM_EPS)
    return (y * g.astype(jnp.float32)).astype(t.dtype)


def _ada(t, g, shift, scale):
    return _rms(t, g) * (1.0 + scale) + shift


def _conv3(u, w):
    up = jnp.pad(u, ((0, 0), (1, 1), (0, 0)))
    return up[:, :-2] * w[0] + up[:, 1:-1] * w[1] + up[:, 2:] * w[2]


def _neighbour_mean(h):
    hp = jnp.pad(h, ((0, 0), (1, 1), (0, 0)))
    return 0.5 * (hp[:, :-2] + hp[:, 2:])


def _axial_rope(n):
    rows = n // GRID_W
    row = jnp.broadcast_to(jnp.arange(rows)[:, None], (rows, GRID_W)).reshape(-1).astype(jnp.float32)
    col = jnp.broadcast_to(jnp.arange(GRID_W)[None, :], (rows, GRID_W)).reshape(-1).astype(jnp.float32)
    inv = ROPE_THETA ** (-jnp.arange(ROPE_PAIRS_PER_AXIS, dtype=jnp.float32) / ROPE_PAIRS_PER_AXIS)
    ang = jnp.concatenate([row[:, None] * inv, col[:, None] * inv], axis=-1)
    return jnp.cos(ang), jnp.sin(ang)


def _rope(t, cos, sin):
    tf = t.astype(jnp.float32)
    half = HEAD_DIM // 2
    t1, t2 = tf[..., :half], tf[..., half:]
    c = cos[None, :, None, :]
    s = sin[None, :, None, :]
    return jnp.concatenate([t1 * c - t2 * s, t2 * c + t1 * s], axis=-1).astype(t.dtype)


def _attend(q, k, v):
    b, tq = q.shape[:2]
    qg = q.reshape(b, tq, N_KV_HEADS, Q_PER_KV, HEAD_DIM)
    s = jnp.einsum('bqhgd,bkhd->bhgqk', qg, k).astype(jnp.float32) * (HEAD_DIM ** -0.5)
    p = jax.nn.softmax(s, axis=-1).astype(v.dtype)
    o = jnp.einsum('bhgqk,bkhd->bqhgd', p, v)
    return o.reshape(b, tq, ATTN_W)


def _block_attention(q, k, v):
    b, n = q.shape[:2]
    nb = n // Q_BLOCK
    qb = jnp.moveaxis(q.reshape(b, nb, Q_BLOCK, N_Q_HEADS, HEAD_DIM), 1, 0)
    ob = lax.map(lambda qi: _attend(qi, k, v), qb)
    return jnp.moveaxis(ob, 0, 1).reshape(b, n, ATTN_W)


def _attn_conv_mixer(h, hc, w_in, w_out, q_g, k_g, conv_w, cos, sin, need_ctx):
    b, n, _ = h.shape
    lc = hc.shape[1]
    q, k, v, gb, gc, xc = jnp.split(h @ w_in, PROJ_SPLITS, axis=-1)
    q = _rope(_rms(q.reshape(b, n, N_Q_HEADS, HEAD_DIM), q_g), cos, sin)
    k = _rope(_rms(k.reshape(b, n, N_KV_HEADS, HEAD_DIM), k_g), cos, sin)
    v = v.reshape(b, n, N_KV_HEADS, HEAD_DIM)
    if need_ctx:
        qc, kc, vc, gbc, gcc, xcc = jnp.split(hc @ w_in, PROJ_SPLITS, axis=-1)
    else:
        kc, vc = jnp.split(hc @ w_in[:, ATTN_W:ATTN_W + 2 * KV_W], 2, axis=-1)
    kc = _rms(kc.reshape(b, lc, N_KV_HEADS, HEAD_DIM), k_g)
    vc = vc.reshape(b, lc, N_KV_HEADS, HEAD_DIM)
    k_all = jnp.concatenate([kc, k], axis=1)
    v_all = jnp.concatenate([vc, v], axis=1)
    o_attn = _block_attention(q, k_all, v_all)
    o_conv = gb * _conv3(gc * xc, conv_w)
    y = jnp.concatenate([o_attn, o_conv], axis=-1) @ w_out
    if not need_ctx:
        return y, None
    qc = _rms(qc.reshape(b, lc, N_Q_HEADS, HEAD_DIM), q_g)
    oc_attn = _attend(qc, kc, vc)
    oc_conv = gbc * _conv3(gcc * xcc, conv_w)
    yc = jnp.concatenate([oc_attn, oc_conv], axis=-1) @ w_out
    return y, yc


def _heads(t):
    return t.reshape(t.shape[:-1] + (RW_H, RW_N))


def _rwkv_features(h, mu, w_k, w_v, w0, w1, w2, a0, a1, a2, k_k, k_a):
    xx = _neighbour_mean(h) - h
    k = ((h + xx * mu[2]) @ w_k).astype(jnp.float32)
    v = ((h + xx * mu[3]) @ w_v).astype(jnp.float32)
    xw = h + xx * mu[1]
    xa = h + xx * mu[4]
    kk = _heads(k * k_k.astype(jnp.float32))
    kk = kk * lax.rsqrt(jnp.maximum(jnp.sum(kk * kk, axis=-1, keepdims=True), 1e-24))
    dirs = []
    for d in range(2):
        w_log = -jax.nn.softplus(-(w0[d] + jnp.tanh(xw @ w1[d]) @ w2[d]).astype(jnp.float32)) - 0.5
        a = jax.nn.sigmoid((a0[d] + (xa @ a1[d]) @ a2[d]).astype(jnp.float32))
        kd = k * (1.0 + (a - 1.0) * k_a.astype(jnp.float32))
        dirs.append((_heads(jnp.exp(-jnp.exp(w_log))), _heads(kd), _heads(a)))
    return xx, _heads(v), kk, dirs


def _rwkv_scan(s0, decay, kd, v, kk, a, r, reverse):
    emit = r is not None
    seq = (decay, kd, v, kk, a) + ((r,) if emit else ())
    seq = tuple(jnp.moveaxis(t, 1, 0) for t in seq)

    def step(s, inp):
        w_t, k_t, v_t, kk_t, a_t = inp[:5]
        s_kk = jnp.einsum('bhvk,bhk->bhv', s, kk_t)
        s = (s * w_t[:, :, None, :] - s_kk[..., None] * (kk_t * a_t)[:, :, None, :]
             + v_t[..., None] * k_t[:, :, None, :])
        y_t = jnp.einsum('bhvk,bhk->bhv', s, inp[5]) if emit else None
        return s, y_t

    s_final, ys = lax.scan(step, s0, seq, reverse=reverse)
    return s_final, (jnp.moveaxis(ys, 0, 1) if emit else None)


def _rwkv_readout(h, xx, y_fwd, y_bwd, r, v, kd_fwd, kd_bwd, mu_g, g1, g2, r_k, ln_w, ln_b, w_o):
    y = y_fwd + y_bwd
    mean = jnp.mean(y, axis=-1, keepdims=True)
    var = jnp.mean(jnp.square(y - mean), axis=-1, keepdims=True)
    y = ((y - mean) * lax.rsqrt(var + GN_EPS)).reshape(h.shape) * ln_w + ln_b
    rk = r_k.astype(jnp.float32)
    bonus = (jnp.sum(r * kd_fwd * rk, axis=-1, keepdims=True)
             + jnp.sum(r * kd_bwd * rk, axis=-1, keepdims=True)) * v
    g = jax.nn.sigmoid((h + xx * mu_g) @ g1) @ g2
    return ((y + bonus.reshape(h.shape)).astype(h.dtype) * g) @ w_o


def _rwkv_mixer(h, hc, mu, w_r, w_k, w_v, w_o, w0, w1, w2, a0, a1, a2, g1, g2,
                k_k, k_a, r_k, ln_w, ln_b, need_ctx):
    xx, v, kk, dirs = _rwkv_features(h, mu, w_k, w_v, w0, w1, w2, a0, a1, a2, k_k, k_a)
    xxc, vc, kkc, dirsc = _rwkv_features(hc, mu, w_k, w_v, w0, w1, w2, a0, a1, a2, k_k, k_a)
    r = _heads(((h + xx * mu[0]) @ w_r).astype(jnp.float32))
    rc = _heads(((hc + xxc * mu[0]) @ w_r).astype(jnp.float32)) if need_ctx else None
    s0 = jnp.zeros((h.shape[0], RW_H, RW_N, RW_N), jnp.float32)
    ys, ycs = [], []
    for d, reverse in enumerate((False, True)):
        dec, kd, a = dirs[d]
        decc, kdc, ac = dirsc[d]
        s_ctx, yc = _rwkv_scan(s0, decc, kdc, vc, kkc, ac, rc, reverse)
        _, y = _rwkv_scan(s_ctx, dec, kd, v, kk, a, r, reverse)
        ys.append(y)
        ycs.append(yc)
    out = _rwkv_readout(h, xx, ys[0], ys[1], r, v, dirs[0][1], dirs[1][1],
                        mu[5], g1, g2, r_k, ln_w, ln_b, w_o)
    if not need_ctx:
        return out, None
    out_c = _rwkv_readout(hc, xxc, ycs[0], ycs[1], rc, vc, dirsc[0][1], dirsc[1][1],
                          mu[5], g1, g2, r_k, ln_w, ln_b, w_o)
    return out, out_c


def _ec_moe(h, router, w_gate, w_up, w_down):
    b, n, d = h.shape
    cap = EC_CAPACITY * n // N_EXPERTS
    aff = jax.nn.softmax((h @ router).astype(jnp.float32), axis=-1)
    top_aff, top_idx = lax.top_k(jnp.swapaxes(aff, 1, 2), cap)
    xs = jax.vmap(lambda hb, ib: hb[ib])(h, top_idx)
    gate = jnp.einsum('becd,edf->becf', xs, w_gate)
    up = jnp.einsum('becd,edf->becf', xs, w_up)
    out = jnp.einsum('becf,efd->becd', jax.nn.silu(gate) * up, w_down)
    out = out * top_aff[..., None].astype(out.dtype)
    flat = (jnp.arange(b)[:, None, None] * n + top_idx).reshape(-1)
    y = jnp.zeros((b * n, d), out.dtype).at[flat].add(out.reshape(-1, d))
    return y.reshape(b, n, d)


def setup_inputs(seed: int = 0) -> dict:
    key = jax.random.key(seed)
    ks = iter(jax.random.split(key, 40))
    D = D_MODEL

    def nrm(shape, scale):
        return scale * jax.random.normal(next(ks), shape, jnp.float32)

    return {
        "x": nrm((BATCH, SEQ, D), 1.0),
        "c": nrm((BATCH, D), 1.0),
        "ctx": nrm((BATCH, CTX_LEN, D), 1.0),
        "c_ctx": nrm((D,), 1.0),
        "ada_w": nrm((DEPTH, D, 6 * D), 0.5 * D ** -0.5),
        "ada_b": nrm((DEPTH, 6 * D), 0.02),
        "norm_g": 1.0 + nrm((DEPTH, 2, D), 0.05),
        "final_g": 1.0 + nrm((D,), 0.05),
        "mix_w_in": nrm((N_EVEN, D, PROJ_W), D ** -0.5),
        "mix_w_out": nrm((N_EVEN, MIX_OUT_W, D), MIX_OUT_W ** -0.5),
        "q_norm_g": 1.0 + nrm((N_EVEN, HEAD_DIM), 0.05),
        "k_norm_g": 1.0 + nrm((N_EVEN, HEAD_DIM), 0.05),
        "conv_w": nrm((N_EVEN, CONV_TAPS, CONV_W), CONV_TAPS ** -0.5),
        "rw_mu": jax.random.uniform(next(ks), (N_ODD, 6, D), jnp.float32),
        "rw_w_r": nrm((N_ODD, D, D), D ** -0.5),
        "rw_w_k": nrm((N_ODD, D, D), D ** -0.5),
        "rw_w_v": nrm((N_ODD, D, D), D ** -0.5),
        "rw_w_o": nrm((N_ODD, D, D), D ** -0.5),
        "rw_w0": -1.0 + nrm((N_ODD, 2, D), 0.5),
        "rw_w1": nrm((N_ODD, 2, D, DECAY_LORA), D ** -0.5),
        "rw_w2": nrm((N_ODD, 2, DECAY_LORA, D), 0.5 * DECAY_LORA ** -0.5),
        "rw_a0": nrm((N_ODD, 2, D), 0.1),
        "rw_a1": nrm((N_ODD, 2, D, ICLR_LORA), D ** -0.5),
        "rw_a2": nrm((N_ODD, 2, ICLR_LORA, D), 0.5 * ICLR_LORA ** -0.5),
        "rw_g1": nrm((N_ODD, D, GATE_LORA), D ** -0.5),
        "rw_g2": nrm((N_ODD, GATE_LORA, D), GATE_LORA ** -0.5),
        "rw_k_k": 0.85 + nrm((N_ODD, D), 0.05),
        "rw_k_a": 1.0 + nrm((N_ODD, D), 0.05),
        "rw_r_k": nrm((N_ODD, RW_H, RW_N), 0.1),
        "rw_ln_w": 1.0 + nrm((N_ODD, D), 0.05),
        "rw_ln_b": nrm((N_ODD, D), 0.02),
        "moe_router": nrm((DEPTH, D, N_EXPERTS), D ** -0.5),
        "moe_w_gate": nrm((DEPTH, N_EXPERTS, D, D_EXPERT), D ** -0.5),
        "moe_w_up": nrm((DEPTH, N_EXPERTS, D, D_EXPERT), D ** -0.5),
        "moe_w_down": nrm((DEPTH, N_EXPERTS, D_EXPERT, D), D_EXPERT ** -0.5),
    }


def reference(x, c, ctx, c_ctx, ada_w, ada_b, norm_g, final_g,
              mix_w_in, mix_w_out, q_norm_g, k_norm_g, conv_w,
              rw_mu, rw_w_r, rw_w_k, rw_w_v, rw_w_o, rw_w0, rw_w1, rw_w2,
              rw_a0, rw_a1, rw_a2, rw_g1, rw_g2, rw_k_k, rw_k_a, rw_r_k, rw_ln_w, rw_ln_b,
              moe_router, moe_w_gate, moe_w_up, moe_w_down):
    n = x.shape[1]
    cos, sin = _axial_rope(n)
    for i in range(DEPTH):
        need_ctx = i < DEPTH - 1
        mod = jax.nn.silu(c) @ ada_w[i] + ada_b[i]
        mod_c = jax.nn.silu(c_ctx) @ ada_w[i] + ada_b[i]
        sh1, sc1, gt1, sh2, sc2, gt2 = jnp.split(mod[:, None, :], 6, axis=-1)
        sh1c, sc1c, gt1c, sh2c, sc2c, gt2c = jnp.split(mod_c, 6, axis=-1)

        h = _ada(x, norm_g[i, 0], sh1, sc1)
        hc = _ada(ctx, norm_g[i, 0], sh1c, sc1c)
        j = i // 2
        if i % 2 == 0:
            y, yc = _attn_conv_mixer(h, hc, mix_w_in[j], mix_w_out[j], q_norm_g[j], k_norm_g[j],
                                     conv_w[j], cos, sin, need_ctx)
        else:
            y, yc = _rwkv_mixer(h, hc, rw_mu[j], rw_w_r[j], rw_w_k[j], rw_w_v[j], rw_w_o[j],
                                rw_w0[j], rw_w1[j], rw_w2[j], rw_a0[j], rw_a1[j], rw_a2[j],
                                rw_g1[j], rw_g2[j], rw_k_k[j], rw_k_a[j], rw_r_k[j],
                                rw_ln_w[j], rw_ln_b[j], need_ctx)
        x = x + gt1 * y
        x = x + gt2 * _ec_moe(_ada(x, norm_g[i, 1], sh2, sc2),
                              moe_router[i], moe_w_gate[i], moe_w_up[i], moe_w_down[i])
        if need_ctx:
            ctx = ctx + gt1c * yc
            ctx = ctx + gt2c * _ec_moe(_ada(ctx, norm_g[i, 1], sh2c, sc2c),
                                       moe_router[i], moe_w_gate[i], moe_w_up[i], moe_w_down[i])
    return _rms(x, final_g)
```

```python
import functools

import jax
import jax.numpy as jnp
from jax import lax
from jax.experimental import pallas as pl
from jax.experimental.pallas import tpu as pltpu

F32 = jnp.float32
BF16 = jnp.bfloat16

NORM_EPS = 1e-6
GN_EPS = 64e-5
GRID_W = 64
ROPE_THETA = 10000.0
HEAD_DIM = 128
N_Q_HEADS = 8
N_KV_HEADS = 2
Q_PER_KV = N_Q_HEADS // N_KV_HEADS
RW_N = 64
N_EXPERTS = 16
EC_CAPACITY = 2

LANES = 128
TILE = 256
HALO = 8
CHUNK = 64
SCAN_LANES = 512
VMEM_LIMIT = 56 * 1024 * 1024


def _cparams(sem):
    return pltpu.CompilerParams(dimension_semantics=sem, vmem_limit_bytes=VMEM_LIMIT)


def _dot(a, b):
    return jnp.dot(a.astype(BF16), b.astype(BF16), preferred_element_type=F32)


def _dot_nt(a, b):
    return lax.dot_general(a.astype(BF16), b.astype(BF16), (((1,), (1,)), ((), ())),
                           preferred_element_type=F32)


def _dot_tn(a, b):
    return lax.dot_general(a.astype(BF16), b.astype(BF16), (((0,), (0,)), ((), ())),
                           preferred_element_type=F32)


def _split_dot(x, w, parts):
    acc = None
    rem = x
    for _ in range(parts):
        hi = rem.astype(BF16)
        term = jnp.dot(hi, w, preferred_element_type=F32)
        acc = term if acc is None else acc + term
        rem = rem - hi.astype(F32)
    return acc


def _rms_rows(x, g):
    ms = jnp.mean(x * x, axis=-1, keepdims=True)
    return x * lax.rsqrt(ms + NORM_EPS) * g


def _ada_rows(x, g, shift, scale):
    return _rms_rows(x, g) * (1.0 + scale) + shift


def _sigmoid(x):
    return 1.0 / (1.0 + jnp.exp(-x))


def _softplus(x):
    return jnp.maximum(x, 0.0) + jnp.log(1.0 + jnp.exp(-jnp.abs(x)))


def _shift_rows(h, prev_row, next_row):
    n = h.shape[0]
    rows = lax.broadcasted_iota(jnp.int32, h.shape, 0)
    down = jnp.where(rows == 0, prev_row, pltpu.roll(h, 1, 0))
    up = jnp.where(rows == n - 1, next_row, pltpu.roll(h, n - 1, 0))
    return down, up


def _ada_mod_kernel(c_ref, w_ref, b_ref, o_ref):
    c = c_ref[...]
    o_ref[...] = _dot(c * _sigmoid(c), w_ref[...]) + b_ref[...]


def _ada_mod(cc, ada_w, ada_b):
    depth, d, n6 = ada_w.shape
    rows = cc.shape[0]
    nb = 1536
    return pl.pallas_call(
        _ada_mod_kernel,
        out_shape=jax.ShapeDtypeStruct((depth, rows, n6), F32),
        grid=(depth, n6 // nb),
        in_specs=[
            pl.BlockSpec((rows, d), lambda l, j: (0, 0)),
            pl.BlockSpec((None, d, nb), lambda l, j: (l, 0, j)),
            pl.BlockSpec((None, 1, nb), lambda l, j: (l, 0, j)),
        ],
        out_specs=pl.BlockSpec((None, rows, nb), lambda l, j: (l, 0, j)),
        compiler_params=_cparams(("arbitrary", "arbitrary")),
        name="ada_mod",
    )(cc, ada_w, ada_b.reshape(depth, 1, n6))


def _mod_spec(d):
    return pl.BlockSpec((None, None, 6, d), lambda b, t: (b, jnp.minimum(t, 1), 0, 0))


def _in_proj_kernel(x_ref, mod_ref, g_ref, w_ref, qg_ref, kg_ref, cos_ref, sin_ref,
                    qkv_ref, conv_ref, *, n_qkv, n_rot_heads, n_q_heads):
    h = _ada_rows(x_ref[...], g_ref[...], mod_ref[0:1, :], mod_ref[1:2, :]).astype(BF16)
    nb = 512
    n_all = w_ref.shape[1]
    for j in range(n_all // nb):
        p = jnp.dot(h, w_ref[:, j * nb:(j + 1) * nb], preferred_element_type=F32)
        if j * nb < n_qkv:
            qkv_ref[:, j * nb:(j + 1) * nb] = p
        else:
            conv_ref[:, j * nb - n_qkv:(j + 1) * nb - n_qkv] = p
    cos2 = cos_ref[...]
    sin2 = sin_ref[...]
    for hd in range(n_rot_heads):
        sl = slice(hd * HEAD_DIM, (hd + 1) * HEAD_DIM)
        g = qg_ref[...] if hd < n_q_heads else kg_ref[...]
        t = _rms_rows(qkv_ref[:, sl], g)
        qkv_ref[:, sl] = t * cos2 + pltpu.roll(t, HEAD_DIM // 2, 1) * sin2


def _in_proj(xs, modsel, g, w_in, q_g, k_g, cos2, sin2):
    b, t, d = xs.shape
    n_all = w_in.shape[1]
    n_qkv = (N_Q_HEADS + 2 * N_KV_HEADS) * HEAD_DIM
    kern = functools.partial(_in_proj_kernel, n_qkv=n_qkv, n_rot_heads=N_Q_HEADS + N_KV_HEADS,
                             n_q_heads=N_Q_HEADS)
    return pl.pallas_call(
        kern,
        out_shape=(jax.ShapeDtypeStruct((b, t, n_qkv), F32),
                   jax.ShapeDtypeStruct((b, t, n_all - n_qkv), F32)),
        grid=(b, t // TILE),
        in_specs=[
            pl.BlockSpec((None, TILE, d), lambda i, j: (i, j, 0)),
            _mod_spec(d),
            pl.BlockSpec((1, d), lambda i, j: (0, 0)),
            pl.BlockSpec((d, n_all), lambda i, j: (0, 0)),
            pl.BlockSpec((1, HEAD_DIM), lambda i, j: (0, 0)),
            pl.BlockSpec((1, HEAD_DIM), lambda i, j: (0, 0)),
            pl.BlockSpec((TILE, HEAD_DIM), lambda i, j: (j, 0)),
            pl.BlockSpec((TILE, HEAD_DIM), lambda i, j: (j, 0)),
        ],
        out_specs=(pl.BlockSpec((None, TILE, n_qkv), lambda i, j: (i, j, 0)),
                   pl.BlockSpec((None, TILE, n_all - n_qkv), lambda i, j: (i, j, 0))),
        compiler_params=_cparams(("arbitrary", "arbitrary")),
        name="in_proj",
    )(xs, modsel, g, w_in, q_g, k_g, cos2, sin2)


def _attn_kernel(q_ref, k_ref, v_ref, o_ref, *, n_ctx):
    q = q_ref[...].astype(BF16)
    scale = HEAD_DIM ** -0.5

    def attend(nk):
        k = k_ref[0:nk, :].astype(BF16)
        v = v_ref[0:nk, :].astype(BF16)
        s = _dot_nt(q, k)
        m = jnp.max(s, axis=-1, keepdims=True)
        p = jnp.exp((s - m) * scale)
        l = jnp.sum(p, axis=-1, keepdims=True)
        o_ref[...] = jnp.dot(p.astype(BF16), v, preferred_element_type=F32) / l

    is_ctx = pl.program_id(2) == 0

    @pl.when(is_ctx)
    def _():
        attend(n_ctx)

    @pl.when(jnp.logical_not(is_ctx))
    def _():
        attend(k_ref.shape[0])


def _attention(qkv, n_ctx):
    b, t, _ = qkv.shape
    kern = functools.partial(_attn_kernel, n_ctx=n_ctx)
    return pl.pallas_call(
        kern,
        out_shape=jax.ShapeDtypeStruct((b, t, N_Q_HEADS * HEAD_DIM), F32),
        grid=(b, N_KV_HEADS, t // TILE, Q_PER_KV),
        in_specs=[
            pl.BlockSpec((None, TILE, HEAD_DIM), lambda i, kv, j, g: (i, j, kv * Q_PER_KV + g)),
            pl.BlockSpec((None, t, HEAD_DIM), lambda i, kv, j, g: (i, 0, N_Q_HEADS + kv)),
            pl.BlockSpec((None, t, HEAD_DIM), lambda i, kv, j, g: (i, 0, N_Q_HEADS + N_KV_HEADS + kv)),
        ],
        out_specs=pl.BlockSpec((None, TILE, HEAD_DIM), lambda i, kv, j, g: (i, j, kv * Q_PER_KV + g)),
        compiler_params=_cparams(("arbitrary",) * 4),
        name="attention",
    )(qkv, qkv, qkv)


def _residual_and_router(x, y, mod_ref, g2_ref, router_ref, x1_ref, h2_ref, aff_ref):
    x1 = x + mod_ref[2:3, :] * y
    x1_ref[...] = x1
    h2 = _ada_rows(x1, g2_ref[...], mod_ref[3:4, :], mod_ref[4:5, :]).astype(BF16)
    h2_ref[...] = h2
    logits = _dot_nt(router_ref[...], h2)
    e = jnp.exp(logits - jnp.max(logits, axis=0, keepdims=True))
    aff_ref[...] = e / jnp.sum(e, axis=0, keepdims=True)


def _mix_out_kernel(oa_ref, gb_ref, gc_ref, xc_ref, gcp_ref, xcp_ref, gcn_ref, xcn_ref,
                    x_ref, mod_ref, wout_ref, cw_ref, g2_ref, router_ref,
                    x1_ref, h2_ref, aff_ref, *, n_tiles):
    t = pl.program_id(1)
    has_prev = (t >= 2).astype(F32)
    has_next = jnp.logical_and(t >= 1, t <= n_tiles - 2).astype(F32)
    u = gc_ref[...] * xc_ref[...]
    u_prev = gcp_ref[HALO - 1:HALO, :] * xcp_ref[HALO - 1:HALO, :] * has_prev
    u_next = gcn_ref[0:1, :] * xcn_ref[0:1, :] * has_next
    down, up = _shift_rows(u, u_prev, u_next)
    o_conv = gb_ref[...] * (down * cw_ref[0:1, :] + u * cw_ref[1:2, :] + up * cw_ref[2:3, :])
    d_attn = oa_ref.shape[1]
    y = _dot(oa_ref[...], wout_ref[0:d_attn, :]) + _dot(o_conv, wout_ref[d_attn:, :])
    _residual_and_router(x_ref[...], y, mod_ref, g2_ref, router_ref, x1_ref, h2_ref, aff_ref)


def _mix_out(o_attn, conv, xs, modsel, w_out, conv_w, g2, router_t):
    b, t, d = xs.shape
    n_tiles = t // TILE
    hb = TILE // HALO
    last = t // HALO - 1
    prev_map = lambda col: (lambda i, j: (i, jnp.maximum(j * hb - 1, 0), col))
    next_map = lambda col: (lambda i, j: (i, jnp.minimum((j + 1) * hb, last), col))
    tile_map = lambda col: (lambda i, j: (i, j, col))
    n_e = router_t.shape[0]
    kern = functools.partial(_mix_out_kernel, n_tiles=n_tiles)
    return pl.pallas_call(
        kern,
        out_shape=(jax.ShapeDtypeStruct((b, t, d), F32),
                   jax.ShapeDtypeStruct((b, t, d), BF16),
                   jax.ShapeDtypeStruct((b, n_e, t), F32)),
        grid=(b, n_tiles),
        in_specs=[
            pl.BlockSpec((None, TILE, d), tile_map(0)),
            pl.BlockSpec((None, TILE, d), tile_map(0)),
            pl.BlockSpec((None, TILE, d), tile_map(1)),
            pl.BlockSpec((None, TILE, d), tile_map(2)),
            pl.BlockSpec((None, HALO, d), prev_map(1)),
            pl.BlockSpec((None, HALO, d), prev_map(2)),
            pl.BlockSpec((None, HALO, d), next_map(1)),
            pl.BlockSpec((None, HALO, d), next_map(2)),
            pl.BlockSpec((None, TILE, d), tile_map(0)),
            _mod_spec(d),
            pl.BlockSpec(w_out.shape, lambda i, j: (0, 0)),
            pl.BlockSpec(conv_w.shape, lambda i, j: (0, 0)),
            pl.BlockSpec((1, d), lambda i, j: (0, 0)),
            pl.BlockSpec((n_e, d), lambda i, j: (0, 0)),
        ],
        out_specs=(pl.BlockSpec((None, TILE, d), tile_map(0)),
                   pl.BlockSpec((None, TILE, d), tile_map(0)),
                   pl.BlockSpec((None, n_e, TILE), lambda i, j: (i, 0, j))),
        compiler_params=_cparams(("arbitrary", "arbitrary")),
        name="mix_out",
    )(o_attn, conv, conv, conv, conv, conv, conv, conv, xs, modsel, w_out, conv_w, g2, router_t)


def _moe_ffn_kernel(x_ref, a_ref, wg_ref, wu_ref, wd_ref, o_ref):
    wg = wg_ref[...].astype(BF16)
    wu = wu_ref[...].astype(BF16)
    wd = wd_ref[...].astype(BF16)
    for i in range(x_ref.shape[0]):
        x = x_ref[i]
        gate = jnp.dot(x, wg, preferred_element_type=F32)
        up = jnp.dot(x, wu, preferred_element_type=F32)
        act = (gate * _sigmoid(gate) * up).astype(BF16)
        o_ref[i] = jnp.dot(act, wd, preferred_element_type=F32) * a_ref[i]


def _moe_ffn(xs, aff, w_gate, w_up, w_down):
    b, n_e, r, d = xs.shape
    f = w_gate.shape[2]
    bb = 4 if b % 4 == 0 else 1
    return pl.pallas_call(
        _moe_ffn_kernel,
        out_shape=jax.ShapeDtypeStruct((b, n_e, r, d), F32),
        grid=(n_e, b // bb),
        in_specs=[
            pl.BlockSpec((bb, None, r, d), lambda e, i: (i, e, 0, 0)),
            pl.BlockSpec((bb, None, r, 1), lambda e, i: (i, e, 0, 0)),
            pl.BlockSpec((None, d, f), lambda e, i: (e, 0, 0)),
            pl.BlockSpec((None, d, f), lambda e, i: (e, 0, 0)),
            pl.BlockSpec((None, f, d), lambda e, i: (e, 0, 0)),
        ],
        out_specs=pl.BlockSpec((bb, None, r, d), lambda e, i: (i, e, 0, 0)),
        compiler_params=_cparams(("arbitrary", "arbitrary")),
        name="moe_ffn",
    )(xs, aff, w_gate, w_up, w_down)


def _combine_kernel(x_ref, y_ref, mod_ref, g_ref, o_ref, *, final_norm):
    x2 = x_ref[...] + mod_ref[5:6, :] * y_ref[...]
    o_ref[...] = _rms_rows(x2, g_ref[...]) if final_norm else x2


def _combine(x1, y, modsel, g, *, tile_offset, final_norm):
    b, t, d = x1.shape
    kern = functools.partial(_combine_kernel, final_norm=final_norm)
    return pl.pallas_call(
        kern,
        out_shape=jax.ShapeDtypeStruct((b, t, d), F32),
        grid=(b, t // TILE),
        in_specs=[
            pl.BlockSpec((None, TILE, d), lambda i, j: (i, j, 0)),
            pl.BlockSpec((None, TILE, d), lambda i, j: (i, j, 0)),
            pl.BlockSpec((None, None, 6, d), lambda i, j: (i, jnp.minimum(j + tile_offset, 1), 0, 0)),
            pl.BlockSpec((1, d), lambda i, j: (0, 0)),
        ],
        out_specs=pl.BlockSpec((None, TILE, d), lambda i, j: (i, j, 0)),
        compiler_params=_cparams(("arbitrary", "arbitrary")),
        name="moe_combine",
    )(x1, y, modsel, g)


def _rwkv_feat_kernel(x_ref, xp_ref, xn_ref, mod_ref, g_ref, mu_ref, wr_ref, wk_ref, wv_ref,
                      w1_ref, a1_ref, g1_ref, w2_ref, a2_ref, g2_ref, w0_ref, a0_ref,
                      kk_ref, ka_ref, seg_ref,
                      r_out, v_out, kkn_out, g_out, lw_out, kd_out, bb_out, *, n_tiles):
    t = pl.program_id(1)
    has_prev = (t >= 2).astype(F32)
    has_next = jnp.logical_and(t >= 1, t <= n_tiles - 2).astype(F32)
    g = g_ref[...]
    shift = mod_ref[0:1, :]
    scale = mod_ref[1:2, :]
    h = _ada_rows(x_ref[...], g, shift, scale)
    h_prev = _ada_rows(xp_ref[HALO - 1:HALO, :], g, shift, scale) * has_prev
    h_next = _ada_rows(xn_ref[0:1, :], g, shift, scale) * has_next
    down, up = _shift_rows(h, h_prev, h_next)
    xx = 0.5 * (down + up) - h
    d = h.shape[1]

    r_out[...] = _dot(h + xx * mu_ref[0:1, :], wr_ref[...])
    k = _dot(h + xx * mu_ref[2:3, :], wk_ref[...])
    v_out[...] = _dot(h + xx * mu_ref[3:4, :], wv_ref[...])
    gate = _sigmoid(_dot(h + xx * mu_ref[5:6, :], g1_ref[...]))
    g_out[...] = _dot(gate, g2_ref[...])

    kkp = k * kk_ref[...]
    ss = _split_dot(kkp * kkp, seg_ref[...], 2)
    kkn = kkp * lax.rsqrt(jnp.maximum(ss, 1e-24))
    kkn_out[...] = kkn

    lora_w = _dot(jnp.tanh(_dot(h + xx * mu_ref[1:2, :], w1_ref[...])), w2_ref[...])
    lora_a = _dot(_dot(h + xx * mu_ref[4:5, :], a1_ref[...]), a2_ref[...])
    for di in range(2):
        sl = slice(di * d, (di + 1) * d)
        w_log = -_softplus(-(w0_ref[di:di + 1, :] + lora_w[:, sl])) - 0.5
        lw_out[di] = -jnp.exp(w_log)
        a = _sigmoid(a0_ref[di:di + 1, :] + lora_a[:, sl])
        kd_out[di] = k * (1.0 + (a - 1.0) * ka_ref[...])
        bb_out[di] = a * kkn


def _rwkv_feat(xs, modsel, g, p):
    b, t, d = xs.shape
    n_tiles = t // TILE
    hb = TILE // HALO
    last = t // HALO - 1
    full = lambda a: pl.BlockSpec(a.shape, lambda i, j: (0,) * a.ndim)
    weights = [p["mu"], p["w_r"], p["w_k"], p["w_v"], p["w1"], p["a1"], p["g1"], p["w2"], p["a2"],
               p["g2"], p["w0"], p["a0"], p["k_k"], p["k_a"], p["seg"]]
    tile_spec = pl.BlockSpec((None, TILE, d), lambda i, j: (i, j, 0))
    dir_spec = pl.BlockSpec((2, None, TILE, d), lambda i, j: (0, i, j, 0))
    kern = functools.partial(_rwkv_feat_kernel, n_tiles=n_tiles)
    one = jax.ShapeDtypeStruct((b, t, d), F32)
    two = jax.ShapeDtypeStruct((2, b, t, d), F32)
    return pl.pallas_call(
        kern,
        out_shape=(one, one, one, one, two, two, two),
        grid=(b, n_tiles),
        in_specs=[
            tile_spec,
            pl.BlockSpec((None, HALO, d), lambda i, j: (i, jnp.maximum(j * hb - 1, 0), 0)),
            pl.BlockSpec((None, HALO, d), lambda i, j: (i, jnp.minimum((j + 1) * hb, last), 0)),
            _mod_spec(d),
            pl.BlockSpec((1, d), lambda i, j: (0, 0)),
        ] + [full(a) for a in weights],
        out_specs=(tile_spec, tile_spec, tile_spec, tile_spec, dir_spec, dir_spec, dir_spec),
        compiler_params=_cparams(("arbitrary", "arbitrary")),
        name="rwkv_feat",
    )(xs, xs, xs, modsel, g, *weights)


def _scan_kernel(lw_ref, kd_ref, bb_ref, kk_ref, r_ref, v_ref, y_ref, s_ref):
    rev = pl.program_id(0)
    step = pl.program_id(3)

    @pl.when(step == 0)
    def _():
        s_ref[...] = jnp.zeros_like(s_ref)

    L = CHUNK
    sgn = 1 - 2 * rev
    ti = lax.broadcasted_iota(jnp.int32, (L, LANES), 0)
    ii = lax.broadcasted_iota(jnp.int32, (L, LANES), 1) & (L - 1)
    order = (ti - ii) * sgn
    before = order > 0
    before_eq = order >= 0
    t2 = lax.broadcasted_iota(jnp.int32, (L, L), 0)
    i2 = lax.broadcasted_iota(jnp.int32, (L, L), 1)
    tri = jnp.where((t2 - i2) * sgn >= 0, 1.0, 0.0).astype(BF16)
    r128 = lax.broadcasted_iota(jnp.int32, (LANES, LANES), 0)
    c128 = lax.broadcasted_iota(jnp.int32, (LANES, LANES), 1)
    head_eq = (r128 >> 6) == (c128 >> 6)
    eye = r128 == c128

    def same_block(log2s):
        return (ti >> log2s) == (ii >> log2s)

    def bdiag(y):
        yb = y.astype(BF16)
        return jnp.where(head_eq, jnp.concatenate([yb, yb], axis=0), jnp.zeros((), BF16))

    def hmul(x, y):
        return jnp.dot(x.astype(BF16), bdiag(y), preferred_element_type=F32)

    for pair in range(lw_ref.shape[1] // LANES):
        sl = slice(pair * LANES, (pair + 1) * LANES)
        lw = lw_ref[:, sl]
        kk = kk_ref[:, sl]
        bb = bb_ref[:, sl]
        kd = kd_ref[:, sl]
        r = r_ref[:, sl]
        v = v_ref[:, sl]

        lam = _split_dot_left(tri, lw)
        tot = jnp.sum(lw, axis=0, keepdims=True)
        rho = lam[L // 2:L // 2 + 1, :]
        ks = kk * jnp.exp(lam - lw)
        rs = r * jnp.exp(lam)
        e_neg = jnp.exp(-rho)
        e_inv = jnp.exp(rho - lam)
        e_end = jnp.exp(tot - lam)

        lhs = jnp.concatenate([ks * e_neg, rs * e_neg], axis=0)
        g_b = lax.dot_general(lhs.astype(BF16), bdiag(bb * e_inv), (((1,), (1,)), ((), ())),
                              preferred_element_type=F32)
        g_k = lax.dot_general(lhs.astype(BF16), bdiag(kd * e_inv), (((1,), (1,)), ((), ())),
                              preferred_element_type=F32)
        a_b = jnp.where(before, g_b[0:L], 0.0)
        a_k = jnp.where(before, g_k[0:L], 0.0)
        m_b = jnp.where(before_eq, g_b[L:], 0.0)
        m_k = jnp.where(before_eq, g_k[L:], 0.0)

        n1 = jnp.where(same_block(3), -a_b, 0.0)
        n2 = hmul(n1, n1)
        n3 = hmul(n1, n2)
        n4 = hmul(n2, n2)
        x = jnp.where(ti == ii, 1.0, 0.0) + n1 + n2 + n3
        tinv = x + hmul(x, n4)
        for log2s in (3, 4, 5):
            off = jnp.where(jnp.logical_and(same_block(log2s + 1), jnp.logical_not(same_block(log2s))),
                            a_b, 0.0)
            tinv = tinv - hmul(tinv, hmul(off, tinv))

        q1 = -hmul(tinv, ks)
        p0 = -hmul(tinv, hmul(a_k, v))
        q2 = rs + hmul(m_b, q1)
        y0 = hmul(m_b, p0) + hmul(m_k, v)
        b_end = bb * e_end
        k_end = kd * e_end
        m_t = jnp.where(head_eq, _dot_tn(b_end, q1), 0.0) + jnp.where(eye, jnp.exp(tot), 0.0)
        n0 = jnp.where(head_eq, _dot_tn(jnp.concatenate([b_end, k_end], axis=0),
                                        jnp.concatenate([p0, v], axis=0)), 0.0)

        s0 = s_ref[pair]
        y_ref[:, sl] = _dot(q2, s0) + y0
        s_ref[pair] = _dot(m_t, s0) + n0


def _split_dot_left(tri, x):
    acc = None
    rem = x
    for _ in range(3):
        hi = rem.astype(BF16)
        term = jnp.dot(tri, hi, preferred_element_type=F32)
        acc = term if acc is None else acc + term
        rem = rem - hi.astype(F32)
    return acc


def _rwkv_scan(lw, kd, bb, kk, r, v, n_ctx):
    _, b, t, d = lw.shape
    nc_ctx = n_ctx // CHUNK
    nc = t // CHUNK

    def chunk(rev, j):
        bwd = jnp.where(j < nc_ctx, nc_ctx - 1 - j, nc + nc_ctx - 1 - j)
        return jnp.where(rev == 0, j, bwd)

    dir_spec = pl.BlockSpec((None, None, CHUNK, SCAN_LANES), lambda rv, i, hg, j: (rv, i, chunk(rv, j), hg))
    one_spec = pl.BlockSpec((None, CHUNK, SCAN_LANES), lambda rv, i, hg, j: (i, chunk(rv, j), hg))
    return pl.pallas_call(
        _scan_kernel,
        out_shape=jax.ShapeDtypeStruct((2, b, t, d), F32),
        grid=(2, b, d // SCAN_LANES, nc),
        in_specs=[dir_spec, dir_spec, dir_spec, one_spec, one_spec, one_spec],
        out_specs=dir_spec,
        scratch_shapes=[pltpu.VMEM((SCAN_LANES // LANES, LANES, LANES), F32)],
        compiler_params=_cparams(("arbitrary",) * 4),
        name="rwkv_scan",
    )(lw, kd, bb, kk, r, v)


def _rwkv_out_kernel(yf_ref, yb_ref, r_ref, v_ref, g_ref, kdf_ref, kdb_ref, x_ref, mod_ref,
                     wo_ref, lnw_ref, lnb_ref, rk_ref, seg_ref, g2_ref, router_ref,
                     x1_ref, h2_ref, aff_ref):
    seg = seg_ref[...]
    inv_n = 1.0 / RW_N
    y = yf_ref[...] + yb_ref[...]
    mean = _split_dot(y, seg, 2) * inv_n
    dev = y - mean
    var = _split_dot(dev * dev, seg, 2) * inv_n
    yn = dev * lax.rsqrt(var + GN_EPS) * lnw_ref[...] + lnb_ref[...]
    r = r_ref[...]
    rk = rk_ref[...]
    bonus = _split_dot(r * kdf_ref[...] * rk + r * kdb_ref[...] * rk, seg, 2) * v_ref[...]
    out = _dot((yn + bonus) * g_ref[...], wo_ref[...])
    _residual_and_router(x_ref[...], out, mod_ref, g2_ref, router_ref, x1_ref, h2_ref, aff_ref)


def _rwkv_out(y, r, v, g, kd, xs, modsel, p, g2, router_t, n_ctx):
    b, t, d = xs.shape
    off = n_ctx // TILE
    n = t - n_ctx
    n_e = router_t.shape[0]
    tile = pl.BlockSpec((None, TILE, d), lambda i, j: (i, j + off, 0))
    dir0 = pl.BlockSpec((None, None, TILE, d), lambda i, j: (0, i, j + off, 0))
    dir1 = pl.BlockSpec((None, None, TILE, d), lambda i, j: (1, i, j + off, 0))
    vec = pl.BlockSpec((1, d), lambda i, j: (0, 0))
    out_tile = pl.BlockSpec((None, TILE, d), lambda i, j: (i, j, 0))
    return pl.pallas_call(
        _rwkv_out_kernel,
        out_shape=(jax.ShapeDtypeStruct((b, n, d), F32),
                   jax.ShapeDtypeStruct((b, n, d), BF16),
                   jax.ShapeDtypeStruct((b, n_e, n), F32)),
        grid=(b, n // TILE),
        in_specs=[dir0, dir1, tile, tile, tile, dir0, dir1, tile,
                  pl.BlockSpec((None, None, 6, d), lambda i, j: (i, 1, 0, 0)),
                  pl.BlockSpec((d, d), lambda i, j: (0, 0)),
                  vec, vec, vec,
                  pl.BlockSpec((d, d), lambda i, j: (0, 0)),
                  vec,
                  pl.BlockSpec((n_e, d), lambda i, j: (0, 0))],
        out_specs=(out_tile, out_tile, pl.BlockSpec((None, n_e, TILE), lambda i, j: (i, 0, j))),
        compiler_params=_cparams(("arbitrary", "arbitrary")),
        name="rwkv_out",
    )(y, y, r, v, g, kd, kd, xs, modsel, p["w_o"], p["ln_w"], p["ln_b"], p["r_k"], p["seg"], g2, router_t)


def _route_and_ffn(h2, aff_t, sets, w_gate, w_up, w_down):
    b, t, d = h2.shape
    xs_parts, aff_parts, idx_parts = [], [], []
    for start, length in sets:
        cap = EC_CAPACITY * length // N_EXPERTS
        top_aff, top_idx = lax.top_k(aff_t[:, :, start:start + length], cap)
        top_idx = top_idx + start
        xs_parts.append(jax.vmap(lambda hb, ib: hb[ib])(h2, top_idx))
        aff_parts.append(top_aff)
        idx_parts.append(top_idx)
    xs = jnp.concatenate(xs_parts, axis=2)
    aff = jnp.concatenate(aff_parts, axis=2)
    idx = jnp.concatenate(idx_parts, axis=2)
    out = _moe_ffn(xs, aff[..., None], w_gate, w_up, w_down)
    flat = (jnp.arange(b)[:, None, None] * t + idx).reshape(-1)
    y = jnp.zeros((b * t, d), F32).at[flat].add(out.reshape(-1, d))
    return y.reshape(b, t, d)


def _rope_tables(n_ctx, n):
    rows = n // GRID_W
    row = jnp.broadcast_to(jnp.arange(rows)[:, None], (rows, GRID_W)).reshape(-1).astype(F32)
    col = jnp.broadcast_to(jnp.arange(GRID_W)[None, :], (rows, GRID_W)).reshape(-1).astype(F32)
    pairs = HEAD_DIM // 4
    inv = ROPE_THETA ** (-jnp.arange(pairs, dtype=F32) / pairs)
    ang = jnp.concatenate([row[:, None] * inv, col[:, None] * inv], axis=-1)
    cos, sin = jnp.cos(ang), jnp.sin(ang)
    cos2 = jnp.concatenate([cos, cos], axis=-1)
    sin2 = jnp.concatenate([-sin, sin], axis=-1)
    cos2 = jnp.concatenate([jnp.ones((n_ctx, HEAD_DIM), F32), cos2], axis=0)
    sin2 = jnp.concatenate([jnp.zeros((n_ctx, HEAD_DIM), F32), sin2], axis=0)
    return cos2, sin2


def _block_diag2(a, b):
    z = jnp.zeros_like(a)
    return jnp.concatenate([jnp.concatenate([a, z], axis=1), jnp.concatenate([z, b], axis=1)], axis=0)


def kernel(x, c, ctx, c_ctx, ada_w, ada_b, norm_g, final_g, mix_w_in, mix_w_out, q_norm_g, k_norm_g, conv_w, rw_mu, rw_w_r, rw_w_k, rw_w_v, rw_w_o, rw_w0, rw_w1, rw_w2, rw_a0, rw_a1, rw_a2, rw_g1, rw_g2, rw_k_k, rw_k_a, rw_r_k, rw_ln_w, rw_ln_b, moe_router, moe_w_gate, moe_w_up, moe_w_down):
    b, n, d = x.shape
    n_ctx = ctx.shape[1]
    depth = ada_w.shape[0]
    assert depth == 2 and n_ctx == TILE and n % TILE == 0 and d % SCAN_LANES == 0
    t = n_ctx + n

    xs = jnp.concatenate([ctx, x], axis=1)

    rows = -(-(b + 1) // HALO) * HALO
    cc = jnp.concatenate([c, c_ctx[None, :], jnp.zeros((rows - b - 1, d), F32)], axis=0)
    mod = _ada_mod(cc, ada_w, ada_b)
    mod_lat = mod[:, :b].reshape(depth, b, 1, 6, d)
    mod_ctx = jnp.broadcast_to(mod[:, b].reshape(depth, 1, 1, 6, d), (depth, b, 1, 6, d))
    modsel = jnp.concatenate([mod_ctx, mod_lat], axis=2)

    router_t = jnp.swapaxes(moe_router, 1, 2).astype(BF16)

    cos2, sin2 = _rope_tables(n_ctx, n)
    qkv, conv = _in_proj(xs, modsel[0], norm_g[0, 0][None], mix_w_in[0].astype(BF16),
                         q_norm_g[0][None], k_norm_g[0][None], cos2, sin2)
    o_attn = _attention(qkv, n_ctx)
    x1, h2, aff_t = _mix_out(o_attn, conv, xs, modsel[0], mix_w_out[0].astype(BF16), conv_w[0],
                             norm_g[0, 1][None], router_t[0])
    y = _route_and_ffn(h2, aff_t, [(n_ctx, n), (0, n_ctx)], moe_w_gate[0], moe_w_up[0], moe_w_down[0])
    xs = _combine(x1, y, modsel[0], final_g[None], tile_offset=0, final_norm=False)

    heads = d // RW_N
    seg = jnp.kron(jnp.eye(heads, dtype=F32), jnp.ones((RW_N, RW_N), F32)).astype(BF16)
    p = {
        "mu": rw_mu[0], "w_r": rw_w_r[0].astype(BF16), "w_k": rw_w_k[0].astype(BF16),
        "w_v": rw_w_v[0].astype(BF16), "w_o": rw_w_o[0].astype(BF16),
        "w1": jnp.concatenate([rw_w1[0, 0], rw_w1[0, 1]], axis=1).astype(BF16),
        "a1": jnp.concatenate([rw_a1[0, 0], rw_a1[0, 1]], axis=1).astype(BF16),
        "g1": rw_g1[0].astype(BF16),
        "w2": _block_diag2(rw_w2[0, 0], rw_w2[0, 1]).astype(BF16),
        "a2": _block_diag2(rw_a2[0, 0], rw_a2[0, 1]).astype(BF16),
        "g2": rw_g2[0].astype(BF16),
        "w0": rw_w0[0], "a0": rw_a0[0], "k_k": rw_k_k[0][None], "k_a": rw_k_a[0][None],
        "r_k": rw_r_k[0].reshape(1, d), "ln_w": rw_ln_w[0][None], "ln_b": rw_ln_b[0][None],
        "seg": seg,
    }
    r, v, kk, g, lw, kd, bb = _rwkv_feat(xs, modsel[1], norm_g[1, 0][None], p)
    ys = _rwkv_scan(lw, kd, bb, kk, r, v, n_ctx)
    x1, h2, aff_t = _rwkv_out(ys, r, v, g, kd, xs, modsel[1], p, norm_g[1, 1][None], router_t[1], n_ctx)
    y = _route_and_ffn(h2, aff_t, [(0, n)], moe_w_gate[1], moe_w_up[1], moe_w_down[1])
    return _combine(x1, y, modsel[1], final_g[None], tile_offset=1, final_norm=True)
```

```python
import functools

import jax
import jax.numpy as jnp
from jax import lax
from jax.experimental import pallas as pl
from jax.experimental.pallas import tpu as pltpu

F32 = jnp.float32
BF16 = jnp.bfloat16

NORM_EPS = 1e-6
GN_EPS = 64e-5
GRID_W = 64
ROPE_THETA = 10000.0
HEAD_DIM = 128
N_Q_HEADS = 8
N_KV_HEADS = 2
Q_PER_KV = N_Q_HEADS // N_KV_HEADS
RW_N = 64
N_EXPERTS = 16
EC_CAPACITY = 2

LANES = 128
TILE = 256
HALO = 8
CHUNK = 64
SCAN_LANES = 1024
VMEM_LIMIT = 56 * 1024 * 1024


def _cparams(sem):
    return pltpu.CompilerParams(dimension_semantics=sem, vmem_limit_bytes=VMEM_LIMIT)


def _dot(a, b):
    return jnp.dot(a.astype(BF16), b.astype(BF16), preferred_element_type=F32)


def _dot_nt(a, b):
    return lax.dot_general(a.astype(BF16), b.astype(BF16), (((1,), (1,)), ((), ())),
                           preferred_element_type=F32)


def _dot_tn(a, b):
    return lax.dot_general(a.astype(BF16), b.astype(BF16), (((0,), (0,)), ((), ())),
                           preferred_element_type=F32)


def _split_dot(x, w, parts):
    acc = None
    rem = x
    for _ in range(parts):
        hi = rem.astype(BF16)
        term = jnp.dot(hi, w, preferred_element_type=F32)
        acc = term if acc is None else acc + term
        rem = rem - hi.astype(F32)
    return acc


def _rms_rows(x, g):
    ms = jnp.mean(x * x, axis=-1, keepdims=True)
    return x * lax.rsqrt(ms + NORM_EPS) * g


def _ada_rows(x, g, shift, scale):
    return _rms_rows(x, g) * (1.0 + scale) + shift


def _sigmoid(x):
    return 1.0 / (1.0 + jnp.exp(-x))


def _softplus(x):
    return jnp.maximum(x, 0.0) + jnp.log(1.0 + jnp.exp(-jnp.abs(x)))


def _shift_rows(h, prev_row, next_row):
    n = h.shape[0]
    rows = lax.broadcasted_iota(jnp.int32, h.shape, 0)
    down = jnp.where(rows == 0, prev_row, pltpu.roll(h, 1, 0))
    up = jnp.where(rows == n - 1, next_row, pltpu.roll(h, n - 1, 0))
    return down, up


def _ada_mod_kernel(c_ref, w_ref, b_ref, o_ref):
    c = c_ref[...]
    o_ref[...] = _dot(c * _sigmoid(c), w_ref[...]) + b_ref[...]


def _ada_mod(cc, ada_w, ada_b):
    depth, d, n6 = ada_w.shape
    rows = cc.shape[0]
    nb = 1536
    return pl.pallas_call(
        _ada_mod_kernel,
        out_shape=jax.ShapeDtypeStruct((depth, rows, n6), F32),
        grid=(depth, n6 // nb),
        in_specs=[
            pl.BlockSpec((rows, d), lambda l, j: (0, 0)),
            pl.BlockSpec((None, d, nb), lambda l, j: (l, 0, j)),
            pl.BlockSpec((None, 1, nb), lambda l, j: (l, 0, j)),
        ],
        out_specs=pl.BlockSpec((None, rows, nb), lambda l, j: (l, 0, j)),
        compiler_params=_cparams(("arbitrary", "arbitrary")),
        name="ada_mod",
    )(cc, ada_w, ada_b.reshape(depth, 1, n6))


def _mod_spec(d):
    return pl.BlockSpec((None, None, 6, d), lambda b, t: (b, jnp.minimum(t, 1), 0, 0))


def _in_proj_kernel(x_ref, mod_ref, g_ref, w_ref, qg_ref, kg_ref, cos_ref, sin_ref,
                    qkv_ref, conv_ref, *, n_qkv, n_rot_heads, n_q_heads):
    h = _ada_rows(x_ref[...], g_ref[...], mod_ref[0:1, :], mod_ref[1:2, :]).astype(BF16)
    nb = 512
    n_all = w_ref.shape[1]
    for j in range(n_all // nb):
        p = jnp.dot(h, w_ref[:, j * nb:(j + 1) * nb], preferred_element_type=F32)
        if j * nb < n_qkv:
            qkv_ref[:, j * nb:(j + 1) * nb] = p
        else:
            conv_ref[:, j * nb - n_qkv:(j + 1) * nb - n_qkv] = p
    cos2 = cos_ref[...]
    sin2 = sin_ref[...]
    for hd in range(n_rot_heads):
        sl = slice(hd * HEAD_DIM, (hd + 1) * HEAD_DIM)
        g = qg_ref[...] if hd < n_q_heads else kg_ref[...]
        t = _rms_rows(qkv_ref[:, sl], g)
        qkv_ref[:, sl] = t * cos2 + pltpu.roll(t, HEAD_DIM // 2, 1) * sin2


def _in_proj(xs, modsel, g, w_in, q_g, k_g, cos2, sin2):
    b, t, d = xs.shape
    n_all = w_in.shape[1]
    n_qkv = (N_Q_HEADS + 2 * N_KV_HEADS) * HEAD_DIM
    kern = functools.partial(_in_proj_kernel, n_qkv=n_qkv, n_rot_heads=N_Q_HEADS + N_KV_HEADS,
                             n_q_heads=N_Q_HEADS)
    return pl.pallas_call(
        kern,
        out_shape=(jax.ShapeDtypeStruct((b, t, n_qkv), F32),
                   jax.ShapeDtypeStruct((b, t, n_all - n_qkv), F32)),
        grid=(b, t // TILE),
        in_specs=[
            pl.BlockSpec((None, TILE, d), lambda i, j: (i, j, 0)),
            _mod_spec(d),
            pl.BlockSpec((1, d), lambda i, j: (0, 0)),
            pl.BlockSpec((d, n_all), lambda i, j: (0, 0)),
            pl.BlockSpec((1, HEAD_DIM), lambda i, j: (0, 0)),
            pl.BlockSpec((1, HEAD_DIM), lambda i, j: (0, 0)),
            pl.BlockSpec((TILE, HEAD_DIM), lambda i, j: (j, 0)),
            pl.BlockSpec((TILE, HEAD_DIM), lambda i, j: (j, 0)),
        ],
        out_specs=(pl.BlockSpec((None, TILE, n_qkv), lambda i, j: (i, j, 0)),
                   pl.BlockSpec((None, TILE, n_all - n_qkv), lambda i, j: (i, j, 0))),
        compiler_params=_cparams(("arbitrary", "arbitrary")),
        name="in_proj",
    )(xs, modsel, g, w_in, q_g, k_g, cos2, sin2)


def _attn_kernel(q_ref, k_ref, v_ref, o_ref, *, n_ctx):
    q = q_ref[...].astype(BF16)
    scale = HEAD_DIM ** -0.5

    def attend(nk):
        k = k_ref[0:nk, :].astype(BF16)
        v = v_ref[0:nk, :].astype(BF16)
        s = _dot_nt(q, k)
        m = jnp.max(s, axis=-1, keepdims=True)
        p = jnp.exp((s - m) * scale)
        l = jnp.sum(p, axis=-1, keepdims=True)
        o_ref[...] = jnp.dot(p.astype(BF16), v, preferred_element_type=F32) / l

    is_ctx = pl.program_id(2) == 0

    @pl.when(is_ctx)
    def _():
        attend(n_ctx)

    @pl.when(jnp.logical_not(is_ctx))
    def _():
        attend(k_ref.shape[0])


def _attention(qkv, n_ctx):
    b, t, _ = qkv.shape
    kern = functools.partial(_attn_kernel, n_ctx=n_ctx)
    return pl.pallas_call(
        kern,
        out_shape=jax.ShapeDtypeStruct((b, t, N_Q_HEADS * HEAD_DIM), F32),
        grid=(b, N_KV_HEADS, t // TILE, Q_PER_KV),
        in_specs=[
            pl.BlockSpec((None, TILE, HEAD_DIM), lambda i, kv, j, g: (i, j, kv * Q_PER_KV + g)),
            pl.BlockSpec((None, t, HEAD_DIM), lambda i, kv, j, g: (i, 0, N_Q_HEADS + kv)),
            pl.BlockSpec((None, t, HEAD_DIM), lambda i, kv, j, g: (i, 0, N_Q_HEADS + N_KV_HEADS + kv)),
        ],
        out_specs=pl.BlockSpec((None, TILE, HEAD_DIM), lambda i, kv, j, g: (i, j, kv * Q_PER_KV + g)),
        compiler_params=_cparams(("arbitrary",) * 4),
        name="attention",
    )(qkv, qkv, qkv)


def _residual_and_router(x, y, mod_ref, g2_ref, router_ref, x1_ref, h2_ref, aff_ref):
    x1 = x + mod_ref[2:3, :] * y
    x1_ref[...] = x1
    h2 = _ada_rows(x1, g2_ref[...], mod_ref[3:4, :], mod_ref[4:5, :]).astype(BF16)
    h2_ref[...] = h2
    logits = _dot_nt(router_ref[...], h2)
    e = jnp.exp(logits - jnp.max(logits, axis=0, keepdims=True))
    aff_ref[...] = e / jnp.sum(e, axis=0, keepdims=True)


def _mix_out_kernel(oa_ref, gb_ref, gc_ref, xc_ref, gcp_ref, xcp_ref, gcn_ref, xcn_ref,
                    x_ref, mod_ref, wout_ref, cw_ref, g2_ref, router_ref,
                    x1_ref, h2_ref, aff_ref, *, n_tiles):
    t = pl.program_id(1)
    has_prev = (t >= 2).astype(F32)
    has_next = jnp.logical_and(t >= 1, t <= n_tiles - 2).astype(F32)
    u = gc_ref[...] * xc_ref[...]
    u_prev = gcp_ref[HALO - 1:HALO, :] * xcp_ref[HALO - 1:HALO, :] * has_prev
    u_next = gcn_ref[0:1, :] * xcn_ref[0:1, :] * has_next
    down, up = _shift_rows(u, u_prev, u_next)
    o_conv = gb_ref[...] * (down * cw_ref[0:1, :] + u * cw_ref[1:2, :] + up * cw_ref[2:3, :])
    d_attn = oa_ref.shape[1]
    y = _dot(oa_ref[...], wout_ref[0:d_attn, :]) + _dot(o_conv, wout_ref[d_attn:, :])
    _residual_and_router(x_ref[...], y, mod_ref, g2_ref, router_ref, x1_ref, h2_ref, aff_ref)


def _mix_out(o_attn, conv, xs, modsel, w_out, conv_w, g2, router_t):
    b, t, d = xs.shape
    n_tiles = t // TILE
    hb = TILE // HALO
    last = t // HALO - 1
    prev_map = lambda col: (lambda i, j: (i, jnp.maximum(j * hb - 1, 0), col))
    next_map = lambda col: (lambda i, j: (i, jnp.minimum((j + 1) * hb, last), col))
    tile_map = lambda col: (lambda i, j: (i, j, col))
    n_e = router_t.shape[0]
    kern = functools.partial(_mix_out_kernel, n_tiles=n_tiles)
    return pl.pallas_call(
        kern,
        out_shape=(jax.ShapeDtypeStruct((b, t, d), F32),
                   jax.ShapeDtypeStruct((b, t, d), BF16),
                   jax.ShapeDtypeStruct((b, n_e, t), F32)),
        grid=(b, n_tiles),
        in_specs=[
            pl.BlockSpec((None, TILE, d), tile_map(0)),
            pl.BlockSpec((None, TILE, d), tile_map(0)),
            pl.BlockSpec((None, TILE, d), tile_map(1)),
            pl.BlockSpec((None, TILE, d), tile_map(2)),
            pl.BlockSpec((None, HALO, d), prev_map(1)),
            pl.BlockSpec((None, HALO, d), prev_map(2)),
            pl.BlockSpec((None, HALO, d), next_map(1)),
            pl.BlockSpec((None, HALO, d), next_map(2)),
            pl.BlockSpec((None, TILE, d), tile_map(0)),
            _mod_spec(d),
            pl.BlockSpec(w_out.shape, lambda i, j: (0, 0)),
            pl.BlockSpec(conv_w.shape, lambda i, j: (0, 0)),
            pl.BlockSpec((1, d), lambda i, j: (0, 0)),
            pl.BlockSpec((n_e, d), lambda i, j: (0, 0)),
        ],
        out_specs=(pl.BlockSpec((None, TILE, d), tile_map(0)),
                   pl.BlockSpec((None, TILE, d), tile_map(0)),
                   pl.BlockSpec((None, n_e, TILE), lambda i, j: (i, 0, j))),
        compiler_params=_cparams(("arbitrary", "arbitrary")),
        name="mix_out",
    )(o_attn, conv, conv, conv, conv, conv, conv, conv, xs, modsel, w_out, conv_w, g2, router_t)


def _moe_ffn_kernel(x_ref, a_ref, wg_ref, wu_ref, wd_ref, o_ref):
    wg = wg_ref[...].astype(BF16)
    wu = wu_ref[...].astype(BF16)
    wd = wd_ref[...].astype(BF16)
    for i in range(x_ref.shape[0]):
        x = x_ref[i]
        gate = jnp.dot(x, wg, preferred_element_type=F32)
        up = jnp.dot(x, wu, preferred_element_type=F32)
        act = (gate * _sigmoid(gate) * up).astype(BF16)
        o_ref[i] = jnp.dot(act, wd, preferred_element_type=F32) * a_ref[i]


def _moe_ffn(xs, aff, w_gate, w_up, w_down):
    b, n_e, r, d = xs.shape
    f = w_gate.shape[2]
    bb = 4 if b % 4 == 0 else 1
    return pl.pallas_call(
        _moe_ffn_kernel,
        out_shape=jax.ShapeDtypeStruct((b, n_e, r, d), F32),
        grid=(n_e, b // bb),
        in_specs=[
            pl.BlockSpec((bb, None, r, d), lambda e, i: (i, e, 0, 0)),
            pl.BlockSpec((bb, None, r, 1), lambda e, i: (i, e, 0, 0)),
            pl.BlockSpec((None, d, f), lambda e, i: (e, 0, 0)),
            pl.BlockSpec((None, d, f), lambda e, i: (e, 0, 0)),
            pl.BlockSpec((None, f, d), lambda e, i: (e, 0, 0)),
        ],
        out_specs=pl.BlockSpec((bb, None, r, d), lambda e, i: (i, e, 0, 0)),
        compiler_params=_cparams(("arbitrary", "arbitrary")),
        name="moe_ffn",
    )(xs, aff, w_gate, w_up, w_down)


def _combine_kernel(x_ref, y_ref, mod_ref, g_ref, o_ref, *, final_norm):
    x2 = x_ref[...] + mod_ref[5:6, :] * y_ref[...]
    o_ref[...] = _rms_rows(x2, g_ref[...]) if final_norm else x2


def _combine(x1, y, modsel, g, *, tile_offset, final_norm):
    b, t, d = x1.shape
    kern = functools.partial(_combine_kernel, final_norm=final_norm)
    return pl.pallas_call(
        kern,
        out_shape=jax.ShapeDtypeStruct((b, t, d), F32),
        grid=(b, t // TILE),
        in_specs=[
            pl.BlockSpec((None, TILE, d), lambda i, j: (i, j, 0)),
            pl.BlockSpec((None, TILE, d), lambda i, j: (i, j, 0)),
            pl.BlockSpec((None, None, 6, d), lambda i, j: (i, jnp.minimum(j + tile_offset, 1), 0, 0)),
            pl.BlockSpec((1, d), lambda i, j: (0, 0)),
        ],
        out_specs=pl.BlockSpec((None, TILE, d), lambda i, j: (i, j, 0)),
        compiler_params=_cparams(("arbitrary", "arbitrary")),
        name="moe_combine",
    )(x1, y, modsel, g)


def _rwkv_feat_kernel(x_ref, xp_ref, xn_ref, mod_ref, g_ref, mu_ref, wr_ref, wk_ref, wv_ref,
                      w1_ref, a1_ref, g1_ref, w2_ref, a2_ref, g2_ref, w0_ref, a0_ref,
                      kk_ref, ka_ref, seg_ref,
                      r_out, v_out, kkn_out, g_out, lw_out, kd_out, bb_out, *, n_tiles):
    t = pl.program_id(1)
    has_prev = (t >= 2).astype(F32)
    has_next = jnp.logical_and(t >= 1, t <= n_tiles - 2).astype(F32)
    g = g_ref[...]
    shift = mod_ref[0:1, :]
    scale = mod_ref[1:2, :]
    h = _ada_rows(x_ref[...], g, shift, scale)
    h_prev = _ada_rows(xp_ref[HALO - 1:HALO, :], g, shift, scale) * has_prev
    h_next = _ada_rows(xn_ref[0:1, :], g, shift, scale) * has_next
    down, up = _shift_rows(h, h_prev, h_next)
    xx = 0.5 * (down + up) - h
    d = h.shape[1]

    r_out[...] = _dot(h + xx * mu_ref[0:1, :], wr_ref[...])
    k = _dot(h + xx * mu_ref[2:3, :], wk_ref[...])
    v_out[...] = _dot(h + xx * mu_ref[3:4, :], wv_ref[...])
    gate = _sigmoid(_dot(h + xx * mu_ref[5:6, :], g1_ref[...]))
    g_out[...] = _dot(gate, g2_ref[...])

    kkp = k * kk_ref[...]
    ss = _split_dot(kkp * kkp, seg_ref[...], 2)
    kkn = kkp * lax.rsqrt(jnp.maximum(ss, 1e-24))
    kkn_out[...] = kkn

    lora_w = _dot(jnp.tanh(_dot(h + xx * mu_ref[1:2, :], w1_ref[...])), w2_ref[...])
    lora_a = _dot(_dot(h + xx * mu_ref[4:5, :], a1_ref[...]), a2_ref[...])
    for di in range(2):
        sl = slice(di * d, (di + 1) * d)
        w_log = -_softplus(-(w0_ref[di:di + 1, :] + lora_w[:, sl])) - 0.5
        lw_out[di] = -jnp.exp(w_log)
        a = _sigmoid(a0_ref[di:di + 1, :] + lora_a[:, sl])
        kd_out[di] = k * (1.0 + (a - 1.0) * ka_ref[...])
        bb_out[di] = a * kkn


def _rwkv_feat(xs, modsel, g, p):
    b, t, d = xs.shape
    n_tiles = t // TILE
    hb = TILE // HALO
    last = t // HALO - 1
    full = lambda a: pl.BlockSpec(a.shape, lambda i, j: (0,) * a.ndim)
    weights = [p["mu"], p["w_r"], p["w_k"], p["w_v"], p["w1"], p["a1"], p["g1"], p["w2"], p["a2"],
               p["g2"], p["w0"], p["a0"], p["k_k"], p["k_a"], p["seg"]]
    tile_spec = pl.BlockSpec((None, TILE, d), lambda i, j: (i, j, 0))
    dir_spec = pl.BlockSpec((2, None, TILE, d), lambda i, j: (0, i, j, 0))
    kern = functools.partial(_rwkv_feat_kernel, n_tiles=n_tiles)
    one = jax.ShapeDtypeStruct((b, t, d), F32)
    two = jax.ShapeDtypeStruct((2, b, t, d), F32)
    return pl.pallas_call(
        kern,
        out_shape=(one, one, one, one, two, two, two),
        grid=(b, n_tiles),
        in_specs=[
            tile_spec,
            pl.BlockSpec((None, HALO, d), lambda i, j: (i, jnp.maximum(j * hb - 1, 0), 0)),
            pl.BlockSpec((None, HALO, d), lambda i, j: (i, jnp.minimum((j + 1) * hb, last), 0)),
            _mod_spec(d),
            pl.BlockSpec((1, d), lambda i, j: (0, 0)),
        ] + [full(a) for a in weights],
        out_specs=(tile_spec, tile_spec, tile_spec, tile_spec, dir_spec, dir_spec, dir_spec),
        compiler_params=_cparams(("arbitrary", "arbitrary")),
        name="rwkv_feat",
    )(xs, xs, xs, modsel, g, *weights)


def _scan_kernel(lw_ref, kd_ref, bb_ref, kk_ref, r_ref, v_ref, y_ref, s_ref):
    rev = pl.program_id(0)
    step = pl.program_id(3)

    @pl.when(step == 0)
    def _():
        s_ref[...] = jnp.zeros_like(s_ref)

    L = CHUNK
    sgn = 1 - 2 * rev
    ti = lax.broadcasted_iota(jnp.int32, (L, LANES), 0)
    ii = lax.broadcasted_iota(jnp.int32, (L, LANES), 1) & (L - 1)
    order = (ti - ii) * sgn
    before = order > 0
    before_eq = order >= 0
    t2 = lax.broadcasted_iota(jnp.int32, (L, L), 0)
    i2 = lax.broadcasted_iota(jnp.int32, (L, L), 1)
    tri = jnp.where((t2 - i2) * sgn >= 0, 1.0, 0.0).astype(BF16)
    r128 = lax.broadcasted_iota(jnp.int32, (LANES, LANES), 0)
    c128 = lax.broadcasted_iota(jnp.int32, (LANES, LANES), 1)
    head_eq = (r128 >> 6) == (c128 >> 6)
    eye = r128 == c128

    def same_block(log2s):
        return (ti >> log2s) == (ii >> log2s)

    def bdiag(y):
        yb = y.astype(BF16)
        return jnp.where(head_eq, jnp.concatenate([yb, yb], axis=0), jnp.zeros((), BF16))

    def hmul(x, y):
        return jnp.dot(x.astype(BF16), bdiag(y), preferred_element_type=F32)

    pairs = range(lw_ref.shape[1] // LANES)
    sls = [slice(p * LANES, (p + 1) * LANES) for p in pairs]

    def each(f, *cols):
        return [f(*args) for args in zip(*cols)]

    lw = [lw_ref[:, sl] for sl in sls]
    kk = [kk_ref[:, sl] for sl in sls]
    bb = [bb_ref[:, sl] for sl in sls]
    kd = [kd_ref[:, sl] for sl in sls]
    r = [r_ref[:, sl] for sl in sls]
    v = [v_ref[:, sl] for sl in sls]

    lam = each(lambda a: _split_dot_left(tri, a), lw)
    tot = each(lambda a: jnp.sum(a, axis=0, keepdims=True), lw)
    rho = each(lambda a: a[L // 2:L // 2 + 1, :], lam)
    ks = each(lambda a, b, c: a * jnp.exp(b - c), kk, lam, lw)
    rs = each(lambda a, b: a * jnp.exp(b), r, lam)
    e_neg = each(lambda a: jnp.exp(-a), rho)
    e_inv = each(lambda a, b: jnp.exp(a - b), rho, lam)
    e_end = each(lambda a, b: jnp.exp(a - b), tot, lam)

    lhs = each(lambda a, b, c: jnp.concatenate([a * c, b * c], axis=0).astype(BF16), ks, rs, e_neg)
    nt = lambda a, w: lax.dot_general(a, w, (((1,), (1,)), ((), ())), preferred_element_type=F32)
    g_b = each(lambda a, b, c: nt(a, bdiag(b * c)), lhs, bb, e_inv)
    g_k = each(lambda a, b, c: nt(a, bdiag(b * c)), lhs, kd, e_inv)
    a_b = each(lambda a: jnp.where(before, a[0:L], 0.0), g_b)
    a_k = each(lambda a: jnp.where(before, a[0:L], 0.0), g_k)
    m_b = each(lambda a: jnp.where(before_eq, a[L:], 0.0), g_b)
    m_k = each(lambda a: jnp.where(before_eq, a[L:], 0.0), g_k)
    akv = each(hmul, a_k, v)

    ident = jnp.where(ti == ii, 1.0, 0.0)
    blk8 = same_block(3)
    n1 = each(lambda a: jnp.where(blk8, -a, 0.0), a_b)
    n2 = each(hmul, n1, n1)
    n3 = each(hmul, n1, n2)
    n4 = each(hmul, n2, n2)
    x = each(lambda a, b, c: ident + a + b + c, n1, n2, n3)
    tinv = each(lambda a, b: a + hmul(a, b), x, n4)
    for log2s in (3, 4, 5):
        off_mask = jnp.logical_and(same_block(log2s + 1), jnp.logical_not(same_block(log2s)))
        off = each(lambda a: jnp.where(off_mask, a, 0.0), a_b)
        corr = each(hmul, off, tinv)
        tinv = each(lambda a, b: a - hmul(a, b), tinv, corr)

    q1 = each(lambda a, b: -hmul(a, b), tinv, ks)
    p0 = each(lambda a, b: -hmul(a, b), tinv, akv)
    q2 = each(lambda a, b, c: a + hmul(b, c), rs, m_b, q1)
    y0 = each(lambda a, b, c, d: hmul(a, b) + hmul(c, d), m_b, p0, m_k, v)
    b_end = each(lambda a, b: a * b, bb, e_end)
    k_end = each(lambda a, b: a * b, kd, e_end)
    m_t = each(lambda a, b, c: jnp.where(head_eq, _dot_tn(a, b), 0.0) + jnp.where(eye, jnp.exp(c), 0.0),
               b_end, q1, tot)
    n0 = each(lambda a, b, c, d: jnp.where(head_eq, _dot_tn(jnp.concatenate([a, b], axis=0),
                                                            jnp.concatenate([c, d], axis=0)), 0.0),
              b_end, k_end, p0, v)

    s0 = [s_ref[p] for p in pairs]
    ys = each(lambda a, b, c: _dot(a, b) + c, q2, s0, y0)
    s1 = each(lambda a, b, c: _dot(a, b) + c, m_t, s0, n0)
    for p in pairs:
        y_ref[:, sls[p]] = ys[p]
        s_ref[p] = s1[p]


def _split_dot_left(tri, x):
    acc = None
    rem = x
    for _ in range(3):
        hi = rem.astype(BF16)
        term = jnp.dot(tri, hi, preferred_element_type=F32)
        acc = term if acc is None else acc + term
        rem = rem - hi.astype(F32)
    return acc


def _rwkv_scan(lw, kd, bb, kk, r, v, n_ctx):
    _, b, t, d = lw.shape
    nc_ctx = n_ctx // CHUNK
    nc = t // CHUNK

    def chunk(rev, j):
        bwd = jnp.where(j < nc_ctx, nc_ctx - 1 - j, nc + nc_ctx - 1 - j)
        return jnp.where(rev == 0, j, bwd)

    dir_spec = pl.BlockSpec((None, None, CHUNK, SCAN_LANES), lambda rv, i, hg, j: (rv, i, chunk(rv, j), hg))
    one_spec = pl.BlockSpec((None, CHUNK, SCAN_LANES), lambda rv, i, hg, j: (i, chunk(rv, j), hg))
    return pl.pallas_call(
        _scan_kernel,
        out_shape=jax.ShapeDtypeStruct((2, b, t, d), F32),
        grid=(2, b, d // SCAN_LANES, nc),
        in_specs=[dir_spec, dir_spec, dir_spec, one_spec, one_spec, one_spec],
        out_specs=dir_spec,
        scratch_shapes=[pltpu.VMEM((SCAN_LANES // LANES, LANES, LANES), F32)],
        compiler_params=_cparams(("arbitrary",) * 4),
        name="rwkv_scan",
    )(lw, kd, bb, kk, r, v)


def _rwkv_out_kernel(yf_ref, yb_ref, r_ref, v_ref, g_ref, kdf_ref, kdb_ref, x_ref, mod_ref,
                     wo_ref, lnw_ref, lnb_ref, rk_ref, seg_ref, g2_ref, router_ref,
                     x1_ref, h2_ref, aff_ref):
    seg = seg_ref[...]
    inv_n = 1.0 / RW_N
    y = yf_ref[...] + yb_ref[...]
    mean = _split_dot(y, seg, 2) * inv_n
    dev = y - mean
    var = _split_dot(dev * dev, seg, 2) * inv_n
    yn = dev * lax.rsqrt(var + GN_EPS) * lnw_ref[...] + lnb_ref[...]
    r = r_ref[...]
    rk = rk_ref[...]
    bonus = _split_dot(r * kdf_ref[...] * rk + r * kdb_ref[...] * rk, seg, 2) * v_ref[...]
    out = _dot((yn + bonus) * g_ref[...], wo_ref[...])
    _residual_and_router(x_ref[...], out, mod_ref, g2_ref, router_ref, x1_ref, h2_ref, aff_ref)


def _rwkv_out(y, r, v, g, kd, xs, modsel, p, g2, router_t, n_ctx):
    b, t, d = xs.shape
    off = n_ctx // TILE
    n = t - n_ctx
    n_e = router_t.shape[0]
    tile = pl.BlockSpec((None, TILE, d), lambda i, j: (i, j + off, 0))
    dir0 = pl.BlockSpec((None, None, TILE, d), lambda i, j: (0, i, j + off, 0))
    dir1 = pl.BlockSpec((None, None, TILE, d), lambda i, j: (1, i, j + off, 0))
    vec = pl.BlockSpec((1, d), lambda i, j: (0, 0))
    out_tile = pl.BlockSpec((None, TILE, d), lambda i, j: (i, j, 0))
    return pl.pallas_call(
        _rwkv_out_kernel,
        out_shape=(jax.ShapeDtypeStruct((b, n, d), F32),
                   jax.ShapeDtypeStruct((b, n, d), BF16),
                   jax.ShapeDtypeStruct((b, n_e, n), F32)),
        grid=(b, n // TILE),
        in_specs=[dir0, dir1, tile, tile, tile, dir0, dir1, tile,
                  pl.BlockSpec((None, None, 6, d), lambda i, j: (i, 1, 0, 0)),
                  pl.BlockSpec((d, d), lambda i, j: (0, 0)),
                  vec, vec, vec,
                  pl.BlockSpec((d, d), lambda i, j: (0, 0)),
                  vec,
                  pl.BlockSpec((n_e, d), lambda i, j: (0, 0))],
        out_specs=(out_tile, out_tile, pl.BlockSpec((None, n_e, TILE), lambda i, j: (i, 0, j))),
        compiler_params=_cparams(("arbitrary", "arbitrary")),
        name="rwkv_out",
    )(y, y, r, v, g, kd, kd, xs, modsel, p["w_o"], p["ln_w"], p["ln_b"], p["r_k"], p["seg"], g2, router_t)


def _route_and_ffn(h2, aff_t, sets, w_gate, w_up, w_down):
    b, t, d = h2.shape
    xs_parts, aff_parts, idx_parts = [], [], []
    for start, length in sets:
        cap = EC_CAPACITY * length // N_EXPERTS
        top_aff, top_idx = lax.top_k(aff_t[:, :, start:start + length], cap)
        top_idx = top_idx + start
        xs_parts.append(jax.vmap(lambda hb, ib: hb[ib])(h2, top_idx))
        aff_parts.append(top_aff)
        idx_parts.append(top_idx)
    xs = jnp.concatenate(xs_parts, axis=2)
    aff = jnp.concatenate(aff_parts, axis=2)
    idx = jnp.concatenate(idx_parts, axis=2)
    out = _moe_ffn(xs, aff[..., None], w_gate, w_up, w_down)
    flat = (jnp.arange(b)[:, None, None] * t + idx).reshape(-1)
    y = jnp.zeros((b * t, d), F32).at[flat].add(out.reshape(-1, d))
    return y.reshape(b, t, d)


def _rope_tables(n_ctx, n):
    rows = n // GRID_W
    row = jnp.broadcast_to(jnp.arange(rows)[:, None], (rows, GRID_W)).reshape(-1).astype(F32)
    col = jnp.broadcast_to(jnp.arange(GRID_W)[None, :], (rows, GRID_W)).reshape(-1).astype(F32)
    pairs = HEAD_DIM // 4
    inv = ROPE_THETA ** (-jnp.arange(pairs, dtype=F32) / pairs)
    ang = jnp.concatenate([row[:, None] * inv, col[:, None] * inv], axis=-1)
    cos, sin = jnp.cos(ang), jnp.sin(ang)
    cos2 = jnp.concatenate([cos, cos], axis=-1)
    sin2 = jnp.concatenate([-sin, sin], axis=-1)
    cos2 = jnp.concatenate([jnp.ones((n_ctx, HEAD_DIM), F32), cos2], axis=0)
    sin2 = jnp.concatenate([jnp.zeros((n_ctx, HEAD_DIM), F32), sin2], axis=0)
    return cos2, sin2


def _block_diag2(a, b):
    z = jnp.zeros_like(a)
    return jnp.concatenate([jnp.concatenate([a, z], axis=1), jnp.concatenate([z, b], axis=1)], axis=0)


def kernel(x, c, ctx, c_ctx, ada_w, ada_b, norm_g, final_g, mix_w_in, mix_w_out, q_norm_g, k_norm_g, conv_w, rw_mu, rw_w_r, rw_w_k, rw_w_v, rw_w_o, rw_w0, rw_w1, rw_w2, rw_a0, rw_a1, rw_a2, rw_g1, rw_g2, rw_k_k, rw_k_a, rw_r_k, rw_ln_w, rw_ln_b, moe_router, moe_w_gate, moe_w_up, moe_w_down):
    b, n, d = x.shape
    n_ctx = ctx.shape[1]
    depth = ada_w.shape[0]
    assert depth == 2 and n_ctx == TILE and n % TILE == 0 and d % SCAN_LANES == 0
    t = n_ctx + n

    xs = jnp.concatenate([ctx, x], axis=1)

    rows = -(-(b + 1) // HALO) * HALO
    cc = jnp.concatenate([c, c_ctx[None, :], jnp.zeros((rows - b - 1, d), F32)], axis=0)
    mod = _ada_mod(cc, ada_w, ada_b)
    mod_lat = mod[:, :b].reshape(depth, b, 1, 6, d)
    mod_ctx = jnp.broadcast_to(mod[:, b].reshape(depth, 1, 1, 6, d), (depth, b, 1, 6, d))
    modsel = jnp.concatenate([mod_ctx, mod_lat], axis=2)

    router_t = jnp.swapaxes(moe_router, 1, 2).astype(BF16)

    cos2, sin2 = _rope_tables(n_ctx, n)
    qkv, conv = _in_proj(xs, modsel[0], norm_g[0, 0][None], mix_w_in[0].astype(BF16),
                         q_norm_g[0][None], k_norm_g[0][None], cos2, sin2)
    o_attn = _attention(qkv, n_ctx)
    x1, h2, aff_t = _mix_out(o_attn, conv, xs, modsel[0], mix_w_out[0].astype(BF16), conv_w[0],
                             norm_g[0, 1][None], router_t[0])
    y = _route_and_ffn(h2, aff_t, [(n_ctx, n), (0, n_ctx)], moe_w_gate[0], moe_w_up[0], moe_w_down[0])
    xs = _combine(x1, y, modsel[0], final_g[None], tile_offset=0, final_norm=False)

    heads = d // RW_N
    seg = jnp.kron(jnp.eye(heads, dtype=F32), jnp.ones((RW_N, RW_N), F32)).astype(BF16)
    p = {
        "mu": rw_mu[0], "w_r": rw_w_r[0].astype(BF16), "w_k": rw_w_k[0].astype(BF16),
        "w_v": rw_w_v[0].astype(BF16), "w_o": rw_w_o[0].astype(BF16),
        "w1": jnp.concatenate([rw_w1[0, 0], rw_w1[0, 1]], axis=1).astype(BF16),
        "a1": jnp.concatenate([rw_a1[0, 0], rw_a1[0, 1]], axis=1).astype(BF16),
        "g1": rw_g1[0].astype(BF16),
        "w2": _block_diag2(rw_w2[0, 0], rw_w2[0, 1]).astype(BF16),
        "a2": _block_diag2(rw_a2[0, 0], rw_a2[0, 1]).astype(BF16),
        "g2": rw_g2[0].astype(BF16),
        "w0": rw_w0[0], "a0": rw_a0[0], "k_k": rw_k_k[0][None], "k_a": rw_k_a[0][None],
        "r_k": rw_r_k[0].reshape(1, d), "ln_w": rw_ln_w[0][None], "ln_b": rw_ln_b[0][None],
        "seg": seg,
    }
    r, v, kk, g, lw, kd, bb = _rwkv_feat(xs, modsel[1], norm_g[1, 0][None], p)
    ys = _rwkv_scan(lw, kd, bb, kk, r, v, n_ctx)
    x1, h2, aff_t = _rwkv_out(ys, r, v, g, kd, xs, modsel[1], p, norm_g[1, 1][None], router_t[1], n_ctx)
    y = _route_and_ffn(h2, aff_t, [(0, n)], moe_w_gate[1], moe_w_up[1], moe_w_down[1])
    return _combine(x1, y, modsel[1], final_g[None], tile_offset=1, final_norm=True)
```

```python
import functools

import jax
import jax.numpy as jnp
from jax import lax
from jax.experimental import pallas as pl
from jax.experimental.pallas import tpu as pltpu

F32 = jnp.float32
BF16 = jnp.bfloat16

NORM_EPS = 1e-6
GN_EPS = 64e-5
GRID_W = 64
ROPE_THETA = 10000.0
HEAD_DIM = 128
N_Q_HEADS = 8
N_KV_HEADS = 2
Q_PER_KV = N_Q_HEADS // N_KV_HEADS
RW_N = 64
N_EXPERTS = 16
EC_CAPACITY = 2
LOG2_E = 1.4426950408889634

LANES = 128
TILE = 256
HALO = 8
CHUNK = 64
SCAN_LANES = 1024
VMEM_LIMIT = 56 * 1024 * 1024


def _cparams(sem):
    return pltpu.CompilerParams(dimension_semantics=sem, vmem_limit_bytes=VMEM_LIMIT)


def _dot(a, b):
    return jnp.dot(a.astype(BF16), b.astype(BF16), preferred_element_type=F32)


def _dot_nt(a, b):
    return lax.dot_general(a.astype(BF16), b.astype(BF16), (((1,), (1,)), ((), ())),
                           preferred_element_type=F32)


def _dot_tn(a, b):
    return lax.dot_general(a.astype(BF16), b.astype(BF16), (((0,), (0,)), ((), ())),
                           preferred_element_type=F32)


def _split_dot(x, w, parts):
    acc = None
    rem = x
    for _ in range(parts):
        hi = rem.astype(BF16)
        term = jnp.dot(hi, w, preferred_element_type=F32)
        acc = term if acc is None else acc + term
        rem = rem - hi.astype(F32)
    return acc


def _rms_rows(x, g):
    ms = jnp.mean(x * x, axis=-1, keepdims=True)
    return x * lax.rsqrt(ms + NORM_EPS) * g


def _ada_rows(x, g, shift, scale):
    return _rms_rows(x, g) * (1.0 + scale) + shift


def _sigmoid(x):
    return 1.0 / (1.0 + jnp.exp(-x))


def _softplus(x):
    return jnp.maximum(x, 0.0) + jnp.log(1.0 + jnp.exp(-jnp.abs(x)))


def _shift_rows(h, prev_row, next_row):
    n = h.shape[0]
    rows = lax.broadcasted_iota(jnp.int32, h.shape, 0)
    down = jnp.where(rows == 0, prev_row, pltpu.roll(h, 1, 0))
    up = jnp.where(rows == n - 1, next_row, pltpu.roll(h, n - 1, 0))
    return down, up


def _ada_mod_kernel(c_ref, w_ref, b_ref, o_ref):
    c = c_ref[...]
    o_ref[...] = _dot(c * _sigmoid(c), w_ref[...]) + b_ref[...]


def _ada_mod(cc, ada_w, ada_b):
    depth, d, n6 = ada_w.shape
    rows = cc.shape[0]
    nb = 1536
    return pl.pallas_call(
        _ada_mod_kernel,
        out_shape=jax.ShapeDtypeStruct((depth, rows, n6), F32),
        grid=(depth, n6 // nb),
        in_specs=[
            pl.BlockSpec((rows, d), lambda l, j: (0, 0)),
            pl.BlockSpec((None, d, nb), lambda l, j: (l, 0, j)),
            pl.BlockSpec((None, 1, nb), lambda l, j: (l, 0, j)),
        ],
        out_specs=pl.BlockSpec((None, rows, nb), lambda l, j: (l, 0, j)),
        compiler_params=_cparams(("arbitrary", "arbitrary")),
        name="ada_mod",
    )(cc, ada_w, ada_b.reshape(depth, 1, n6))


def _mod_spec(d):
    return pl.BlockSpec((None, None, 6, d), lambda b, t: (b, jnp.minimum(t, 1), 0, 0))


def _in_proj_kernel(x_ref, mod_ref, g_ref, w_ref, qg_ref, kg_ref, cos_ref, sin_ref,
                    qkv_ref, conv_ref, *, n_qkv, n_rot_heads, n_q_heads):
    h = _ada_rows(x_ref[...], g_ref[...], mod_ref[0:1, :], mod_ref[1:2, :]).astype(BF16)
    nb = 512
    n_all = w_ref.shape[1]
    cos2 = cos_ref[...]
    sin2 = sin_ref[...]
    for j in range(n_all // nb):
        p = jnp.dot(h, w_ref[:, j * nb:(j + 1) * nb], preferred_element_type=F32)
        if j * nb >= n_qkv:
            conv_ref[:, j * nb - n_qkv:(j + 1) * nb - n_qkv] = p
            continue
        for i in range(nb // HEAD_DIM):
            hd = j * (nb // HEAD_DIM) + i
            t = p[:, i * HEAD_DIM:(i + 1) * HEAD_DIM]
            if hd < n_rot_heads:
                t = _rms_rows(t, qg_ref[...] if hd < n_q_heads else kg_ref[...])
                t = t * cos2 + pltpu.roll(t, HEAD_DIM // 2, 1) * sin2
            if hd < n_q_heads:
                t = t * ((HEAD_DIM ** -0.5) * LOG2_E)
            qkv_ref[:, hd * HEAD_DIM:(hd + 1) * HEAD_DIM] = t.astype(BF16)


def _in_proj(xs, modsel, g, w_in, q_g, k_g, cos2, sin2):
    b, t, d = xs.shape
    n_all = w_in.shape[1]
    n_qkv = (N_Q_HEADS + 2 * N_KV_HEADS) * HEAD_DIM
    kern = functools.partial(_in_proj_kernel, n_qkv=n_qkv, n_rot_heads=N_Q_HEADS + N_KV_HEADS,
                             n_q_heads=N_Q_HEADS)
    return pl.pallas_call(
        kern,
        out_shape=(jax.ShapeDtypeStruct((b, t, n_qkv), BF16),
                   jax.ShapeDtypeStruct((b, t, n_all - n_qkv), F32)),
        grid=(b, t // TILE),
        in_specs=[
            pl.BlockSpec((None, TILE, d), lambda i, j: (i, j, 0)),
            _mod_spec(d),
            pl.BlockSpec((1, d), lambda i, j: (0, 0)),
            pl.BlockSpec((d, n_all), lambda i, j: (0, 0)),
            pl.BlockSpec((1, HEAD_DIM), lambda i, j: (0, 0)),
            pl.BlockSpec((1, HEAD_DIM), lambda i, j: (0, 0)),
            pl.BlockSpec((TILE, HEAD_DIM), lambda i, j: (j, 0)),
            pl.BlockSpec((TILE, HEAD_DIM), lambda i, j: (j, 0)),
        ],
        out_specs=(pl.BlockSpec((None, TILE, n_qkv), lambda i, j: (i, j, 0)),
                   pl.BlockSpec((None, TILE, n_all - n_qkv), lambda i, j: (i, j, 0))),
        compiler_params=_cparams(("arbitrary", "arbitrary")),
        name="in_proj",
    )(xs, modsel, g, w_in, q_g, k_g, cos2, sin2)


def _attn_kernel(q_ref, k_ref, v_ref, o_ref, *, n_ctx):
    heads = [slice(g * HEAD_DIM, (g + 1) * HEAD_DIM) for g in range(Q_PER_KV)]

    def attend(nk):
        k = k_ref[0:nk, :]
        v = v_ref[0:nk, :]
        s = _dot_nt(q_ref[:, heads[0]], k)
        for g in range(Q_PER_KV):
            s_next = _dot_nt(q_ref[:, heads[g + 1]], k) if g + 1 < Q_PER_KV else None
            p = jnp.exp2(s - jnp.max(s, axis=-1, keepdims=True))
            l = jnp.sum(p, axis=-1, keepdims=True)
            o_ref[:, heads[g]] = jnp.dot(p.astype(BF16), v, preferred_element_type=F32) / l
            s = s_next

    is_ctx = pl.program_id(2) == 0

    @pl.when(is_ctx)
    def _():
        attend(n_ctx)

    @pl.when(jnp.logical_not(is_ctx))
    def _():
        attend(k_ref.shape[0])


def _attention(qkv, n_ctx):
    b, t, _ = qkv.shape
    kern = functools.partial(_attn_kernel, n_ctx=n_ctx)
    return pl.pallas_call(
        kern,
        out_shape=jax.ShapeDtypeStruct((b, t, N_Q_HEADS * HEAD_DIM), F32),
        grid=(b, N_KV_HEADS, t // TILE),
        in_specs=[
            pl.BlockSpec((None, TILE, Q_PER_KV * HEAD_DIM), lambda i, kv, j: (i, j, kv)),
            pl.BlockSpec((None, t, HEAD_DIM), lambda i, kv, j: (i, 0, N_Q_HEADS + kv)),
            pl.BlockSpec((None, t, HEAD_DIM), lambda i, kv, j: (i, 0, N_Q_HEADS + N_KV_HEADS + kv)),
        ],
        out_specs=pl.BlockSpec((None, TILE, Q_PER_KV * HEAD_DIM), lambda i, kv, j: (i, j, kv)),
        compiler_params=_cparams(("arbitrary",) * 3),
        name="attention",
    )(qkv, qkv, qkv)


def _residual_and_router(x, y, mod_ref, g2_ref, router_ref, x1_ref, h2_ref, aff_ref):
    x1 = x + mod_ref[2:3, :] * y
    x1_ref[...] = x1
    h2 = _ada_rows(x1, g2_ref[...], mod_ref[3:4, :], mod_ref[4:5, :]).astype(BF16)
    h2_ref[...] = h2
    logits = _dot_nt(router_ref[...], h2)
    e = jnp.exp(logits - jnp.max(logits, axis=0, keepdims=True))
    aff_ref[...] = e / jnp.sum(e, axis=0, keepdims=True)


def _mix_out_kernel(oa_ref, gb_ref, gc_ref, xc_ref, gcp_ref, xcp_ref, gcn_ref, xcn_ref,
                    x_ref, mod_ref, wout_ref, cw_ref, g2_ref, router_ref,
                    x1_ref, h2_ref, aff_ref, *, n_tiles):
    t = pl.program_id(1)
    has_prev = (t >= 2).astype(F32)
    has_next = jnp.logical_and(t >= 1, t <= n_tiles - 2).astype(F32)
    u = gc_ref[...] * xc_ref[...]
    u_prev = gcp_ref[HALO - 1:HALO, :] * xcp_ref[HALO - 1:HALO, :] * has_prev
    u_next = gcn_ref[0:1, :] * xcn_ref[0:1, :] * has_next
    down, up = _shift_rows(u, u_prev, u_next)
    o_conv = gb_ref[...] * (down * cw_ref[0:1, :] + u * cw_ref[1:2, :] + up * cw_ref[2:3, :])
    d_attn = oa_ref.shape[1]
    y = _dot(oa_ref[...], wout_ref[0:d_attn, :]) + _dot(o_conv, wout_ref[d_attn:, :])
    _residual_and_router(x_ref[...], y, mod_ref, g2_ref, router_ref, x1_ref, h2_ref, aff_ref)


def _mix_out(o_attn, conv, xs, modsel, w_out, conv_w, g2, router_t):
    b, t, d = xs.shape
    n_tiles = t // TILE
    hb = TILE // HALO
    last = t // HALO - 1
    prev_map = lambda col: (lambda i, j: (i, jnp.maximum(j * hb - 1, 0), col))
    next_map = lambda col: (lambda i, j: (i, jnp.minimum((j + 1) * hb, last), col))
    tile_map = lambda col: (lambda i, j: (i, j, col))
    n_e = router_t.shape[0]
    kern = functools.partial(_mix_out_kernel, n_tiles=n_tiles)
    return pl.pallas_call(
        kern,
        out_shape=(jax.ShapeDtypeStruct((b, t, d), F32),
                   jax.ShapeDtypeStruct((b, t, d), BF16),
                   jax.ShapeDtypeStruct((b, n_e, t), F32)),
        grid=(b, n_tiles),
        in_specs=[
            pl.BlockSpec((None, TILE, d), tile_map(0)),
            pl.BlockSpec((None, TILE, d), tile_map(0)),
            pl.BlockSpec((None, TILE, d), tile_map(1)),
            pl.BlockSpec((None, TILE, d), tile_map(2)),
            pl.BlockSpec((None, HALO, d), prev_map(1)),
            pl.BlockSpec((None, HALO, d), prev_map(2)),
            pl.BlockSpec((None, HALO, d), next_map(1)),
            pl.BlockSpec((None, HALO, d), next_map(2)),
            pl.BlockSpec((None, TILE, d), tile_map(0)),
            _mod_spec(d),
            pl.BlockSpec(w_out.shape, lambda i, j: (0, 0)),
            pl.BlockSpec(conv_w.shape, lambda i, j: (0, 0)),
            pl.BlockSpec((1, d), lambda i, j: (0, 0)),
            pl.BlockSpec((n_e, d), lambda i, j: (0, 0)),
        ],
        out_specs=(pl.BlockSpec((None, TILE, d), tile_map(0)),
                   pl.BlockSpec((None, TILE, d), tile_map(0)),
                   pl.BlockSpec((None, n_e, TILE), lambda i, j: (i, 0, j))),
        compiler_params=_cparams(("arbitrary", "arbitrary")),
        name="mix_out",
    )(o_attn, conv, conv, conv, conv, conv, conv, conv, xs, modsel, w_out, conv_w, g2, router_t)


def _moe_ffn_kernel(slot_ref, aff_ref, h_ref, wg_ref, wu_ref, wd_ref, o_ref, wg_s, wu_s, wd_s):
    @pl.when(pl.program_id(1) == 0)
    def _():
        wg_s[...] = wg_ref[...].astype(BF16)
        wu_s[...] = wu_ref[...].astype(BF16)
        wd_s[...] = wd_ref[...].astype(BF16)

    r = o_ref.shape[0]
    t = h_ref.shape[0]
    rows = lax.broadcasted_iota(jnp.int32, (r, t), 0).astype(F32)
    hit = slot_ref[...] == rows
    a_row = jnp.sum(jnp.where(hit, aff_ref[...], 0.0), axis=-1, keepdims=True)
    x = jnp.dot(jnp.where(hit, 1.0, 0.0).astype(BF16), h_ref[...],
                preferred_element_type=F32).astype(BF16)
    gate = jnp.dot(x, wg_s[...], preferred_element_type=F32)
    up = jnp.dot(x, wu_s[...], preferred_element_type=F32)
    act = (gate * _sigmoid(gate) * up).astype(BF16)
    o_ref[...] = (jnp.dot(act, wd_s[...], preferred_element_type=F32) * a_row).astype(BF16)


def _moe_ffn(slots, aff_t, h2, w_gate, w_up, w_down, n_slots):
    b, n_e, t = slots.shape
    d = h2.shape[2]
    f = w_gate.shape[2]
    row = pl.BlockSpec((None, None, 1, t), lambda e, i: (i, e, 0, 0))
    return pl.pallas_call(
        _moe_ffn_kernel,
        out_shape=jax.ShapeDtypeStruct((b, n_e, n_slots, d), BF16),
        grid=(n_e, b),
        in_specs=[
            row, row,
            pl.BlockSpec((None, t, d), lambda e, i: (i, 0, 0)),
            pl.BlockSpec((None, d, f), lambda e, i: (e, 0, 0)),
            pl.BlockSpec((None, d, f), lambda e, i: (e, 0, 0)),
            pl.BlockSpec((None, f, d), lambda e, i: (e, 0, 0)),
        ],
        out_specs=pl.BlockSpec((None, None, n_slots, d), lambda e, i: (i, e, 0, 0)),
        scratch_shapes=[pltpu.VMEM((d, f), BF16), pltpu.VMEM((d, f), BF16), pltpu.VMEM((f, d), BF16)],
        compiler_params=_cparams(("arbitrary", "arbitrary")),
        name="moe_ffn",
    )(slots.reshape(b, n_e, 1, t), aff_t.reshape(b, n_e, 1, t), h2, w_gate, w_up, w_down)


def _prefix_count(mask, tri):
    blk = tri.shape[0]
    carry = jnp.zeros((mask.shape[0], 1), F32)
    parts = []
    for j in range(mask.shape[1] // blk):
        m = mask[:, j * blk:(j + 1) * blk]
        parts.append(jnp.dot(m.astype(BF16), tri, preferred_element_type=F32) + carry)
        carry = carry + jnp.sum(m, axis=-1, keepdims=True)
    return jnp.concatenate(parts, axis=-1)


def _route_kernel(aff_ref, slot_ref, *, sets):
    blk = 256
    tri = jnp.where(lax.broadcasted_iota(jnp.int32, (blk, blk), 0)
                    <= lax.broadcasted_iota(jnp.int32, (blk, blk), 1), 1.0, 0.0).astype(BF16)
    for start, length, slot0 in sets:
        cap = float(EC_CAPACITY * length // N_EXPERTS)
        bits = pltpu.bitcast(aff_ref[:, start:start + length], jnp.int32)
        thr = jnp.zeros((bits.shape[0], 1), jnp.int32)
        for bit in range(30, -1, -1):
            cand = thr | (1 << bit)
            cnt = jnp.sum(jnp.where(bits >= cand, 1.0, 0.0), axis=-1, keepdims=True)
            thr = jnp.where(cnt >= cap, cand, thr)
        above = jnp.where(bits > thr, 1.0, 0.0)
        tied = jnp.where(bits == thr, 1.0, 0.0)
        need = cap - jnp.sum(above, axis=-1, keepdims=True)
        chosen = above + jnp.where(_prefix_count(tied, tri) <= need, tied, 0.0)
        slot = _prefix_count(chosen, tri) - 1.0 + slot0
        slot_ref[:, start:start + length] = jnp.where(chosen > 0.0, slot, -1.0)


def _route(aff_t, sets):
    b, n_e, t = aff_t.shape
    kern = functools.partial(_route_kernel, sets=tuple(sets))
    spec = pl.BlockSpec((None, n_e, t), lambda i: (i, 0, 0))
    return pl.pallas_call(
        kern,
        out_shape=jax.ShapeDtypeStruct((b, n_e, t), F32),
        grid=(b,),
        in_specs=[spec],
        out_specs=spec,
        compiler_params=_cparams(("arbitrary",)),
        name="moe_route",
    )(aff_t)


def _combine_kernel(slot_ref, o_ref, x_ref, mod_ref, g_ref, out_ref, *, final_norm):
    n_e, r, _ = o_ref.shape
    rows = lax.broadcasted_iota(jnp.int32, (r, slot_ref.shape[1]), 0).astype(F32)
    y = None
    for e in range(n_e):
        onehot = jnp.where(slot_ref[e:e + 1, :] == rows, 1.0, 0.0).astype(BF16)
        term = lax.dot_general(onehot, o_ref[e], (((0,), (0,)), ((), ())), preferred_element_type=F32)
        y = term if y is None else y + term
    x2 = x_ref[...] + mod_ref[5:6, :] * y
    out_ref[...] = _rms_rows(x2, g_ref[...]) if final_norm else x2


def _combine(slots, outs, x1, modsel, g, *, tile_offset, final_norm):
    b, t, d = x1.shape
    _, n_e, r, _ = outs.shape
    kern = functools.partial(_combine_kernel, final_norm=final_norm)
    return pl.pallas_call(
        kern,
        out_shape=jax.ShapeDtypeStruct((b, t, d), F32),
        grid=(b, t // TILE),
        in_specs=[
            pl.BlockSpec((None, n_e, TILE), lambda i, j: (i, 0, j)),
            pl.BlockSpec((None, n_e, r, d), lambda i, j: (i, 0, 0, 0)),
            pl.BlockSpec((None, TILE, d), lambda i, j: (i, j, 0)),
            pl.BlockSpec((None, None, 6, d), lambda i, j: (i, jnp.minimum(j + tile_offset, 1), 0, 0)),
            pl.BlockSpec((1, d), lambda i, j: (0, 0)),
        ],
        out_specs=pl.BlockSpec((None, TILE, d), lambda i, j: (i, j, 0)),
        compiler_params=_cparams(("arbitrary", "arbitrary")),
        name="moe_combine",
    )(slots, outs, x1, modsel, g)


def _rwkv_feat_kernel(x_ref, xp_ref, xn_ref, mod_ref, g_ref, mu_ref, wr_ref, wk_ref, wv_ref,
                      w1_ref, a1_ref, g1_ref, w2_ref, a2_ref, g2_ref, w0_ref, a0_ref,
                      kk_ref, ka_ref, seg_ref,
                      r_out, v_out, kkn_out, g_out, lw_out, kd_out, bb_out, *, n_tiles):
    t = pl.program_id(1)
    has_prev = (t >= 2).astype(F32)
    has_next = jnp.logical_and(t >= 1, t <= n_tiles - 2).astype(F32)
    g = g_ref[...]
    shift = mod_ref[0:1, :]
    scale = mod_ref[1:2, :]
    h = _ada_rows(x_ref[...], g, shift, scale)
    h_prev = _ada_rows(xp_ref[HALO - 1:HALO, :], g, shift, scale) * has_prev
    h_next = _ada_rows(xn_ref[0:1, :], g, shift, scale) * has_next
    down, up = _shift_rows(h, h_prev, h_next)
    xx = 0.5 * (down + up) - h
    d = h.shape[1]

    r_out[...] = _dot(h + xx * mu_ref[0:1, :], wr_ref[...])
    k = _dot(h + xx * mu_ref[2:3, :], wk_ref[...])
    v_out[...] = _dot(h + xx * mu_ref[3:4, :], wv_ref[...])
    gate = _sigmoid(_dot(h + xx * mu_ref[5:6, :], g1_ref[...]))
    g_out[...] = _dot(gate, g2_ref[...])

    kkp = k * kk_ref[...]
    ss = _split_dot(kkp * kkp, seg_ref[...], 2)
    kkn = kkp * lax.rsqrt(jnp.maximum(ss, 1e-24))
    kkn_out[...] = kkn

    lora_w = _dot(jnp.tanh(_dot(h + xx * mu_ref[1:2, :], w1_ref[...])), w2_ref[...])
    lora_a = _dot(_dot(h + xx * mu_ref[4:5, :], a1_ref[...]), a2_ref[...])
    for di in range(2):
        sl = slice(di * d, (di + 1) * d)
        w_log = -_softplus(-(w0_ref[di:di + 1, :] + lora_w[:, sl])) - 0.5
        lw_out[di] = -jnp.exp(w_log)
        a = _sigmoid(a0_ref[di:di + 1, :] + lora_a[:, sl])
        kd_out[di] = k * (1.0 + (a - 1.0) * ka_ref[...])
        bb_out[di] = a * kkn


def _rwkv_feat(xs, modsel, g, p):
    b, t, d = xs.shape
    n_tiles = t // TILE
    hb = TILE // HALO
    last = t // HALO - 1
    full = lambda a: pl.BlockSpec(a.shape, lambda i, j: (0,) * a.ndim)
    weights = [p["mu"], p["w_r"], p["w_k"], p["w_v"], p["w1"], p["a1"], p["g1"], p["w2"], p["a2"],
               p["g2"], p["w0"], p["a0"], p["k_k"], p["k_a"], p["seg"]]
    tile_spec = pl.BlockSpec((None, TILE, d), lambda i, j: (i, j, 0))
    dir_spec = pl.BlockSpec((2, None, TILE, d), lambda i, j: (0, i, j, 0))
    kern = functools.partial(_rwkv_feat_kernel, n_tiles=n_tiles)
    one = jax.ShapeDtypeStruct((b, t, d), F32)
    two = jax.ShapeDtypeStruct((2, b, t, d), F32)
    return pl.pallas_call(
        kern,
        out_shape=(one, one, one, one, two, two, two),
        grid=(b, n_tiles),
        in_specs=[
            tile_spec,
            pl.BlockSpec((None, HALO, d), lambda i, j: (i, jnp.maximum(j * hb - 1, 0), 0)),
            pl.BlockSpec((None, HALO, d), lambda i, j: (i, jnp.minimum((j + 1) * hb, last), 0)),
            _mod_spec(d),
            pl.BlockSpec((1, d), lambda i, j: (0, 0)),
        ] + [full(a) for a in weights],
        out_specs=(tile_spec, tile_spec, tile_spec, tile_spec, dir_spec, dir_spec, dir_spec),
        compiler_params=_cparams(("arbitrary", "arbitrary")),
        name="rwkv_feat",
    )(xs, xs, xs, modsel, g, *weights)


def _scan_kernel(lw_ref, kd_ref, bb_ref, kk_ref, r_ref, v_ref, y_ref, s_ref):
    rev = pl.program_id(0)
    step = pl.program_id(3)

    @pl.when(step == 0)
    def _():
        s_ref[...] = jnp.zeros_like(s_ref)

    L = CHUNK
    sgn = 1 - 2 * rev
    ti = lax.broadcasted_iota(jnp.int32, (L, LANES), 0)
    ii = lax.broadcasted_iota(jnp.int32, (L, LANES), 1) & (L - 1)
    order = (ti - ii) * sgn
    before = order > 0
    before_eq = order >= 0
    t2 = lax.broadcasted_iota(jnp.int32, (L, L), 0)
    i2 = lax.broadcasted_iota(jnp.int32, (L, L), 1)
    tri = jnp.where((t2 - i2) * sgn >= 0, 1.0, 0.0).astype(BF16)
    r128 = lax.broadcasted_iota(jnp.int32, (LANES, LANES), 0)
    c128 = lax.broadcasted_iota(jnp.int32, (LANES, LANES), 1)
    head_eq = (r128 >> 6) == (c128 >> 6)
    eye = r128 == c128

    def same_block(log2s):
        return (ti >> log2s) == (ii >> log2s)

    def bdiag(y):
        yb = y.astype(BF16)
        return jnp.where(head_eq, jnp.concatenate([yb, yb], axis=0), jnp.zeros((), BF16))

    def hmul(x, y):
        return jnp.dot(x.astype(BF16), bdiag(y), preferred_element_type=F32)

    pairs = range(lw_ref.shape[1] // LANES)
    sls = [slice(p * LANES, (p + 1) * LANES) for p in pairs]

    def each(f, *cols):
        return [f(*args) for args in zip(*cols)]

    lw = [lw_ref[:, sl] for sl in sls]
    kk = [kk_ref[:, sl] for sl in sls]
    bb = [bb_ref[:, sl] for sl in sls]
    kd = [kd_ref[:, sl] for sl in sls]
    r = [r_ref[:, sl] for sl in sls]
    v = [v_ref[:, sl] for sl in sls]

    lam = each(lambda a: _split_dot_left(tri, a), lw)
    tot = each(lambda a: jnp.sum(a, axis=0, keepdims=True), lw)
    rho = each(lambda a: a[L // 2:L // 2 + 1, :], lam)
    ks = each(lambda a, b, c: a * jnp.exp(b - c), kk, lam, lw)
    rs = each(lambda a, b: a * jnp.exp(b), r, lam)
    e_neg = each(lambda a: jnp.exp(-a), rho)
    e_inv = each(lambda a, b: jnp.exp(a - b), rho, lam)
    e_end = each(lambda a, b: jnp.exp(a - b), tot, lam)

    lhs = each(lambda a, b, c: jnp.concatenate([a * c, b * c], axis=0).astype(BF16), ks, rs, e_neg)
    nt = lambda a, w: lax.dot_general(a, w, (((1,), (1,)), ((), ())), preferred_element_type=F32)
    g_b = each(lambda a, b, c: nt(a, bdiag(b * c)), lhs, bb, e_inv)
    g_k = each(lambda a, b, c: nt(a, bdiag(b * c)), lhs, kd, e_inv)
    a_b = each(lambda a: jnp.where(before, a[0:L], 0.0), g_b)
    a_k = each(lambda a: jnp.where(before, a[0:L], 0.0), g_k)
    m_b = each(lambda a: jnp.where(before_eq, a[L:], 0.0), g_b)
    m_k = each(lambda a: jnp.where(before_eq, a[L:], 0.0), g_k)
    akv = each(hmul, a_k, v)

    ident = jnp.where(ti == ii, 1.0, 0.0)
    blk8 = same_block(3)
    n1 = each(lambda a: jnp.where(blk8, -a, 0.0), a_b)
    n2 = each(hmul, n1, n1)
    n3 = each(hmul, n1, n2)
    n4 = each(hmul, n2, n2)
    x = each(lambda a, b, c: ident + a + b + c, n1, n2, n3)
    tinv = each(lambda a, b: a + hmul(a, b), x, n4)
    for log2s in (3, 4, 5):
        off_mask = jnp.logical_and(same_block(log2s + 1), jnp.logical_not(same_block(log2s)))
        off = each(lambda a: jnp.where(off_mask, a, 0.0), a_b)
        corr = each(hmul, off, tinv)
        tinv = each(lambda a, b: a - hmul(a, b), tinv, corr)

    q1 = each(lambda a, b: -hmul(a, b), tinv, ks)
    p0 = each(lambda a, b: -hmul(a, b), tinv, akv)
    q2 = each(lambda a, b, c: a + hmul(b, c), rs, m_b, q1)
    y0 = each(lambda a, b, c, d: hmul(a, b) + hmul(c, d), m_b, p0, m_k, v)
    b_end = each(lambda a, b: a * b, bb, e_end)
    k_end = each(lambda a, b: a * b, kd, e_end)
    m_t = each(lambda a, b, c: jnp.where(head_eq, _dot_tn(a, b), 0.0) + jnp.where(eye, jnp.exp(c), 0.0),
               b_end, q1, tot)
    n0 = each(lambda a, b, c, d: jnp.where(head_eq, _dot_tn(jnp.concatenate([a, b], axis=0),
                                                            jnp.concatenate([c, d], axis=0)), 0.0),
              b_end, k_end, p0, v)

    s0 = [s_ref[p] for p in pairs]
    ys = each(lambda a, b, c: _dot(a, b) + c, q2, s0, y0)
    s1 = each(lambda a, b, c: _dot(a, b) + c, m_t, s0, n0)
    for p in pairs:
        y_ref[:, sls[p]] = ys[p]
        s_ref[p] = s1[p]


def _split_dot_left(tri, x):
    acc = None
    rem = x
    for _ in range(3):
        hi = rem.astype(BF16)
        term = jnp.dot(tri, hi, preferred_element_type=F32)
        acc = term if acc is None else acc + term
        rem = rem - hi.astype(F32)
    return acc


def _rwkv_scan(lw, kd, bb, kk, r, v, n_ctx):
    _, b, t, d = lw.shape
    nc_ctx = n_ctx // CHUNK
    nc = t // CHUNK

    def chunk(rev, j):
        bwd = jnp.where(j < nc_ctx, nc_ctx - 1 - j, nc + nc_ctx - 1 - j)
        return jnp.where(rev == 0, j, bwd)

    dir_spec = pl.BlockSpec((None, None, CHUNK, SCAN_LANES), lambda rv, i, hg, j: (rv, i, chunk(rv, j), hg))
    one_spec = pl.BlockSpec((None, CHUNK, SCAN_LANES), lambda rv, i, hg, j: (i, chunk(rv, j), hg))
    return pl.pallas_call(
        _scan_kernel,
        out_shape=jax.ShapeDtypeStruct((2, b, t, d), F32),
        grid=(2, b, d // SCAN_LANES, nc),
        in_specs=[dir_spec, dir_spec, dir_spec, one_spec, one_spec, one_spec],
        out_specs=dir_spec,
        scratch_shapes=[pltpu.VMEM((SCAN_LANES // LANES, LANES, LANES), F32)],
        compiler_params=_cparams(("arbitrary",) * 4),
        name="rwkv_scan",
    )(lw, kd, bb, kk, r, v)


def _rwkv_out_kernel(yf_ref, yb_ref, r_ref, v_ref, g_ref, kdf_ref, kdb_ref, x_ref, mod_ref,
                     wo_ref, lnw_ref, lnb_ref, rk_ref, seg_ref, g2_ref, router_ref,
                     x1_ref, h2_ref, aff_ref):
    seg = seg_ref[...]
    inv_n = 1.0 / RW_N
    y = yf_ref[...] + yb_ref[...]
    mean = _split_dot(y, seg, 2) * inv_n
    dev = y - mean
    var = _split_dot(dev * dev, seg, 2) * inv_n
    yn = dev * lax.rsqrt(var + GN_EPS) * lnw_ref[...] + lnb_ref[...]
    r = r_ref[...]
    rk = rk_ref[...]
    bonus = _split_dot(r * kdf_ref[...] * rk + r * kdb_ref[...] * rk, seg, 2) * v_ref[...]
    out = _dot((yn + bonus) * g_ref[...], wo_ref[...])
    _residual_and_router(x_ref[...], out, mod_ref, g2_ref, router_ref, x1_ref, h2_ref, aff_ref)


def _rwkv_out(y, r, v, g, kd, xs, modsel, p, g2, router_t, n_ctx):
    b, t, d = xs.shape
    off = n_ctx // TILE
    n = t - n_ctx
    n_e = router_t.shape[0]
    tile = pl.BlockSpec((None, TILE, d), lambda i, j: (i, j + off, 0))
    dir0 = pl.BlockSpec((None, None, TILE, d), lambda i, j: (0, i, j + off, 0))
    dir1 = pl.BlockSpec((None, None, TILE, d), lambda i, j: (1, i, j + off, 0))
    vec = pl.BlockSpec((1, d), lambda i, j: (0, 0))
    out_tile = pl.BlockSpec((None, TILE, d), lambda i, j: (i, j, 0))
    return pl.pallas_call(
        _rwkv_out_kernel,
        out_shape=(jax.ShapeDtypeStruct((b, n, d), F32),
                   jax.ShapeDtypeStruct((b, n, d), BF16),
                   jax.ShapeDtypeStruct((b, n_e, n), F32)),
        grid=(b, n // TILE),
        in_specs=[dir0, dir1, tile, tile, tile, dir0, dir1, tile,
                  pl.BlockSpec((None, None, 6, d), lambda i, j: (i, 1, 0, 0)),
                  pl.BlockSpec((d, d), lambda i, j: (0, 0)),
                  vec, vec, vec,
                  pl.BlockSpec((d, d), lambda i, j: (0, 0)),
                  vec,
                  pl.BlockSpec((n_e, d), lambda i, j: (0, 0))],
        out_specs=(out_tile, out_tile, pl.BlockSpec((None, n_e, TILE), lambda i, j: (i, 0, j))),
        compiler_params=_cparams(("arbitrary", "arbitrary")),
        name="rwkv_out",
    )(y, y, r, v, g, kd, kd, xs, modsel, p["w_o"], p["ln_w"], p["ln_b"], p["r_k"], p["seg"], g2, router_t)


def _moe_block(x1, h2, aff_t, token_sets, modsel, g, w_gate, w_up, w_down, *, tile_offset, final_norm):
    sets, n_slots = [], 0
    for start, length in token_sets:
        sets.append((start, length, float(n_slots)))
        n_slots += EC_CAPACITY * length // N_EXPERTS
    slots = _route(aff_t, sets)
    outs = _moe_ffn(slots, aff_t, h2, w_gate, w_up, w_down, n_slots)
    return _combine(slots, outs, x1, modsel, g, tile_offset=tile_offset, final_norm=final_norm)


def _rope_tables(n_ctx, n):
    rows = n // GRID_W
    row = jnp.broadcast_to(jnp.arange(rows)[:, None], (rows, GRID_W)).reshape(-1).astype(F32)
    col = jnp.broadcast_to(jnp.arange(GRID_W)[None, :], (rows, GRID_W)).reshape(-1).astype(F32)
    pairs = HEAD_DIM // 4
    inv = ROPE_THETA ** (-jnp.arange(pairs, dtype=F32) / pairs)
    ang = jnp.concatenate([row[:, None] * inv, col[:, None] * inv], axis=-1)
    cos, sin = jnp.cos(ang), jnp.sin(ang)
    cos2 = jnp.concatenate([cos, cos], axis=-1)
    sin2 = jnp.concatenate([-sin, sin], axis=-1)
    cos2 = jnp.concatenate([jnp.ones((n_ctx, HEAD_DIM), F32), cos2], axis=0)
    sin2 = jnp.concatenate([jnp.zeros((n_ctx, HEAD_DIM), F32), sin2], axis=0)
    return cos2, sin2


def _block_diag2(a, b):
    z = jnp.zeros_like(a)
    return jnp.concatenate([jnp.concatenate([a, z], axis=1), jnp.concatenate([z, b], axis=1)], axis=0)


def kernel(x, c, ctx, c_ctx, ada_w, ada_b, norm_g, final_g, mix_w_in, mix_w_out, q_norm_g, k_norm_g, conv_w, rw_mu, rw_w_r, rw_w_k, rw_w_v, rw_w_o, rw_w0, rw_w1, rw_w2, rw_a0, rw_a1, rw_a2, rw_g1, rw_g2, rw_k_k, rw_k_a, rw_r_k, rw_ln_w, rw_ln_b, moe_router, moe_w_gate, moe_w_up, moe_w_down):
    b, n, d = x.shape
    n_ctx = ctx.shape[1]
    depth = ada_w.shape[0]
    assert depth == 2 and n_ctx == TILE and n % TILE == 0 and d % SCAN_LANES == 0
    t = n_ctx + n

    xs = jnp.concatenate([ctx, x], axis=1)

    rows = -(-(b + 1) // HALO) * HALO
    cc = jnp.concatenate([c, c_ctx[None, :], jnp.zeros((rows - b - 1, d), F32)], axis=0)
    mod = _ada_mod(cc, ada_w, ada_b)
    mod_lat = mod[:, :b].reshape(depth, b, 1, 6, d)
    mod_ctx = jnp.broadcast_to(mod[:, b].reshape(depth, 1, 1, 6, d), (depth, b, 1, 6, d))
    modsel = jnp.concatenate([mod_ctx, mod_lat], axis=2)

    router_t = jnp.swapaxes(moe_router, 1, 2).astype(BF16)

    cos2, sin2 = _rope_tables(n_ctx, n)
    qkv, conv = _in_proj(xs, modsel[0], norm_g[0, 0][None], mix_w_in[0].astype(BF16),
                         q_norm_g[0][None], k_norm_g[0][None], cos2, sin2)
    o_attn = _attention(qkv, n_ctx)
    x1, h2, aff_t = _mix_out(o_attn, conv, xs, modsel[0], mix_w_out[0].astype(BF16), conv_w[0],
                             norm_g[0, 1][None], router_t[0])
    xs = _moe_block(x1, h2, aff_t, [(n_ctx, n), (0, n_ctx)], modsel[0], final_g[None],
                    moe_w_gate[0], moe_w_up[0], moe_w_down[0], tile_offset=0, final_norm=False)

    heads = d // RW_N
    seg = jnp.kron(jnp.eye(heads, dtype=F32), jnp.ones((RW_N, RW_N), F32)).astype(BF16)
    p = {
        "mu": rw_mu[0], "w_r": rw_w_r[0].astype(BF16), "w_k": rw_w_k[0].astype(BF16),
        "w_v": rw_w_v[0].astype(BF16), "w_o": rw_w_o[0].astype(BF16),
        "w1": jnp.concatenate([rw_w1[0, 0], rw_w1[0, 1]], axis=1).astype(BF16),
        "a1": jnp.concatenate([rw_a1[0, 0], rw_a1[0, 1]], axis=1).astype(BF16),
        "g1": rw_g1[0].astype(BF16),
        "w2": _block_diag2(rw_w2[0, 0], rw_w2[0, 1]).astype(BF16),
        "a2": _block_diag2(rw_a2[0, 0], rw_a2[0, 1]).astype(BF16),
        "g2": rw_g2[0].astype(BF16),
        "w0": rw_w0[0], "a0": rw_a0[0], "k_k": rw_k_k[0][None], "k_a": rw_k_a[0][None],
        "r_k": rw_r_k[0].reshape(1, d), "ln_w": rw_ln_w[0][None], "ln_b": rw_ln_b[0][None],
        "seg": seg,
    }
    r, v, kk, g, lw, kd, bb = _rwkv_feat(xs, modsel[1], norm_g[1, 0][None], p)
    ys = _rwkv_scan(lw, kd, bb, kk, r, v, n_ctx)
    x1, h2, aff_t = _rwkv_out(ys, r, v, g, kd, xs, modsel[1], p, norm_g[1, 1][None], router_t[1], n_ctx)
    return _moe_block(x1, h2, aff_t, [(0, n)], modsel[1], final_g[None],
                      moe_w_gate[1], moe_w_up[1], moe_w_down[1], tile_offset=1, final_norm=True)
```

```python
import functools

import jax
import jax.numpy as jnp
from jax import lax
from jax.experimental import pallas as pl
from jax.experimental.pallas import tpu as pltpu

F32 = jnp.float32
BF16 = jnp.bfloat16

NORM_EPS = 1e-6
GN_EPS = 64e-5
GRID_W = 64
ROPE_THETA = 10000.0
HEAD_DIM = 128
N_Q_HEADS = 8
N_KV_HEADS = 2
Q_PER_KV = N_Q_HEADS // N_KV_HEADS
RW_N = 64
N_EXPERTS = 16
EC_CAPACITY = 2
LOG2_E = 1.4426950408889634

LANES = 128
TILE = 256
HALO = 8
CHUNK = 64
SCAN_BLOCK = 128
VMEM_LIMIT = 56 * 1024 * 1024


def _cparams(sem):
    return pltpu.CompilerParams(dimension_semantics=sem, vmem_limit_bytes=VMEM_LIMIT)


def _dot(a, b):
    return jnp.dot(a.astype(BF16), b.astype(BF16), preferred_element_type=F32)


def _dot_nt(a, b):
    return lax.dot_general(a.astype(BF16), b.astype(BF16), (((1,), (1,)), ((), ())),
                           preferred_element_type=F32)


def _dot_tn(a, b):
    return lax.dot_general(a.astype(BF16), b.astype(BF16), (((0,), (0,)), ((), ())),
                           preferred_element_type=F32)


def _split_dot(x, w, parts):
    acc = None
    rem = x
    for _ in range(parts):
        hi = rem.astype(BF16)
        term = jnp.dot(hi, w, preferred_element_type=F32)
        acc = term if acc is None else acc + term
        rem = rem - hi.astype(F32)
    return acc


def _rms_rows(x, g):
    ms = jnp.mean(x * x, axis=-1, keepdims=True)
    return x * lax.rsqrt(ms + NORM_EPS) * g


def _ada_rows(x, g, shift, scale):
    return _rms_rows(x, g) * (1.0 + scale) + shift


def _sigmoid(x):
    return 1.0 / (1.0 + jnp.exp(-x))


def _softplus(x):
    return jnp.maximum(x, 0.0) + jnp.log(1.0 + jnp.exp(-jnp.abs(x)))


def _shift_rows(h, prev_row, next_row):
    n = h.shape[0]
    rows = lax.broadcasted_iota(jnp.int32, h.shape, 0)
    down = jnp.where(rows == 0, prev_row, pltpu.roll(h, 1, 0))
    up = jnp.where(rows == n - 1, next_row, pltpu.roll(h, n - 1, 0))
    return down, up


def _ada_mod_kernel(c_ref, w_ref, b_ref, o_ref):
    c = c_ref[...]
    o_ref[...] = _dot(c * _sigmoid(c), w_ref[...]) + b_ref[...]


def _ada_mod(cc, ada_w, ada_b):
    depth, d, n6 = ada_w.shape
    rows = cc.shape[0]
    nb = 1536
    return pl.pallas_call(
        _ada_mod_kernel,
        out_shape=jax.ShapeDtypeStruct((depth, rows, n6), F32),
        grid=(depth, n6 // nb),
        in_specs=[
            pl.BlockSpec((rows, d), lambda l, j: (0, 0)),
            pl.BlockSpec((None, d, nb), lambda l, j: (l, 0, j)),
            pl.BlockSpec((None, 1, nb), lambda l, j: (l, 0, j)),
        ],
        out_specs=pl.BlockSpec((None, rows, nb), lambda l, j: (l, 0, j)),
        compiler_params=_cparams(("arbitrary", "arbitrary")),
        name="ada_mod",
    )(cc, ada_w, ada_b.reshape(depth, 1, n6))


def _mod_spec(d):
    return pl.BlockSpec((None, None, 6, d), lambda b, t: (b, jnp.minimum(t, 1), 0, 0))


def _in_proj_kernel(x_ref, mod_ref, g_ref, w_ref, qg_ref, kg_ref, cos_ref, sin_ref,
                    qkv_ref, conv_ref, *, n_qkv, n_rot_heads, n_q_heads):
    h = _ada_rows(x_ref[...], g_ref[...], mod_ref[0:1, :], mod_ref[1:2, :]).astype(BF16)
    nb = 512
    n_all = w_ref.shape[1]
    cos2 = cos_ref[...]
    sin2 = sin_ref[...]
    for j in range(n_all // nb):
        p = jnp.dot(h, w_ref[:, j * nb:(j + 1) * nb], preferred_element_type=F32)
        if j * nb >= n_qkv:
            conv_ref[:, j * nb - n_qkv:(j + 1) * nb - n_qkv] = p
            continue
        for i in range(nb // HEAD_DIM):
            hd = j * (nb // HEAD_DIM) + i
            t = p[:, i * HEAD_DIM:(i + 1) * HEAD_DIM]
            if hd < n_rot_heads:
                t = _rms_rows(t, qg_ref[...] if hd < n_q_heads else kg_ref[...])
                t = t * cos2 + pltpu.roll(t, HEAD_DIM // 2, 1) * sin2
            if hd < n_q_heads:
                t = t * ((HEAD_DIM ** -0.5) * LOG2_E)
            qkv_ref[:, hd * HEAD_DIM:(hd + 1) * HEAD_DIM] = t.astype(BF16)


def _in_proj(xs, modsel, g, w_in, q_g, k_g, cos2, sin2):
    b, t, d = xs.shape
    n_all = w_in.shape[1]
    n_qkv = (N_Q_HEADS + 2 * N_KV_HEADS) * HEAD_DIM
    kern = functools.partial(_in_proj_kernel, n_qkv=n_qkv, n_rot_heads=N_Q_HEADS + N_KV_HEADS,
                             n_q_heads=N_Q_HEADS)
    return pl.pallas_call(
        kern,
        out_shape=(jax.ShapeDtypeStruct((b, t, n_qkv), BF16),
                   jax.ShapeDtypeStruct((b, t, n_all - n_qkv), F32)),
        grid=(b, t // TILE),
        in_specs=[
            pl.BlockSpec((None, TILE, d), lambda i, j: (i, j, 0)),
            _mod_spec(d),
            pl.BlockSpec((1, d), lambda i, j: (0, 0)),
            pl.BlockSpec((d, n_all), lambda i, j: (0, 0)),
            pl.BlockSpec((1, HEAD_DIM), lambda i, j: (0, 0)),
            pl.BlockSpec((1, HEAD_DIM), lambda i, j: (0, 0)),
            pl.BlockSpec((TILE, HEAD_DIM), lambda i, j: (j, 0)),
            pl.BlockSpec((TILE, HEAD_DIM), lambda i, j: (j, 0)),
        ],
        out_specs=(pl.BlockSpec((None, TILE, n_qkv), lambda i, j: (i, j, 0)),
                   pl.BlockSpec((None, TILE, n_all - n_qkv), lambda i, j: (i, j, 0))),
        compiler_params=_cparams(("arbitrary", "arbitrary")),
        name="in_proj",
    )(xs, modsel, g, w_in, q_g, k_g, cos2, sin2)


def _attn_kernel(q_ref, k_ref, v_ref, o_ref, *, n_ctx):
    heads = [slice(g * HEAD_DIM, (g + 1) * HEAD_DIM) for g in range(Q_PER_KV)]

    def attend(nk):
        k = k_ref[0:nk, :]
        v = v_ref[0:nk, :]
        s = _dot_nt(q_ref[:, heads[0]], k)
        for g in range(Q_PER_KV):
            s_next = _dot_nt(q_ref[:, heads[g + 1]], k) if g + 1 < Q_PER_KV else None
            p = jnp.exp2(s - jnp.max(s, axis=-1, keepdims=True))
            l = jnp.sum(p, axis=-1, keepdims=True)
            o_ref[:, heads[g]] = jnp.dot(p.astype(BF16), v, preferred_element_type=F32) / l
            s = s_next

    is_ctx = pl.program_id(2) == 0

    @pl.when(is_ctx)
    def _():
        attend(n_ctx)

    @pl.when(jnp.logical_not(is_ctx))
    def _():
        attend(k_ref.shape[0])


def _attention(qkv, n_ctx):
    b, t, _ = qkv.shape
    kern = functools.partial(_attn_kernel, n_ctx=n_ctx)
    return pl.pallas_call(
        kern,
        out_shape=jax.ShapeDtypeStruct((b, t, N_Q_HEADS * HEAD_DIM), F32),
        grid=(b, N_KV_HEADS, t // TILE),
        in_specs=[
            pl.BlockSpec((None, TILE, Q_PER_KV * HEAD_DIM), lambda i, kv, j: (i, j, kv)),
            pl.BlockSpec((None, t, HEAD_DIM), lambda i, kv, j: (i, 0, N_Q_HEADS + kv)),
            pl.BlockSpec((None, t, HEAD_DIM), lambda i, kv, j: (i, 0, N_Q_HEADS + N_KV_HEADS + kv)),
        ],
        out_specs=pl.BlockSpec((None, TILE, Q_PER_KV * HEAD_DIM), lambda i, kv, j: (i, j, kv)),
        compiler_params=_cparams(("arbitrary",) * 3),
        name="attention",
    )(qkv, qkv, qkv)


def _residual_and_router(x, y, mod_ref, g2_ref, router_ref, x1_ref, h2_ref, aff_ref):
    x1 = x + mod_ref[2:3, :] * y
    x1_ref[...] = x1
    h2 = _ada_rows(x1, g2_ref[...], mod_ref[3:4, :], mod_ref[4:5, :]).astype(BF16)
    h2_ref[...] = h2
    logits = _dot_nt(router_ref[...], h2)
    e = jnp.exp(logits - jnp.max(logits, axis=0, keepdims=True))
    aff_ref[...] = e / jnp.sum(e, axis=0, keepdims=True)


def _mix_out_kernel(oa_ref, gb_ref, gc_ref, xc_ref, gcp_ref, xcp_ref, gcn_ref, xcn_ref,
                    x_ref, mod_ref, wout_ref, cw_ref, g2_ref, router_ref,
                    x1_ref, h2_ref, aff_ref, *, n_tiles):
    t = pl.program_id(1)
    has_prev = (t >= 2).astype(F32)
    has_next = jnp.logical_and(t >= 1, t <= n_tiles - 2).astype(F32)
    u = gc_ref[...] * xc_ref[...]
    u_prev = gcp_ref[HALO - 1:HALO, :] * xcp_ref[HALO - 1:HALO, :] * has_prev
    u_next = gcn_ref[0:1, :] * xcn_ref[0:1, :] * has_next
    down, up = _shift_rows(u, u_prev, u_next)
    o_conv = gb_ref[...] * (down * cw_ref[0:1, :] + u * cw_ref[1:2, :] + up * cw_ref[2:3, :])
    d_attn = oa_ref.shape[1]
    y = _dot(oa_ref[...], wout_ref[0:d_attn, :]) + _dot(o_conv, wout_ref[d_attn:, :])
    _residual_and_router(x_ref[...], y, mod_ref, g2_ref, router_ref, x1_ref, h2_ref, aff_ref)


def _mix_out(o_attn, conv, xs, modsel, w_out, conv_w, g2, router_t):
    b, t, d = xs.shape
    n_tiles = t // TILE
    hb = TILE // HALO
    last = t // HALO - 1
    prev_map = lambda col: (lambda i, j: (i, jnp.maximum(j * hb - 1, 0), col))
    next_map = lambda col: (lambda i, j: (i, jnp.minimum((j + 1) * hb, last), col))
    tile_map = lambda col: (lambda i, j: (i, j, col))
    n_e = router_t.shape[0]
    kern = functools.partial(_mix_out_kernel, n_tiles=n_tiles)
    return pl.pallas_call(
        kern,
        out_shape=(jax.ShapeDtypeStruct((b, t, d), F32),
                   jax.ShapeDtypeStruct((b, t, d), BF16),
                   jax.ShapeDtypeStruct((b, n_e, t), F32)),
        grid=(b, n_tiles),
        in_specs=[
            pl.BlockSpec((None, TILE, d), tile_map(0)),
            pl.BlockSpec((None, TILE, d), tile_map(0)),
            pl.BlockSpec((None, TILE, d), tile_map(1)),
            pl.BlockSpec((None, TILE, d), tile_map(2)),
            pl.BlockSpec((None, HALO, d), prev_map(1)),
            pl.BlockSpec((None, HALO, d), prev_map(2)),
            pl.BlockSpec((None, HALO, d), next_map(1)),
            pl.BlockSpec((None, HALO, d), next_map(2)),
            pl.BlockSpec((None, TILE, d), tile_map(0)),
            _mod_spec(d),
            pl.BlockSpec(w_out.shape, lambda i, j: (0, 0)),
            pl.BlockSpec(conv_w.shape, lambda i, j: (0, 0)),
            pl.BlockSpec((1, d), lambda i, j: (0, 0)),
            pl.BlockSpec((n_e, d), lambda i, j: (0, 0)),
        ],
        out_specs=(pl.BlockSpec((None, TILE, d), tile_map(0)),
                   pl.BlockSpec((None, TILE, d), tile_map(0)),
                   pl.BlockSpec((None, n_e, TILE), lambda i, j: (i, 0, j))),
        compiler_params=_cparams(("arbitrary", "arbitrary")),
        name="mix_out",
    )(o_attn, conv, conv, conv, conv, conv, conv, conv, xs, modsel, w_out, conv_w, g2, router_t)


def _moe_ffn_kernel(slot_ref, aff_ref, h_ref, wg_ref, wu_ref, wd_ref, o_ref, wg_s, wu_s, wd_s, *, sets):
    @pl.when(pl.program_id(1) == 0)
    def _():
        wg_s[...] = wg_ref[...].astype(BF16)
        wu_s[...] = wu_ref[...].astype(BF16)
        wd_s[...] = wd_ref[...].astype(BF16)

    xs, a_rows = [], []
    for start, length, slot0, cap in sets:
        tok = slice(start, start + length)
        rows = lax.broadcasted_iota(jnp.int32, (cap, length), 0).astype(F32) + slot0
        hit = slot_ref[:, tok] == rows
        a_rows.append(jnp.sum(jnp.where(hit, aff_ref[:, tok], 0.0), axis=-1, keepdims=True))
        xs.append(jnp.dot(jnp.where(hit, 1.0, 0.0).astype(BF16), h_ref[tok, :],
                          preferred_element_type=F32).astype(BF16))
    x = jnp.concatenate(xs, axis=0)
    a_row = jnp.concatenate(a_rows, axis=0)
    gate = jnp.dot(x, wg_s[...], preferred_element_type=F32)
    up = jnp.dot(x, wu_s[...], preferred_element_type=F32)
    act = (gate * _sigmoid(gate) * up).astype(BF16)
    o_ref[...] = (jnp.dot(act, wd_s[...], preferred_element_type=F32) * a_row).astype(BF16)


def _moe_ffn(slots, aff_t, h2, w_gate, w_up, w_down, layer, sets, n_slots):
    b, n_e, t = slots.shape
    d = h2.shape[2]
    f = w_gate.shape[3]
    row = pl.BlockSpec((None, None, 1, t), lambda e, i: (i, e, 0, 0))
    return pl.pallas_call(
        functools.partial(_moe_ffn_kernel, sets=sets),
        out_shape=jax.ShapeDtypeStruct((b, n_e, n_slots, d), BF16),
        grid=(n_e, b),
        in_specs=[
            row, row,
            pl.BlockSpec((None, t, d), lambda e, i: (i, 0, 0)),
            pl.BlockSpec((None, None, d, f), lambda e, i: (layer, e, 0, 0)),
            pl.BlockSpec((None, None, d, f), lambda e, i: (layer, e, 0, 0)),
            pl.BlockSpec((None, None, f, d), lambda e, i: (layer, e, 0, 0)),
        ],
        out_specs=pl.BlockSpec((None, None, n_slots, d), lambda e, i: (i, e, 0, 0)),
        scratch_shapes=[pltpu.VMEM((d, f), BF16), pltpu.VMEM((d, f), BF16), pltpu.VMEM((f, d), BF16)],
        compiler_params=_cparams(("arbitrary", "arbitrary")),
        name="moe_ffn",
    )(slots.reshape(b, n_e, 1, t), aff_t.reshape(b, n_e, 1, t), h2, w_gate, w_up, w_down)


def _prefix_count(mask, tri):
    blk = tri.shape[0]
    carry = jnp.zeros((mask.shape[0], 1), F32)
    parts = []
    for j in range(mask.shape[1] // blk):
        m = mask[:, j * blk:(j + 1) * blk]
        parts.append(jnp.dot(m.astype(BF16), tri, preferred_element_type=F32) + carry)
        carry = carry + jnp.sum(m, axis=-1, keepdims=True)
    return jnp.concatenate(parts, axis=-1)


def _route_kernel(aff_ref, slot_ref, *, sets):
    blk = 256
    tri = jnp.where(lax.broadcasted_iota(jnp.int32, (blk, blk), 0)
                    <= lax.broadcasted_iota(jnp.int32, (blk, blk), 1), 1.0, 0.0).astype(BF16)
    for start, length, slot0, cap in sets:
        cap = float(cap)
        bits = pltpu.bitcast(aff_ref[:, start:start + length], jnp.int32)
        thr = jnp.zeros((bits.shape[0], 1), jnp.int32)
        for bit in range(30, -1, -1):
            cand = thr | (1 << bit)
            cnt = jnp.sum(jnp.where(bits >= cand, 1.0, 0.0), axis=-1, keepdims=True)
            thr = jnp.where(cnt >= cap, cand, thr)
        above = jnp.where(bits > thr, 1.0, 0.0)
        tied = jnp.where(bits == thr, 1.0, 0.0)
        need = cap - jnp.sum(above, axis=-1, keepdims=True)
        chosen = above + jnp.where(_prefix_count(tied, tri) <= need, tied, 0.0)
        slot = _prefix_count(chosen, tri) - 1.0 + slot0
        slot_ref[:, start:start + length] = jnp.where(chosen > 0.0, slot, -1.0)


def _route(aff_t, sets):
    b, n_e, t = aff_t.shape
    kern = functools.partial(_route_kernel, sets=tuple(sets))
    spec = pl.BlockSpec((None, n_e, t), lambda i: (i, 0, 0))
    return pl.pallas_call(
        kern,
        out_shape=jax.ShapeDtypeStruct((b, n_e, t), F32),
        grid=(b,),
        in_specs=[spec],
        out_specs=spec,
        compiler_params=_cparams(("arbitrary",)),
        name="moe_route",
    )(aff_t)


def _combine_kernel(slot_ref, o_ref, x_ref, mod_ref, g_ref, out_ref, *, sets, final_norm):
    n_e = o_ref.shape[0]
    tile = pl.program_id(1)

    def scatter(slot0, cap):
        rows = lax.broadcasted_iota(jnp.int32, (cap, slot_ref.shape[1]), 0).astype(F32) + slot0
        y = None
        for e in range(n_e):
            onehot = jnp.where(slot_ref[e:e + 1, :] == rows, 1.0, 0.0).astype(BF16)
            term = lax.dot_general(onehot, o_ref[e, slot0:slot0 + cap, :], (((0,), (0,)), ((), ())),
                                   preferred_element_type=F32)
            y = term if y is None else y + term
        x2 = x_ref[...] + mod_ref[5:6, :] * y
        out_ref[...] = _rms_rows(x2, g_ref[...]) if final_norm else x2

    for start, length, slot0, cap in sets:
        first, last = start // TILE, (start + length) // TILE

        @pl.when(jnp.logical_and(tile >= first, tile < last))
        def _():
            scatter(slot0, cap)


def _combine(slots, outs, x1, modsel, g, sets, *, tile_offset, final_norm):
    b, t, d = x1.shape
    _, n_e, r, _ = outs.shape
    kern = functools.partial(_combine_kernel, sets=sets, final_norm=final_norm)
    return pl.pallas_call(
        kern,
        out_shape=jax.ShapeDtypeStruct((b, t, d), F32),
        grid=(b, t // TILE),
        in_specs=[
            pl.BlockSpec((None, n_e, TILE), lambda i, j: (i, 0, j)),
            pl.BlockSpec((None, n_e, r, d), lambda i, j: (i, 0, 0, 0)),
            pl.BlockSpec((None, TILE, d), lambda i, j: (i, j, 0)),
            pl.BlockSpec((None, None, 6, d), lambda i, j: (i, jnp.minimum(j + tile_offset, 1), 0, 0)),
            pl.BlockSpec((1, d), lambda i, j: (0, 0)),
        ],
        out_specs=pl.BlockSpec((None, TILE, d), lambda i, j: (i, j, 0)),
        compiler_params=_cparams(("arbitrary", "arbitrary")),
        name="moe_combine",
    )(slots, outs, x1, modsel, g)


def _rwkv_feat_kernel(x_ref, xp_ref, xn_ref, mod_ref, g_ref, mu_ref, wr_ref, wk_ref, wv_ref,
                      w1_ref, a1_ref, g1_ref, w2_ref, a2_ref, g2_ref, w0_ref, a0_ref,
                      kk_ref, ka_ref, seg_ref,
                      r_out, v_out, kkn_out, g_out, lw_out, kd_out, bb_out, *, n_tiles):
    t = pl.program_id(1)
    has_prev = (t >= 2).astype(F32)
    has_next = jnp.logical_and(t >= 1, t <= n_tiles - 2).astype(F32)
    g = g_ref[...]
    shift = mod_ref[0:1, :]
    scale = mod_ref[1:2, :]
    h = _ada_rows(x_ref[...], g, shift, scale)
    h_prev = _ada_rows(xp_ref[HALO - 1:HALO, :], g, shift, scale) * has_prev
    h_next = _ada_rows(xn_ref[0:1, :], g, shift, scale) * has_next
    down, up = _shift_rows(h, h_prev, h_next)
    xx = 0.5 * (down + up) - h
    d = h.shape[1]

    r_out[...] = _dot(h + xx * mu_ref[0:1, :], wr_ref[...])
    k = _dot(h + xx * mu_ref[2:3, :], wk_ref[...])
    v_out[...] = _dot(h + xx * mu_ref[3:4, :], wv_ref[...])
    gate = _sigmoid(_dot(h + xx * mu_ref[5:6, :], g1_ref[...]))
    g_out[...] = _dot(gate, g2_ref[...])

    kkp = k * kk_ref[...]
    ss = _split_dot(kkp * kkp, seg_ref[...], 2)
    kkn = kkp * lax.rsqrt(jnp.maximum(ss, 1e-24))
    kkn_out[...] = kkn

    lora_w = _dot(jnp.tanh(_dot(h + xx * mu_ref[1:2, :], w1_ref[...])), w2_ref[...])
    lora_a = _dot(_dot(h + xx * mu_ref[4:5, :], a1_ref[...]), a2_ref[...])
    for di in range(2):
        sl = slice(di * d, (di + 1) * d)
        w_log = -_softplus(-(w0_ref[di:di + 1, :] + lora_w[:, sl])) - 0.5
        lw_out[di] = -jnp.exp(w_log)
        a = _sigmoid(a0_ref[di:di + 1, :] + lora_a[:, sl])
        kd_out[di] = k * (1.0 + (a - 1.0) * ka_ref[...])
        bb_out[di] = a * kkn


def _rwkv_feat(xs, modsel, g, p):
    b, t, d = xs.shape
    n_tiles = t // TILE
    hb = TILE // HALO
    last = t // HALO - 1
    full = lambda a: pl.BlockSpec(a.shape, lambda i, j: (0,) * a.ndim)
    weights = [p["mu"], p["w_r"], p["w_k"], p["w_v"], p["w1"], p["a1"], p["g1"], p["w2"], p["a2"],
               p["g2"], p["w0"], p["a0"], p["k_k"], p["k_a"], p["seg"]]
    tile_spec = pl.BlockSpec((None, TILE, d), lambda i, j: (i, j, 0))
    dir_spec = pl.BlockSpec((2, None, TILE, d), lambda i, j: (0, i, j, 0))
    kern = functools.partial(_rwkv_feat_kernel, n_tiles=n_tiles)
    one = jax.ShapeDtypeStruct((b, t, d), F32)
    two = jax.ShapeDtypeStruct((2, b, t, d), F32)
    return pl.pallas_call(
        kern,
        out_shape=(one, one, one, one, two, two, two),
        grid=(b, n_tiles),
        in_specs=[
            tile_spec,
            pl.BlockSpec((None, HALO, d), lambda i, j: (i, jnp.maximum(j * hb - 1, 0), 0)),
            pl.BlockSpec((None, HALO, d), lambda i, j: (i, jnp.minimum((j + 1) * hb, last), 0)),
            _mod_spec(d),
            pl.BlockSpec((1, d), lambda i, j: (0, 0)),
        ] + [full(a) for a in weights],
        out_specs=(tile_spec, tile_spec, tile_spec, tile_spec, dir_spec, dir_spec, dir_spec),
        compiler_params=_cparams(("arbitrary", "arbitrary")),
        name="rwkv_feat",
    )(xs, xs, xs, modsel, g, *weights)


def _scan_kernel(lw_ref, kd_ref, bb_ref, kk_ref, r_ref, v_ref, y_ref, s_ref, *, rev):
    @pl.when(pl.program_id(1) == 0)
    def _():
        s_ref[...] = jnp.zeros_like(s_ref)

    L = CHUNK
    ti = lax.broadcasted_iota(jnp.int32, (L, LANES), 0)
    ii = lax.broadcasted_iota(jnp.int32, (L, LANES), 1) & (L - 1)
    before = (ii > ti) if rev else (ii < ti)
    before_eq = (ii >= ti) if rev else (ii <= ti)
    t2 = lax.broadcasted_iota(jnp.int32, (L, L), 0)
    i2 = lax.broadcasted_iota(jnp.int32, (L, L), 1)
    tri = jnp.where((i2 >= t2) if rev else (i2 <= t2), 1.0, 0.0).astype(BF16)
    r128 = lax.broadcasted_iota(jnp.int32, (LANES, LANES), 0)
    c128 = lax.broadcasted_iota(jnp.int32, (LANES, LANES), 1)
    head_eq = (r128 >> 6) == (c128 >> 6)
    eye = r128 == c128

    def same_block(log2s):
        return (ti >> log2s) == (ii >> log2s)

    def bdiag(y):
        yb = y.astype(BF16)
        return jnp.where(head_eq, jnp.concatenate([yb, yb], axis=0), jnp.zeros((), BF16))

    def hmul(x, y):
        return jnp.dot(x.astype(BF16), bdiag(y), preferred_element_type=F32)

    def hmul2(x, y, z):
        return jnp.dot(x.astype(BF16), jnp.concatenate([bdiag(y), bdiag(z)], axis=1),
                       preferred_element_type=F32)

    n_sub = lw_ref.shape[0] // L
    n_pair = lw_ref.shape[1] // LANES
    sub_order = list(range(n_sub))[::-1] if rev else list(range(n_sub))
    chains = [(slice(s * L, (s + 1) * L), slice(p * LANES, (p + 1) * LANES))
              for s in sub_order for p in range(n_pair)]

    def each(f, *cols):
        return [f(*args) for args in zip(*cols)]

    lam_sub = {s: _split_dot_left(tri, lw_ref[s * L:(s + 1) * L, :]) for s in sub_order}
    lw = [lw_ref[rs_, cs] for rs_, cs in chains]
    lam = [lam_sub[rs_.start // L][:, cs] for rs_, cs in chains]
    kk = [kk_ref[rs_, cs] for rs_, cs in chains]
    bb = [bb_ref[rs_, cs] for rs_, cs in chains]
    kd = [kd_ref[rs_, cs] for rs_, cs in chains]
    r = [r_ref[rs_, cs] for rs_, cs in chains]
    v = [v_ref[rs_, cs] for rs_, cs in chains]

    tot = each(lambda a: jnp.sum(a, axis=0, keepdims=True), lw)
    rho = each(lambda a: a[L // 2:L // 2 + 1, :], lam)
    ks = each(lambda a, b, c: a * jnp.exp(b - c), kk, lam, lw)
    rs = each(lambda a, b: a * jnp.exp(b), r, lam)
    e_neg = each(lambda a: jnp.exp(-a), rho)
    e_inv = each(lambda a, b: jnp.exp(a - b), rho, lam)
    e_end = each(lambda a, b: jnp.exp(a - b), tot, lam)

    lhs = each(lambda a, b, c: jnp.concatenate([a * c, b * c], axis=0).astype(BF16), ks, rs, e_neg)
    nt = lambda a, w: lax.dot_general(a, w, (((1,), (1,)), ((), ())), preferred_element_type=F32)
    g_bk = each(lambda a, b, c, d: nt(a, jnp.concatenate([bdiag(b * d), bdiag(c * d)], axis=0)),
                lhs, bb, kd, e_inv)
    a_b = each(lambda a: jnp.where(before, a[0:L, 0:LANES], 0.0), g_bk)
    a_k = each(lambda a: jnp.where(before, a[0:L, LANES:], 0.0), g_bk)
    m_b = each(lambda a: jnp.where(before_eq, a[L:, 0:LANES], 0.0), g_bk)
    m_k = each(lambda a: jnp.where(before_eq, a[L:, LANES:], 0.0), g_bk)
    akv = each(hmul, a_k, v)

    ident = jnp.where(ti == ii, 1.0, 0.0)
    blk8 = same_block(3)
    n1 = each(lambda a: jnp.where(blk8, -a, 0.0), a_b)
    n2 = each(hmul, n1, n1)
    n3 = each(hmul, n1, n2)
    n4 = each(hmul, n2, n2)
    x = each(lambda a, b, c: ident + a + b + c, n1, n2, n3)
    tinv = each(lambda a, b: a + hmul(a, b), x, n4)
    for log2s in (3, 4, 5):
        off_mask = jnp.logical_and(same_block(log2s + 1), jnp.logical_not(same_block(log2s)))
        off = each(lambda a: jnp.where(off_mask, a, 0.0), a_b)
        corr = each(hmul, off, tinv)
        tinv = each(lambda a, b: a - hmul(a, b), tinv, corr)

    qp = each(lambda a, b, c: -hmul2(a, b, c), tinv, ks, akv)
    q1 = each(lambda a: a[:, 0:LANES], qp)
    p0 = each(lambda a: a[:, LANES:], qp)
    mqp = each(hmul2, m_b, q1, p0)
    q2 = each(lambda a, b: a + b[:, 0:LANES], rs, mqp)
    y0 = each(lambda a, b, c: a[:, LANES:] + hmul(b, c), mqp, m_k, v)
    end = each(lambda a, b, c: jnp.concatenate([a * c, b * c], axis=0), bb, kd, e_end)
    zero = jnp.zeros((L, LANES), F32)
    mn = each(lambda a, b, c, d: _dot_tn(a, jnp.concatenate(
        [jnp.concatenate([b, c], axis=1), jnp.concatenate([zero, d], axis=1)], axis=0)),
        end, q1, p0, v)
    m_t = each(lambda a, c: jnp.where(head_eq, a[:, 0:LANES], 0.0) + jnp.where(eye, jnp.exp(c), 0.0),
               mn, tot)
    n0 = each(lambda a: jnp.where(head_eq, a[:, LANES:], 0.0), mn)
    qm = each(lambda a, b: jnp.concatenate([a, b], axis=0).astype(BF16), q2, m_t)

    state = [s_ref[p] for p in range(n_pair)]
    for idx, (rs_, cs) in enumerate(chains):
        p = idx % n_pair
        out = jnp.dot(qm[idx], state[p].astype(BF16), preferred_element_type=F32)
        y_ref[rs_, cs] = out[0:L] + y0[idx]
        state[p] = out[L:] + n0[idx]
    for p in range(n_pair):
        s_ref[p] = state[p]


def _split_dot_left(tri, x):
    acc = None
    rem = x
    for _ in range(3):
        hi = rem.astype(BF16)
        term = jnp.dot(tri, hi, preferred_element_type=F32)
        acc = term if acc is None else acc + term
        rem = rem - hi.astype(F32)
    return acc


def _rwkv_scan(lw, kd, bb, kk, r, v, n_ctx, rev):
    _, b, t, d = lw.shape
    nb_ctx = n_ctx // SCAN_BLOCK
    nb = t // SCAN_BLOCK
    di = 1 if rev else 0

    def block(j):
        return jnp.where(j < nb_ctx, nb_ctx - 1 - j, nb + nb_ctx - 1 - j) if rev else j

    dir_spec = pl.BlockSpec((None, None, SCAN_BLOCK, d), lambda i, j: (di, i, block(j), 0))
    one_spec = pl.BlockSpec((None, SCAN_BLOCK, d), lambda i, j: (i, block(j), 0))
    return pl.pallas_call(
        functools.partial(_scan_kernel, rev=rev),
        out_shape=jax.ShapeDtypeStruct((b, t, d), F32),
        grid=(b, nb),
        in_specs=[dir_spec, dir_spec, dir_spec, one_spec, one_spec, one_spec],
        out_specs=one_spec,
        scratch_shapes=[pltpu.VMEM((d // LANES, LANES, LANES), F32)],
        compiler_params=_cparams(("arbitrary", "arbitrary")),
        name="rwkv_scan_bwd" if rev else "rwkv_scan_fwd",
    )(lw, kd, bb, kk, r, v)


def _rwkv_out_kernel(yf_ref, yb_ref, r_ref, v_ref, g_ref, kdf_ref, kdb_ref, x_ref, mod_ref,
                     wo_ref, lnw_ref, lnb_ref, rk_ref, seg_ref, g2_ref, router_ref,
                     x1_ref, h2_ref, aff_ref):
    seg = seg_ref[...]
    inv_n = 1.0 / RW_N
    y = yf_ref[...] + yb_ref[...]
    mean = _split_dot(y, seg, 2) * inv_n
    dev = y - mean
    var = _split_dot(dev * dev, seg, 2) * inv_n
    yn = dev * lax.rsqrt(var + GN_EPS) * lnw_ref[...] + lnb_ref[...]
    r = r_ref[...]
    rk = rk_ref[...]
    bonus = _split_dot(r * kdf_ref[...] * rk + r * kdb_ref[...] * rk, seg, 2) * v_ref[...]
    out = _dot((yn + bonus) * g_ref[...], wo_ref[...])
    _residual_and_router(x_ref[...], out, mod_ref, g2_ref, router_ref, x1_ref, h2_ref, aff_ref)


def _rwkv_out(y_fwd, y_bwd, r, v, g, kd, xs, modsel, p, g2, router_t, n_ctx):
    b, t, d = xs.shape
    off = n_ctx // TILE
    n = t - n_ctx
    n_e = router_t.shape[0]
    tile = pl.BlockSpec((None, TILE, d), lambda i, j: (i, j + off, 0))
    dir0 = pl.BlockSpec((None, None, TILE, d), lambda i, j: (0, i, j + off, 0))
    dir1 = pl.BlockSpec((None, None, TILE, d), lambda i, j: (1, i, j + off, 0))
    vec = pl.BlockSpec((1, d), lambda i, j: (0, 0))
    out_tile = pl.BlockSpec((None, TILE, d), lambda i, j: (i, j, 0))
    return pl.pallas_call(
        _rwkv_out_kernel,
        out_shape=(jax.ShapeDtypeStruct((b, n, d), F32),
                   jax.ShapeDtypeStruct((b, n, d), BF16),
                   jax.ShapeDtypeStruct((b, n_e, n), F32)),
        grid=(b, n // TILE),
        in_specs=[tile, tile, tile, tile, tile, dir0, dir1, tile,
                  pl.BlockSpec((None, None, 6, d), lambda i, j: (i, 1, 0, 0)),
                  pl.BlockSpec((d, d), lambda i, j: (0, 0)),
                  vec, vec, vec,
                  pl.BlockSpec((d, d), lambda i, j: (0, 0)),
                  vec,
                  pl.BlockSpec((n_e, d), lambda i, j: (0, 0))],
        out_specs=(out_tile, out_tile, pl.BlockSpec((None, n_e, TILE), lambda i, j: (i, 0, j))),
        compiler_params=_cparams(("arbitrary", "arbitrary")),
        name="rwkv_out",
    )(y_fwd, y_bwd, r, v, g, kd, kd, xs, modsel, p["w_o"], p["ln_w"], p["ln_b"], p["r_k"], p["seg"], g2,
      router_t)


def _moe_block(x1, h2, aff_t, token_sets, modsel, g, w_gate, w_up, w_down, layer, *, tile_offset,
               final_norm):
    sets, n_slots = [], 0
    for start, length in token_sets:
        cap = EC_CAPACITY * length // N_EXPERTS
        sets.append((start, length, n_slots, cap))
        n_slots += cap
    sets = tuple(sets)
    slots = _route(aff_t, sets)
    outs = _moe_ffn(slots, aff_t, h2, w_gate, w_up, w_down, layer, sets, n_slots)
    return _combine(slots, outs, x1, modsel, g, sets, tile_offset=tile_offset, final_norm=final_norm)


def _rope_tables(n_ctx, n):
    rows = n // GRID_W
    row = jnp.broadcast_to(jnp.arange(rows)[:, None], (rows, GRID_W)).reshape(-1).astype(F32)
    col = jnp.broadcast_to(jnp.arange(GRID_W)[None, :], (rows, GRID_W)).reshape(-1).astype(F32)
    pairs = HEAD_DIM // 4
    inv = ROPE_THETA ** (-jnp.arange(pairs, dtype=F32) / pairs)
    ang = jnp.concatenate([row[:, None] * inv, col[:, None] * inv], axis=-1)
    cos, sin = jnp.cos(ang), jnp.sin(ang)
    cos2 = jnp.concatenate([cos, cos], axis=-1)
    sin2 = jnp.concatenate([-sin, sin], axis=-1)
    cos2 = jnp.concatenate([jnp.ones((n_ctx, HEAD_DIM), F32), cos2], axis=0)
    sin2 = jnp.concatenate([jnp.zeros((n_ctx, HEAD_DIM), F32), sin2], axis=0)
    return cos2, sin2


def _block_diag2(a, b):
    z = jnp.zeros_like(a)
    return jnp.concatenate([jnp.concatenate([a, z], axis=1), jnp.concatenate([z, b], axis=1)], axis=0)


def kernel(x, c, ctx, c_ctx, ada_w, ada_b, norm_g, final_g, mix_w_in, mix_w_out, q_norm_g, k_norm_g, conv_w, rw_mu, rw_w_r, rw_w_k, rw_w_v, rw_w_o, rw_w0, rw_w1, rw_w2, rw_a0, rw_a1, rw_a2, rw_g1, rw_g2, rw_k_k, rw_k_a, rw_r_k, rw_ln_w, rw_ln_b, moe_router, moe_w_gate, moe_w_up, moe_w_down):
    b, n, d = x.shape
    n_ctx = ctx.shape[1]
    depth = ada_w.shape[0]
    assert depth == 2 and n_ctx == TILE and n % TILE == 0 and d % LANES == 0
    t = n_ctx + n

    xs = jnp.concatenate([ctx, x], axis=1)

    rows = -(-(b + 1) // HALO) * HALO
    cc = jnp.concatenate([c, c_ctx[None, :], jnp.zeros((rows - b - 1, d), F32)], axis=0)
    mod = _ada_mod(cc, ada_w, ada_b)
    mod_lat = mod[:, :b].reshape(depth, b, 1, 6, d)
    mod_ctx = jnp.broadcast_to(mod[:, b].reshape(depth, 1, 1, 6, d), (depth, b, 1, 6, d))
    modsel = jnp.concatenate([mod_ctx, mod_lat], axis=2)

    router_t = jnp.swapaxes(moe_router, 1, 2).astype(BF16)

    cos2, sin2 = _rope_tables(n_ctx, n)
    qkv, conv = _in_proj(xs, modsel[0], norm_g[0, 0][None], mix_w_in[0].astype(BF16),
                         q_norm_g[0][None], k_norm_g[0][None], cos2, sin2)
    o_attn = _attention(qkv, n_ctx)
    x1, h2, aff_t = _mix_out(o_attn, conv, xs, modsel[0], mix_w_out[0].astype(BF16), conv_w[0],
                             norm_g[0, 1][None], router_t[0])
    xs = _moe_block(x1, h2, aff_t, [(n_ctx, n), (0, n_ctx)], modsel[0], final_g[None],
                    moe_w_gate, moe_w_up, moe_w_down, 0, tile_offset=0, final_norm=False)

    heads = d // RW_N
    seg = jnp.kron(jnp.eye(heads, dtype=F32), jnp.ones((RW_N, RW_N), F32)).astype(BF16)
    p = {
        "mu": rw_mu[0], "w_r": rw_w_r[0].astype(BF16), "w_k": rw_w_k[0].astype(BF16),
        "w_v": rw_w_v[0].astype(BF16), "w_o": rw_w_o[0].astype(BF16),
        "w1": jnp.concatenate([rw_w1[0, 0], rw_w1[0, 1]], axis=1).astype(BF16),
        "a1": jnp.concatenate([rw_a1[0, 0], rw_a1[0, 1]], axis=1).astype(BF16),
        "g1": rw_g1[0].astype(BF16),
        "w2": _block_diag2(rw_w2[0, 0], rw_w2[0, 1]).astype(BF16),
        "a2": _block_diag2(rw_a2[0, 0], rw_a2[0, 1]).astype(BF16),
        "g2": rw_g2[0].astype(BF16),
        "w0": rw_w0[0], "a0": rw_a0[0], "k_k": rw_k_k[0][None], "k_a": rw_k_a[0][None],
        "r_k": rw_r_k[0].reshape(1, d), "ln_w": rw_ln_w[0][None], "ln_b": rw_ln_b[0][None],
        "seg": seg,
    }
    r, v, kk, g, lw, kd, bb = _rwkv_feat(xs, modsel[1], norm_g[1, 0][None], p)
    y_fwd = _rwkv_scan(lw, kd, bb, kk, r, v, n_ctx, rev=False)
    y_bwd = _rwkv_scan(lw, kd, bb, kk, r, v, n_ctx, rev=True)
    x1, h2, aff_t = _rwkv_out(y_fwd, y_bwd, r, v, g, kd, xs, modsel[1], p, norm_g[1, 1][None],
                              router_t[1], n_ctx)
    return _moe_block(x1, h2, aff_t, [(0, n)], modsel[1], final_g[None],
                      moe_w_gate, moe_w_up, moe_w_down, 1, tile_offset=1, final_norm=True)
```

```python
import functools

import jax
import jax.numpy as jnp
from jax import lax
from jax.experimental import pallas as pl
from jax.experimental.pallas import tpu as pltpu

F32 = jnp.float32
BF16 = jnp.bfloat16

NORM_EPS = 1e-6
GN_EPS = 64e-5
GRID_W = 64
ROPE_THETA = 10000.0
HEAD_DIM = 128
N_Q_HEADS = 8
N_KV_HEADS = 2
Q_PER_KV = N_Q_HEADS // N_KV_HEADS
RW_N = 64
N_EXPERTS = 16
EC_CAPACITY = 2
LOG2_E = 1.4426950408889634
DECAY_SCALE = 0.6065306597126334

LANES = 128
TILE = 256
HALO = 8
CHUNK = 64
SCAN_BLOCK = 128
SCATTER_WIN = 64
SCATTER_GROUP = 4
VMEM_LIMIT = 56 * 1024 * 1024


def _cparams(sem):
    return pltpu.CompilerParams(dimension_semantics=sem, vmem_limit_bytes=VMEM_LIMIT)


def _dot(a, b):
    return jnp.dot(a.astype(BF16), b.astype(BF16), preferred_element_type=F32)


def _dot_nt(a, b):
    return lax.dot_general(a.astype(BF16), b.astype(BF16), (((1,), (1,)), ((), ())),
                           preferred_element_type=F32)


def _dot_tn(a, b):
    return lax.dot_general(a.astype(BF16), b.astype(BF16), (((0,), (0,)), ((), ())),
                           preferred_element_type=F32)


def _split_dot(x, w, parts):
    acc = None
    rem = x
    for _ in range(parts):
        hi = rem.astype(BF16)
        term = jnp.dot(hi, w, preferred_element_type=F32)
        acc = term if acc is None else acc + term
        rem = rem - hi.astype(F32)
    return acc


def _rms_rows(x, g):
    ms = jnp.mean(x * x, axis=-1, keepdims=True)
    return x * lax.rsqrt(ms + NORM_EPS) * g


def _ada_rows(x, g, shift, scale):
    return _rms_rows(x, g) * (1.0 + scale) + shift


def _sigmoid(x):
    return 1.0 / (1.0 + jnp.exp(-x))


def _shift_rows(h, prev_row, next_row):
    n = h.shape[0]
    rows = lax.broadcasted_iota(jnp.int32, h.shape, 0)
    down = jnp.where(rows == 0, prev_row, pltpu.roll(h, 1, 0))
    up = jnp.where(rows == n - 1, next_row, pltpu.roll(h, n - 1, 0))
    return down, up


def _ada_mod_kernel(c_ref, w_ref, b_ref, o_ref):
    c = c_ref[...]
    o_ref[...] = _dot(c * _sigmoid(c), w_ref[...]) + b_ref[...]


def _ada_mod(cc, ada_w, ada_b):
    depth, d, n6 = ada_w.shape
    rows = cc.shape[0]
    nb = 1536
    return pl.pallas_call(
        _ada_mod_kernel,
        out_shape=jax.ShapeDtypeStruct((depth, rows, n6), F32),
        grid=(depth, n6 // nb),
        in_specs=[
            pl.BlockSpec((rows, d), lambda l, j: (0, 0)),
            pl.BlockSpec((None, d, nb), lambda l, j: (l, 0, j)),
            pl.BlockSpec((None, 1, nb), lambda l, j: (l, 0, j)),
        ],
        out_specs=pl.BlockSpec((None, rows, nb), lambda l, j: (l, 0, j)),
        compiler_params=_cparams(("arbitrary", "arbitrary")),
        name="ada_mod",
    )(cc, ada_w, ada_b.reshape(depth, 1, n6))


def _mod_spec(d):
    return pl.BlockSpec((None, None, 6, d), lambda b, t: (b, jnp.minimum(t, 1), 0, 0))


def _in_proj_kernel(x_ref, mod_ref, g_ref, w_ref, qg_ref, kg_ref, cos_ref, sin_ref,
                    qkv_ref, conv_ref, *, n_qkv, n_rot_heads, n_q_heads):
    h = _ada_rows(x_ref[...], g_ref[...], mod_ref[0:1, :], mod_ref[1:2, :]).astype(BF16)
    nb = 512
    n_all = w_ref.shape[1]
    cos2 = cos_ref[...]
    sin2 = sin_ref[...]
    for j in range(n_all // nb):
        p = jnp.dot(h, w_ref[:, j * nb:(j + 1) * nb], preferred_element_type=F32)
        if j * nb >= n_qkv:
            conv_ref[:, j * nb - n_qkv:(j + 1) * nb - n_qkv] = p
            continue
        for i in range(nb // HEAD_DIM):
            hd = j * (nb // HEAD_DIM) + i
            t = p[:, i * HEAD_DIM:(i + 1) * HEAD_DIM]
            if hd < n_rot_heads:
                t = _rms_rows(t, qg_ref[...] if hd < n_q_heads else kg_ref[...])
                t = t * cos2 + pltpu.roll(t, HEAD_DIM // 2, 1) * sin2
            if hd < n_q_heads:
                t = t * ((HEAD_DIM ** -0.5) * LOG2_E)
            qkv_ref[:, hd * HEAD_DIM:(hd + 1) * HEAD_DIM] = t.astype(BF16)


def _in_proj(xs, modsel, g, w_in, q_g, k_g, cos2, sin2):
    b, t, d = xs.shape
    n_all = w_in.shape[1]
    n_qkv = (N_Q_HEADS + 2 * N_KV_HEADS) * HEAD_DIM
    kern = functools.partial(_in_proj_kernel, n_qkv=n_qkv, n_rot_heads=N_Q_HEADS + N_KV_HEADS,
                             n_q_heads=N_Q_HEADS)
    return pl.pallas_call(
        kern,
        out_shape=(jax.ShapeDtypeStruct((b, t, n_qkv), BF16),
                   jax.ShapeDtypeStruct((b, t, n_all - n_qkv), F32)),
        grid=(b, t // TILE),
        in_specs=[
            pl.BlockSpec((None, TILE, d), lambda i, j: (i, j, 0)),
            _mod_spec(d),
            pl.BlockSpec((1, d), lambda i, j: (0, 0)),
            pl.BlockSpec((d, n_all), lambda i, j: (0, 0)),
            pl.BlockSpec((1, HEAD_DIM), lambda i, j: (0, 0)),
            pl.BlockSpec((1, HEAD_DIM), lambda i, j: (0, 0)),
            pl.BlockSpec((TILE, HEAD_DIM), lambda i, j: (j, 0)),
            pl.BlockSpec((TILE, HEAD_DIM), lambda i, j: (j, 0)),
        ],
        out_specs=(pl.BlockSpec((None, TILE, n_qkv), lambda i, j: (i, j, 0)),
                   pl.BlockSpec((None, TILE, n_all - n_qkv), lambda i, j: (i, j, 0))),
        compiler_params=_cparams(("arbitrary", "arbitrary")),
        name="in_proj",
    )(xs, modsel, g, w_in, q_g, k_g, cos2, sin2)


def _attn_kernel(q_ref, k_ref, v_ref, o_ref, *, n_ctx):
    heads = [slice(g * HEAD_DIM, (g + 1) * HEAD_DIM) for g in range(Q_PER_KV)]

    def attend(nk):
        k = k_ref[0:nk, :]
        v = v_ref[0:nk, :]
        s = _dot_nt(q_ref[:, heads[0]], k)
        for g in range(Q_PER_KV):
            s_next = _dot_nt(q_ref[:, heads[g + 1]], k) if g + 1 < Q_PER_KV else None
            p = jnp.exp2(s - jnp.max(s, axis=-1, keepdims=True))
            l = jnp.sum(p, axis=-1, keepdims=True)
            o_ref[:, heads[g]] = jnp.dot(p.astype(BF16), v, preferred_element_type=F32) / l
            s = s_next

    is_ctx = pl.program_id(2) == 0

    @pl.when(is_ctx)
    def _():
        attend(n_ctx)

    @pl.when(jnp.logical_not(is_ctx))
    def _():
        attend(k_ref.shape[0])


def _attention(qkv, n_ctx):
    b, t, _ = qkv.shape
    kern = functools.partial(_attn_kernel, n_ctx=n_ctx)
    return pl.pallas_call(
        kern,
        out_shape=jax.ShapeDtypeStruct((b, t, N_Q_HEADS * HEAD_DIM), F32),
        grid=(b, N_KV_HEADS, t // TILE),
        in_specs=[
            pl.BlockSpec((None, TILE, Q_PER_KV * HEAD_DIM), lambda i, kv, j: (i, j, kv)),
            pl.BlockSpec((None, t, HEAD_DIM), lambda i, kv, j: (i, 0, N_Q_HEADS + kv)),
            pl.BlockSpec((None, t, HEAD_DIM), lambda i, kv, j: (i, 0, N_Q_HEADS + N_KV_HEADS + kv)),
        ],
        out_specs=pl.BlockSpec((None, TILE, Q_PER_KV * HEAD_DIM), lambda i, kv, j: (i, j, kv)),
        compiler_params=_cparams(("arbitrary",) * 3),
        name="attention",
    )(qkv, qkv, qkv)


def _residual_and_router(x, y, mod_ref, g2_ref, router_ref, x1_ref, h2_ref, aff_ref):
    x1 = x + mod_ref[2:3, :] * y
    x1_ref[...] = x1
    h2 = _ada_rows(x1, g2_ref[...], mod_ref[3:4, :], mod_ref[4:5, :]).astype(BF16)
    h2_ref[...] = h2
    logits = _dot_nt(router_ref[...], h2)
    e = jnp.exp(logits - jnp.max(logits, axis=0, keepdims=True))
    aff_ref[...] = e / jnp.sum(e, axis=0, keepdims=True)


def _mix_out_kernel(oa_ref, gb_ref, gc_ref, xc_ref, gcp_ref, xcp_ref, gcn_ref, xcn_ref,
                    x_ref, mod_ref, wout_ref, cw_ref, g2_ref, router_ref,
                    x1_ref, h2_ref, aff_ref, *, n_tiles):
    t = pl.program_id(1)
    has_prev = (t >= 2).astype(F32)
    has_next = jnp.logical_and(t >= 1, t <= n_tiles - 2).astype(F32)
    u = gc_ref[...] * xc_ref[...]
    u_prev = gcp_ref[HALO - 1:HALO, :] * xcp_ref[HALO - 1:HALO, :] * has_prev
    u_next = gcn_ref[0:1, :] * xcn_ref[0:1, :] * has_next
    down, up = _shift_rows(u, u_prev, u_next)
    o_conv = gb_ref[...] * (down * cw_ref[0:1, :] + u * cw_ref[1:2, :] + up * cw_ref[2:3, :])
    d_attn = oa_ref.shape[1]
    y = _dot(oa_ref[...], wout_ref[0:d_attn, :]) + _dot(o_conv, wout_ref[d_attn:, :])
    _residual_and_router(x_ref[...], y, mod_ref, g2_ref, router_ref, x1_ref, h2_ref, aff_ref)


def _mix_out(o_attn, conv, xs, modsel, w_out, conv_w, g2, router_t):
    b, t, d = xs.shape
    n_tiles = t // TILE
    hb = TILE // HALO
    last = t // HALO - 1
    prev_map = lambda col: (lambda i, j: (i, jnp.maximum(j * hb - 1, 0), col))
    next_map = lambda col: (lambda i, j: (i, jnp.minimum((j + 1) * hb, last), col))
    tile_map = lambda col: (lambda i, j: (i, j, col))
    n_e = router_t.shape[0]
    kern = functools.partial(_mix_out_kernel, n_tiles=n_tiles)
    return pl.pallas_call(
        kern,
        out_shape=(jax.ShapeDtypeStruct((b, t, d), F32),
                   jax.ShapeDtypeStruct((b, t, d), BF16),
                   jax.ShapeDtypeStruct((b, n_e, t), F32)),
        grid=(b, n_tiles),
        in_specs=[
            pl.BlockSpec((None, TILE, d), tile_map(0)),
            pl.BlockSpec((None, TILE, d), tile_map(0)),
            pl.BlockSpec((None, TILE, d), tile_map(1)),
            pl.BlockSpec((None, TILE, d), tile_map(2)),
            pl.BlockSpec((None, HALO, d), prev_map(1)),
            pl.BlockSpec((None, HALO, d), prev_map(2)),
            pl.BlockSpec((None, HALO, d), next_map(1)),
            pl.BlockSpec((None, HALO, d), next_map(2)),
            pl.BlockSpec((None, TILE, d), tile_map(0)),
            _mod_spec(d),
            pl.BlockSpec(w_out.shape, lambda i, j: (0, 0)),
            pl.BlockSpec(conv_w.shape, lambda i, j: (0, 0)),
            pl.BlockSpec((1, d), lambda i, j: (0, 0)),
            pl.BlockSpec((n_e, d), lambda i, j: (0, 0)),
        ],
        out_specs=(pl.BlockSpec((None, TILE, d), tile_map(0)),
                   pl.BlockSpec((None, TILE, d), tile_map(0)),
                   pl.BlockSpec((None, n_e, TILE), lambda i, j: (i, 0, j))),
        compiler_params=_cparams(("arbitrary", "arbitrary")),
        name="mix_out",
    )(o_attn, conv, conv, conv, conv, conv, conv, conv, xs, modsel, w_out, conv_w, g2, router_t)


def _moe_ffn_kernel(slot_ref, aff_ref, h_ref, wg_ref, wu_ref, wd_ref, o_ref, wg_s, wu_s, wd_s, *, sets):
    @pl.when(pl.program_id(1) == 0)
    def _():
        wg_s[...] = wg_ref[...].astype(BF16)
        wu_s[...] = wu_ref[...].astype(BF16)
        wd_s[...] = wd_ref[...].astype(BF16)

    xs, a_rows = [], []
    for start, length, slot0, cap in sets:
        tok = slice(start, start + length)
        rows = lax.broadcasted_iota(jnp.int32, (cap, length), 0).astype(F32) + slot0
        hit = slot_ref[:, tok] == rows
        a_rows.append(jnp.sum(jnp.where(hit, aff_ref[:, tok], 0.0), axis=-1, keepdims=True))
        xs.append(jnp.dot(jnp.where(hit, 1.0, 0.0).astype(BF16), h_ref[tok, :],
                          preferred_element_type=F32).astype(BF16))
    x = jnp.concatenate(xs, axis=0)
    a_row = jnp.concatenate(a_rows, axis=0)
    gate = jnp.dot(x, wg_s[...], preferred_element_type=F32)
    up = jnp.dot(x, wu_s[...], preferred_element_type=F32)
    act = (gate * _sigmoid(gate) * up).astype(BF16)
    o_ref[...] = (jnp.dot(act, wd_s[...], preferred_element_type=F32) * a_row).astype(BF16)


def _moe_ffn(slots, aff_t, h2, w_gate, w_up, w_down, layer, sets, n_slots):
    b, n_e, t = slots.shape
    d = h2.shape[2]
    f = w_gate.shape[3]
    row = pl.BlockSpec((None, None, 1, t), lambda e, i: (i, e, 0, 0))
    return pl.pallas_call(
        functools.partial(_moe_ffn_kernel, sets=sets),
        out_shape=jax.ShapeDtypeStruct((b, n_e, n_slots, d), BF16),
        grid=(n_e, b),
        in_specs=[
            row, row,
            pl.BlockSpec((None, t, d), lambda e, i: (i, 0, 0)),
            pl.BlockSpec((None, None, d, f), lambda e, i: (layer, e, 0, 0)),
            pl.BlockSpec((None, None, d, f), lambda e, i: (layer, e, 0, 0)),
            pl.BlockSpec((None, None, f, d), lambda e, i: (layer, e, 0, 0)),
        ],
        out_specs=pl.BlockSpec((None, None, n_slots, d), lambda e, i: (i, e, 0, 0)),
        scratch_shapes=[pltpu.VMEM((d, f), BF16), pltpu.VMEM((d, f), BF16), pltpu.VMEM((f, d), BF16)],
        compiler_params=_cparams(("arbitrary", "arbitrary")),
        name="moe_ffn",
    )(slots.reshape(b, n_e, 1, t), aff_t.reshape(b, n_e, 1, t), h2, w_gate, w_up, w_down)


def _prefix_count(mask, tri):
    return _prefix_count_blocks(mask, tri)[0]


def _prefix_count_blocks(mask, tri):
    blk = tri.shape[0]
    carry = jnp.zeros((mask.shape[0], 1), F32)
    parts, carries = [], [carry]
    for j in range(mask.shape[1] // blk):
        m = mask[:, j * blk:(j + 1) * blk]
        parts.append(jnp.dot(m.astype(BF16), tri, preferred_element_type=F32) + carry)
        carry = carry + jnp.sum(m, axis=-1, keepdims=True)
        carries.append(carry)
    return jnp.concatenate(parts, axis=-1), carries


def _route_kernel(aff_ref, slot_ref, bound_ref, *, sets):
    blk = TILE
    n_e = aff_ref.shape[0]
    tri = jnp.where(lax.broadcasted_iota(jnp.int32, (blk, blk), 0)
                    <= lax.broadcasted_iota(jnp.int32, (blk, blk), 1), 1.0, 0.0).astype(BF16)
    lane = lax.broadcasted_iota(jnp.int32, (n_e, LANES), 1)
    lo = jnp.zeros((n_e, LANES), F32)
    hi = jnp.zeros((n_e, LANES), F32)
    for start, length, slot0, cap in sets:
        cap = float(cap)
        bits = pltpu.bitcast(aff_ref[:, start:start + length], jnp.int32)
        thr = jnp.zeros((bits.shape[0], 1), jnp.int32)
        for bit in range(30, -1, -1):
            cand = thr | (1 << bit)
            cnt = jnp.sum(jnp.where(bits >= cand, 1.0, 0.0), axis=-1, keepdims=True)
            thr = jnp.where(cnt >= cap, cand, thr)
        above = jnp.where(bits > thr, 1.0, 0.0)
        tied = jnp.where(bits == thr, 1.0, 0.0)
        need = cap - jnp.sum(above, axis=-1, keepdims=True)
        chosen = above + jnp.where(_prefix_count(tied, tri) <= need, tied, 0.0)
        count, before = _prefix_count_blocks(chosen, tri)
        slot_ref[:, start:start + length] = jnp.where(chosen > 0.0, count - 1.0 + slot0, -1.0)
        for j in range(length // blk):
            here = lane == (start // blk + j)
            lo = jnp.where(here, before[j] + slot0, lo)
            hi = jnp.where(here, before[j + 1] + slot0, hi)
    bound_ref[0:n_e, :] = lo
    bound_ref[n_e:, :] = hi


def _route(aff_t, sets):
    b, n_e, t = aff_t.shape
    assert t // TILE <= LANES
    kern = functools.partial(_route_kernel, sets=tuple(sets))
    spec = pl.BlockSpec((None, n_e, t), lambda i: (i, 0, 0))
    bspec = pl.BlockSpec((None, 2 * n_e, LANES), lambda i: (i, 0, 0))
    return pl.pallas_call(
        kern,
        out_shape=(jax.ShapeDtypeStruct((b, n_e, t), F32),
                   jax.ShapeDtypeStruct((b, 2 * n_e, LANES), F32)),
        grid=(b,),
        in_specs=[spec],
        out_specs=(spec, bspec),
        compiler_params=_cparams(("arbitrary",)),
        name="moe_route",
    )(aff_t)


def _combine_kernel(bnd_ref, slot_ref, o_ref, x_ref, mod_ref, g_ref, out_ref, acc_ref, *, sets,
                    final_norm):
    n_e = o_ref.shape[0]
    n_tok = slot_ref.shape[1]
    sample = pl.program_id(0)
    tile = pl.program_id(1)
    tn = lambda a, w: lax.dot_general(a, w, (((0,), (0,)), ((), ())), preferred_element_type=F32)

    def onehot(e, first, n):
        rows = lax.broadcasted_iota(jnp.int32, (n, n_tok), 0) + first
        return jnp.where(slot_ref[e:e + 1, :] == rows.astype(F32), 1.0, 0.0).astype(BF16)

    def scatter_full(experts, slot0, cap):
        y = None
        for e in experts:
            term = tn(onehot(e, slot0, cap), o_ref[e, slot0:slot0 + cap, :])
            y = term if y is None else y + term
        return y

    for start, length, slot0, cap in sets:
        first_tile, last_tile = start // TILE, (start + length) // TILE

        @pl.when(jnp.logical_and(tile >= first_tile, tile < last_tile))
        def _():
            if cap <= SCATTER_WIN * SCATTER_GROUP // 2:
                acc_ref[...] = scatter_full(range(n_e), slot0, cap)
            else:
                acc_ref[...] = jnp.zeros_like(acc_ref)
                for g0 in range(0, n_e, SCATTER_GROUP):
                    experts = range(g0, g0 + SCATTER_GROUP)
                    starts, fits = [], None
                    for e in experts:
                        lo = bnd_ref[sample, e, tile]
                        hi = bnd_ref[sample, n_e + e, tile]
                        w0 = jnp.minimum((lo >> 4) << 4, slot0 + cap - SCATTER_WIN)
                        ok = hi <= w0 + SCATTER_WIN
                        fits = ok if fits is None else jnp.logical_and(fits, ok)
                        starts.append(pl.multiple_of(w0, 16))

                    @pl.when(fits)
                    def _():
                        oh = jnp.concatenate([onehot(e, w0, SCATTER_WIN) for e, w0 in zip(experts, starts)],
                                             axis=0)
                        rows = jnp.concatenate([o_ref[e, pl.ds(w0, SCATTER_WIN), :]
                                                for e, w0 in zip(experts, starts)], axis=0)
                        acc_ref[...] += tn(oh, rows)

                    @pl.when(jnp.logical_not(fits))
                    def _():
                        acc_ref[...] += scatter_full(experts, slot0, cap)

            x2 = x_ref[...] + mod_ref[5:6, :] * acc_ref[...]
            out_ref[...] = _rms_rows(x2, g_ref[...]) if final_norm else x2


def _combine(slots, bounds, outs, x1, modsel, g, sets, *, tile_offset, final_norm):
    b, t, d = x1.shape
    _, n_e, r, _ = outs.shape
    assert all(s[2] % 16 == 0 and s[3] % 16 == 0 for s in sets)
    kern = functools.partial(_combine_kernel, sets=sets, final_norm=final_norm)
    grid_spec = pltpu.PrefetchScalarGridSpec(
        num_scalar_prefetch=1,
        grid=(b, t // TILE),
        in_specs=[
            pl.BlockSpec((None, n_e, TILE), lambda i, j, bnd: (i, 0, j)),
            pl.BlockSpec((None, n_e, r, d), lambda i, j, bnd: (i, 0, 0, 0)),
            pl.BlockSpec((None, TILE, d), lambda i, j, bnd: (i, j, 0)),
            pl.BlockSpec((None, None, 6, d), lambda i, j, bnd: (i, jnp.minimum(j + tile_offset, 1), 0, 0)),
            pl.BlockSpec((1, d), lambda i, j, bnd: (0, 0)),
        ],
        out_specs=pl.BlockSpec((None, TILE, d), lambda i, j, bnd: (i, j, 0)),
        scratch_shapes=[pltpu.VMEM((TILE, d), F32)],
    )
    return pl.pallas_call(
        kern,
        out_shape=jax.ShapeDtypeStruct((b, t, d), F32),
        grid_spec=grid_spec,
        compiler_params=_cparams(("arbitrary", "arbitrary")),
        name="moe_combine",
    )(bounds[:, :, :t // TILE].astype(jnp.int32), slots, outs, x1, modsel, g)


def _rwkv_feat_kernel(x_ref, xp_ref, xn_ref, mod_ref, g_ref, mu_ref, wr_ref, wk_ref, wv_ref,
                      w1_ref, a1_ref, g1_ref, w2_ref, a2_ref, g2_ref, w0_ref, a0_ref,
                      kk_ref, ka_ref, seg_ref,
                      r_out, v_out, kkn_out, g_out, lw_out, kd_out, bb_out, *, n_tiles):
    t = pl.program_id(1)
    has_prev = (t >= 2).astype(F32)
    has_next = jnp.logical_and(t >= 1, t <= n_tiles - 2).astype(F32)
    g = g_ref[...]
    shift = mod_ref[0:1, :]
    scale = mod_ref[1:2, :]
    h = _ada_rows(x_ref[...], g, shift, scale)
    h_prev = _ada_rows(xp_ref[HALO - 1:HALO, :], g, shift, scale) * has_prev
    h_next = _ada_rows(xn_ref[0:1, :], g, shift, scale) * has_next
    down, up = _shift_rows(h, h_prev, h_next)
    xx = 0.5 * (down + up) - h
    d = h.shape[1]

    r_out[...] = _dot(h + xx * mu_ref[0:1, :], wr_ref[...])
    k = _dot(h + xx * mu_ref[2:3, :], wk_ref[...])
    v_out[...] = _dot(h + xx * mu_ref[3:4, :], wv_ref[...])
    gate = _sigmoid(_dot(h + xx * mu_ref[5:6, :], g1_ref[...]))
    g_out[...] = _dot(gate, g2_ref[...])

    kkp = k * kk_ref[...]
    ss = _split_dot(kkp * kkp, seg_ref[...], 2)
    kkn = kkp * lax.rsqrt(jnp.maximum(ss, 1e-24))
    kkn_out[...] = kkn

    lora_w = _dot(jnp.tanh(_dot(h + xx * mu_ref[1:2, :], w1_ref[...])), w2_ref[...])
    lora_a = _dot(_dot(h + xx * mu_ref[4:5, :], a1_ref[...]), a2_ref[...])
    for di in range(2):
        sl = slice(di * d, (di + 1) * d)
        lw_out[di] = -DECAY_SCALE * _sigmoid(w0_ref[di:di + 1, :] + lora_w[:, sl])
        a = _sigmoid(a0_ref[di:di + 1, :] + lora_a[:, sl])
        kd_out[di] = k * (1.0 + (a - 1.0) * ka_ref[...])
        bb_out[di] = a * kkn


def _rwkv_feat(xs, modsel, g, p):
    b, t, d = xs.shape
    n_tiles = t // TILE
    hb = TILE // HALO
    last = t // HALO - 1
    full = lambda a: pl.BlockSpec(a.shape, lambda i, j: (0,) * a.ndim)
    weights = [p["mu"], p["w_r"], p["w_k"], p["w_v"], p["w1"], p["a1"], p["g1"], p["w2"], p["a2"],
               p["g2"], p["w0"], p["a0"], p["k_k"], p["k_a"], p["seg"]]
    tile_spec = pl.BlockSpec((None, TILE, d), lambda i, j: (i, j, 0))
    dir_spec = pl.BlockSpec((2, None, TILE, d), lambda i, j: (0, i, j, 0))
    kern = functools.partial(_rwkv_feat_kernel, n_tiles=n_tiles)
    one = jax.ShapeDtypeStruct((b, t, d), F32)
    two = jax.ShapeDtypeStruct((2, b, t, d), F32)
    return pl.pallas_call(
        kern,
        out_shape=(one, one, one, one, two, two, two),
        grid=(b, n_tiles),
        in_specs=[
            tile_spec,
            pl.BlockSpec((None, HALO, d), lambda i, j: (i, jnp.maximum(j * hb - 1, 0), 0)),
            pl.BlockSpec((None, HALO, d), lambda i, j: (i, jnp.minimum((j + 1) * hb, last), 0)),
            _mod_spec(d),
            pl.BlockSpec((1, d), lambda i, j: (0, 0)),
        ] + [full(a) for a in weights],
        out_specs=(tile_spec, tile_spec, tile_spec, tile_spec, dir_spec, dir_spec, dir_spec),
        compiler_params=_cparams(("arbitrary", "arbitrary")),
        name="rwkv_feat",
    )(xs, xs, xs, modsel, g, *weights)


def _scan_kernel(lw_ref, kd_ref, bb_ref, kk_ref, r_ref, v_ref, y_ref, s_ref, *, rev):
    @pl.when(pl.program_id(1) == 0)
    def _():
        s_ref[...] = jnp.zeros_like(s_ref)

    L = CHUNK
    ti = lax.broadcasted_iota(jnp.int32, (L, LANES), 0)
    ii = lax.broadcasted_iota(jnp.int32, (L, LANES), 1) & (L - 1)
    before = (ii > ti) if rev else (ii < ti)
    before_eq = (ii >= ti) if rev else (ii <= ti)
    t2 = lax.broadcasted_iota(jnp.int32, (L, L), 0)
    i2 = lax.broadcasted_iota(jnp.int32, (L, L), 1)
    tri = jnp.where((i2 >= t2) if rev else (i2 <= t2), 1.0, 0.0).astype(BF16)
    r128 = lax.broadcasted_iota(jnp.int32, (LANES, LANES), 0)
    c128 = lax.broadcasted_iota(jnp.int32, (LANES, LANES), 1)
    head_eq = (r128 >> 6) == (c128 >> 6)
    eye = r128 == c128

    def same_block(log2s):
        return (ti >> log2s) == (ii >> log2s)

    def bdiag(y):
        yb = y.astype(BF16)
        return jnp.where(head_eq, jnp.concatenate([yb, yb], axis=0), jnp.zeros((), BF16))

    def hmul(x, y):
        return jnp.dot(x.astype(BF16), bdiag(y), preferred_element_type=F32)

    def hmul2(x, y, z):
        return jnp.dot(x.astype(BF16), jnp.concatenate([bdiag(y), bdiag(z)], axis=1),
                       preferred_element_type=F32)

    n_sub = lw_ref.shape[0] // L
    n_pair = lw_ref.shape[1] // LANES
    sub_order = list(range(n_sub))[::-1] if rev else list(range(n_sub))
    chains = [(slice(s * L, (s + 1) * L), slice(p * LANES, (p + 1) * LANES))
              for s in sub_order for p in range(n_pair)]

    def each(f, *cols):
        return [f(*args) for args in zip(*cols)]

    lam_sub = {s: _split_dot_left(tri, lw_ref[s * L:(s + 1) * L, :]) for s in sub_order}
    lw = [lw_ref[rs_, cs] for rs_, cs in chains]
    lam = [lam_sub[rs_.start // L][:, cs] for rs_, cs in chains]
    kk = [kk_ref[rs_, cs] for rs_, cs in chains]
    bb = [bb_ref[rs_, cs] for rs_, cs in chains]
    kd = [kd_ref[rs_, cs] for rs_, cs in chains]
    r = [r_ref[rs_, cs] for rs_, cs in chains]
    v = [v_ref[rs_, cs] for rs_, cs in chains]

    tot = each(lambda a: jnp.sum(a, axis=0, keepdims=True), lw)
    rho = each(lambda a: a[L // 2:L // 2 + 1, :], lam)
    ks = each(lambda a, b, c: a * jnp.exp(b - c), kk, lam, lw)
    rs = each(lambda a, b: a * jnp.exp(b), r, lam)
    e_neg = each(lambda a: jnp.exp(-a), rho)
    e_inv = each(lambda a, b: jnp.exp(a - b), rho, lam)
    e_end = each(lambda a, b: jnp.exp(a - b), tot, lam)

    lhs = each(lambda a, b, c: jnp.concatenate([a * c, b * c], axis=0).astype(BF16), ks, rs, e_neg)
    nt = lambda a, w: lax.dot_general(a, w, (((1,), (1,)), ((), ())), preferred_element_type=F32)
    g_bk = each(lambda a, b, c, d: nt(a, jnp.concatenate([bdiag(b * d), bdiag(c * d)], axis=0)),
                lhs, bb, kd, e_inv)
    a_b = each(lambda a: jnp.where(before, a[0:L, 0:LANES], 0.0), g_bk)
    a_k = each(lambda a: jnp.where(before, a[0:L, LANES:], 0.0), g_bk)
    m_b = each(lambda a: jnp.where(before_eq, a[L:, 0:LANES], 0.0), g_bk)
    m_k = each(lambda a: jnp.where(before_eq, a[L:, LANES:], 0.0), g_bk)
    akv = each(hmul, a_k, v)

    ident = jnp.where(ti == ii, 1.0, 0.0)
    blk8 = same_block(3)
    n1 = each(lambda a: jnp.where(blk8, -a, 0.0), a_b)
    n2 = each(hmul, n1, n1)
    n3 = each(hmul, n1, n2)
    n4 = each(hmul, n2, n2)
    x = each(lambda a, b, c: ident + a + b + c, n1, n2, n3)
    tinv = each(lambda a, b: a + hmul(a, b), x, n4)
    for log2s in (3, 4, 5):
        off_mask = jnp.logical_and(same_block(log2s + 1), jnp.logical_not(same_block(log2s)))
        off = each(lambda a: jnp.where(off_mask, a, 0.0), a_b)
        corr = each(hmul, off, tinv)
        tinv = each(lambda a, b: a - hmul(a, b), tinv, corr)

    qp = each(lambda a, b, c: -hmul2(a, b, c), tinv, ks, akv)
    q1 = each(lambda a: a[:, 0:LANES], qp)
    p0 = each(lambda a: a[:, LANES:], qp)
    mqp = each(hmul2, m_b, q1, p0)
    q2 = each(lambda a, b: a + b[:, 0:LANES], rs, mqp)
    y0 = each(lambda a, b, c: a[:, LANES:] + hmul(b, c), mqp, m_k, v)
    end = each(lambda a, b, c: jnp.concatenate([a * c, b * c], axis=0), bb, kd, e_end)
    zero = jnp.zeros((L, LANES), F32)
    mn = each(lambda a, b, c, d: _dot_tn(a, jnp.concatenate(
        [jnp.concatenate([b, c], axis=1), jnp.concatenate([zero, d], axis=1)], axis=0)),
        end, q1, p0, v)
    m_t = each(lambda a, c: jnp.where(head_eq, a[:, 0:LANES], 0.0) + jnp.where(eye, jnp.exp(c), 0.0),
               mn, tot)
    n0 = each(lambda a: jnp.where(head_eq, a[:, LANES:], 0.0), mn)
    qm = each(lambda a, b: jnp.concatenate([a, b], axis=0).astype(BF16), q2, m_t)

    state = [s_ref[p] for p in range(n_pair)]
    for idx, (rs_, cs) in enumerate(chains):
        p = idx % n_pair
        out = jnp.dot(qm[idx], state[p].astype(BF16), preferred_element_type=F32)
        y_ref[rs_, cs] = out[0:L] + y0[idx]
        state[p] = out[L:] + n0[idx]
    for p in range(n_pair):
        s_ref[p] = state[p]


def _split_dot_left(tri, x):
    acc = None
    rem = x
    for _ in range(3):
        hi = rem.astype(BF16)
        term = jnp.dot(tri, hi, preferred_element_type=F32)
        acc = term if acc is None else acc + term
        rem = rem - hi.astype(F32)
    return acc


def _rwkv_scan(lw, kd, bb, kk, r, v, n_ctx, rev):
    _, b, t, d = lw.shape
    nb_ctx = n_ctx // SCAN_BLOCK
    nb = t // SCAN_BLOCK
    di = 1 if rev else 0

    def block(j):
        return jnp.where(j < nb_ctx, nb_ctx - 1 - j, nb + nb_ctx - 1 - j) if rev else j

    dir_spec = pl.BlockSpec((None, None, SCAN_BLOCK, d), lambda i, j: (di, i, block(j), 0))
    one_spec = pl.BlockSpec((None, SCAN_BLOCK, d), lambda i, j: (i, block(j), 0))
    return pl.pallas_call(
        functools.partial(_scan_kernel, rev=rev),
        out_shape=jax.ShapeDtypeStruct((b, t, d), F32),
        grid=(b, nb),
        in_specs=[dir_spec, dir_spec, dir_spec, one_spec, one_spec, one_spec],
        out_specs=one_spec,
        scratch_shapes=[pltpu.VMEM((d // LANES, LANES, LANES), F32)],
        compiler_params=_cparams(("arbitrary", "arbitrary")),
        name="rwkv_scan_bwd" if rev else "rwkv_scan_fwd",
    )(lw, kd, bb, kk, r, v)


def _rwkv_out_kernel(yf_ref, yb_ref, r_ref, v_ref, g_ref, kdf_ref, kdb_ref, x_ref, mod_ref,
                     wo_ref, lnw_ref, lnb_ref, rk_ref, seg_ref, g2_ref, router_ref,
                     x1_ref, h2_ref, aff_ref):
    seg = seg_ref[...]
    inv_n = 1.0 / RW_N
    y = yf_ref[...] + yb_ref[...]
    mean = _split_dot(y, seg, 2) * inv_n
    dev = y - mean
    var = _split_dot(dev * dev, seg, 2) * inv_n
    yn = dev * lax.rsqrt(var + GN_EPS) * lnw_ref[...] + lnb_ref[...]
    r = r_ref[...]
    rk = rk_ref[...]
    bonus = _split_dot(r * kdf_ref[...] * rk + r * kdb_ref[...] * rk, seg, 2) * v_ref[...]
    out = _dot((yn + bonus) * g_ref[...], wo_ref[...])
    _residual_and_router(x_ref[...], out, mod_ref, g2_ref, router_ref, x1_ref, h2_ref, aff_ref)


def _rwkv_out(y_fwd, y_bwd, r, v, g, kd, xs, modsel, p, g2, router_t, n_ctx):
    b, t, d = xs.shape
    off = n_ctx // TILE
    n = t - n_ctx
    n_e = router_t.shape[0]
    tile = pl.BlockSpec((None, TILE, d), lambda i, j: (i, j + off, 0))
    dir0 = pl.BlockSpec((None, None, TILE, d), lambda i, j: (0, i, j + off, 0))
    dir1 = pl.BlockSpec((None, None, TILE, d), lambda i, j: (1, i, j + off, 0))
    vec = pl.BlockSpec((1, d), lambda i, j: (0, 0))
    out_tile = pl.BlockSpec((None, TILE, d), lambda i, j: (i, j, 0))
    return pl.pallas_call(
        _rwkv_out_kernel,
        out_shape=(jax.ShapeDtypeStruct((b, n, d), F32),
                   jax.ShapeDtypeStruct((b, n, d), BF16),
                   jax.ShapeDtypeStruct((b, n_e, n), F32)),
        grid=(b, n // TILE),
        in_specs=[tile, tile, tile, tile, tile, dir0, dir1, tile,
                  pl.BlockSpec((None, None, 6, d), lambda i, j: (i, 1, 0, 0)),
                  pl.BlockSpec((d, d), lambda i, j: (0, 0)),
                  vec, vec, vec,
                  pl.BlockSpec((d, d), lambda i, j: (0, 0)),
                  vec,
                  pl.BlockSpec((n_e, d), lambda i, j: (0, 0))],
        out_specs=(out_tile, out_tile, pl.BlockSpec((None, n_e, TILE), lambda i, j: (i, 0, j))),
        compiler_params=_cparams(("arbitrary", "arbitrary")),
        name="rwkv_out",
    )(y_fwd, y_bwd, r, v, g, kd, kd, xs, modsel, p["w_o"], p["ln_w"], p["ln_b"], p["r_k"], p["seg"], g2,
      router_t)


def _moe_block(x1, h2, aff_t, token_sets, modsel, g, w_gate, w_up, w_down, layer, *, tile_offset,
               final_norm):
    sets, n_slots = [], 0
    for start, length in token_sets:
        cap = EC_CAPACITY * length // N_EXPERTS
        sets.append((start, length, n_slots, cap))
        n_slots += cap
    sets = tuple(sets)
    slots, bounds = _route(aff_t, sets)
    outs = _moe_ffn(slots, aff_t, h2, w_gate, w_up, w_down, layer, sets, n_slots)
    return _combine(slots, bounds, outs, x1, modsel, g, sets, tile_offset=tile_offset,
                    final_norm=final_norm)


def _rope_tables(n_ctx, n):
    rows = n // GRID_W
    row = jnp.broadcast_to(jnp.arange(rows)[:, None], (rows, GRID_W)).reshape(-1).astype(F32)
    col = jnp.broadcast_to(jnp.arange(GRID_W)[None, :], (rows, GRID_W)).reshape(-1).astype(F32)
    pairs = HEAD_DIM // 4
    inv = ROPE_THETA ** (-jnp.arange(pairs, dtype=F32) / pairs)
    ang = jnp.concatenate([row[:, None] * inv, col[:, None] * inv], axis=-1)
    cos, sin = jnp.cos(ang), jnp.sin(ang)
    cos2 = jnp.concatenate([cos, cos], axis=-1)
    sin2 = jnp.concatenate([-sin, sin], axis=-1)
    cos2 = jnp.concatenate([jnp.ones((n_ctx, HEAD_DIM), F32), cos2], axis=0)
    sin2 = jnp.concatenate([jnp.zeros((n_ctx, HEAD_DIM), F32), sin2], axis=0)
    return cos2, sin2


def _block_diag2(a, b):
    z = jnp.zeros_like(a)
    return jnp.concatenate([jnp.concatenate([a, z], axis=1), jnp.concatenate([z, b], axis=1)], axis=0)


def kernel(x, c, ctx, c_ctx, ada_w, ada_b, norm_g, final_g, mix_w_in, mix_w_out, q_norm_g, k_norm_g, conv_w, rw_mu, rw_w_r, rw_w_k, rw_w_v, rw_w_o, rw_w0, rw_w1, rw_w2, rw_a0, rw_a1, rw_a2, rw_g1, rw_g2, rw_k_k, rw_k_a, rw_r_k, rw_ln_w, rw_ln_b, moe_router, moe_w_gate, moe_w_up, moe_w_down):
    b, n, d = x.shape
    n_ctx = ctx.shape[1]
    depth = ada_w.shape[0]
    assert depth == 2 and n_ctx == TILE and n % TILE == 0 and d % LANES == 0
    t = n_ctx + n

    xs = jnp.concatenate([ctx, x], axis=1)

    rows = -(-(b + 1) // HALO) * HALO
    cc = jnp.concatenate([c, c_ctx[None, :], jnp.zeros((rows - b - 1, d), F32)], axis=0)
    mod = _ada_mod(cc, ada_w, ada_b)
    mod_lat = mod[:, :b].reshape(depth, b, 1, 6, d)
    mod_ctx = jnp.broadcast_to(mod[:, b].reshape(depth, 1, 1, 6, d), (depth, b, 1, 6, d))
    modsel = jnp.concatenate([mod_ctx, mod_lat], axis=2)

    router_t = jnp.swapaxes(moe_router, 1, 2).astype(BF16)

    cos2, sin2 = _rope_tables(n_ctx, n)
    qkv, conv = _in_proj(xs, modsel[0], norm_g[0, 0][None], mix_w_in[0].astype(BF16),
                         q_norm_g[0][None], k_norm_g[0][None], cos2, sin2)
    o_attn = _attention(qkv, n_ctx)
    x1, h2, aff_t = _mix_out(o_attn, conv, xs, modsel[0], mix_w_out[0].astype(BF16), conv_w[0],
                             norm_g[0, 1][None], router_t[0])
    xs = _moe_block(x1, h2, aff_t, [(n_ctx, n), (0, n_ctx)], modsel[0], final_g[None],
                    moe_w_gate, moe_w_up, moe_w_down, 0, tile_offset=0, final_norm=False)

    heads = d // RW_N
    seg = jnp.kron(jnp.eye(heads, dtype=F32), jnp.ones((RW_N, RW_N), F32)).astype(BF16)
    p = {
        "mu": rw_mu[0], "w_r": rw_w_r[0].astype(BF16), "w_k": rw_w_k[0].astype(BF16),
        "w_v": rw_w_v[0].astype(BF16), "w_o": rw_w_o[0].astype(BF16),
        "w1": jnp.concatenate([rw_w1[0, 0], rw_w1[0, 1]], axis=1).astype(BF16),
        "a1": jnp.concatenate([rw_a1[0, 0], rw_a1[0, 1]], axis=1).astype(BF16),
        "g1": rw_g1[0].astype(BF16),
        "w2": _block_diag2(rw_w2[0, 0], rw_w2[0, 1]).astype(BF16),
        "a2": _block_diag2(rw_a2[0, 0], rw_a2[0, 1]).astype(BF16),
        "g2": rw_g2[0].astype(BF16),
        "w0": rw_w0[0], "a0": rw_a0[0], "k_k": rw_k_k[0][None], "k_a": rw_k_a[0][None],
        "r_k": rw_r_k[0].reshape(1, d), "ln_w": rw_ln_w[0][None], "ln_b": rw_ln_b[0][None],
        "seg": seg,
    }
    r, v, kk, g, lw, kd, bb = _rwkv_feat(xs, modsel[1], norm_g[1, 0][None], p)
    y_fwd = _rwkv_scan(lw, kd, bb, kk, r, v, n_ctx, rev=False)
    y_bwd = _rwkv_scan(lw, kd, bb, kk, r, v, n_ctx, rev=True)
    x1, h2, aff_t = _rwkv_out(y_fwd, y_bwd, r, v, g, kd, xs, modsel[1], p, norm_g[1, 1][None],
                              router_t[1], n_ctx)
    return _moe_block(x1, h2, aff_t, [(0, n)], modsel[1], final_g[None],
                      moe_w_gate, moe_w_up, moe_w_down, 1, tile_offset=1, final_norm=True)
```

```python
import functools

import jax
import jax.numpy as jnp
from jax import lax
from jax.experimental import pallas as pl
from jax.experimental.pallas import tpu as pltpu

F32 = jnp.float32
BF16 = jnp.bfloat16

NORM_EPS = 1e-6
GN_EPS = 64e-5
GRID_W = 64
ROPE_THETA = 10000.0
HEAD_DIM = 128
N_Q_HEADS = 8
N_KV_HEADS = 2
Q_PER_KV = N_Q_HEADS // N_KV_HEADS
RW_N = 64
N_EXPERTS = 16
EC_CAPACITY = 2
LOG2_E = 1.4426950408889634
DECAY_SCALE = 0.6065306597126334

LANES = 128
TILE = 256
HALO = 8
CHUNK = 64
SCAN_BLOCK = 128
GATHER_WIN = 64
MOE_FFN_SAMPLES = 4
SCATTER_WIN = 64
SCATTER_GROUP = 4
VMEM_LIMIT = 56 * 1024 * 1024


def _cparams(sem):
    return pltpu.CompilerParams(dimension_semantics=sem, vmem_limit_bytes=VMEM_LIMIT)


def _dot(a, b):
    return jnp.dot(a.astype(BF16), b.astype(BF16), preferred_element_type=F32)


def _dot_nt(a, b):
    return lax.dot_general(a.astype(BF16), b.astype(BF16), (((1,), (1,)), ((), ())),
                           preferred_element_type=F32)


def _dot_tn(a, b):
    return lax.dot_general(a.astype(BF16), b.astype(BF16), (((0,), (0,)), ((), ())),
                           preferred_element_type=F32)


def _split_dot(x, w, parts):
    acc = None
    rem = x
    for _ in range(parts):
        hi = rem.astype(BF16)
        term = jnp.dot(hi, w, preferred_element_type=F32)
        acc = term if acc is None else acc + term
        rem = rem - hi.astype(F32)
    return acc


def _segsum(x, seg_r, seg_b):
    return _split_dot(_split_dot(x, seg_r, 2), seg_b, 3)


def _rms_rows(x, g):
    ms = jnp.mean(x * x, axis=-1, keepdims=True)
    return x * lax.rsqrt(ms + NORM_EPS) * g


def _ada_rows(x, g, shift, scale):
    return _rms_rows(x, g) * (1.0 + scale) + shift


def _sigmoid(x):
    return 1.0 / (1.0 + jnp.exp(-x))


def _shift_rows(h, prev_row, next_row):
    n = h.shape[0]
    rows = lax.broadcasted_iota(jnp.int32, h.shape, 0)
    down = jnp.where(rows == 0, prev_row, pltpu.roll(h, 1, 0))
    up = jnp.where(rows == n - 1, next_row, pltpu.roll(h, n - 1, 0))
    return down, up


def _ada_mod_kernel(c_ref, w_ref, b_ref, o_ref):
    c = c_ref[...]
    o_ref[...] = _dot(c * _sigmoid(c), w_ref[...]) + b_ref[...]


def _ada_mod(cc, ada_w, ada_b):
    depth, d, n6 = ada_w.shape
    rows = cc.shape[0]
    nb = 1536
    return pl.pallas_call(
        _ada_mod_kernel,
        out_shape=jax.ShapeDtypeStruct((depth, rows, n6), F32),
        grid=(depth, n6 // nb),
        in_specs=[
            pl.BlockSpec((rows, d), lambda l, j: (0, 0)),
            pl.BlockSpec((None, d, nb), lambda l, j: (l, 0, j)),
            pl.BlockSpec((None, 1, nb), lambda l, j: (l, 0, j)),
        ],
        out_specs=pl.BlockSpec((None, rows, nb), lambda l, j: (l, 0, j)),
        compiler_params=_cparams(("arbitrary", "arbitrary")),
        name="ada_mod",
    )(cc, ada_w, ada_b.reshape(depth, 1, n6))


def _mod_spec(d):
    return pl.BlockSpec((None, None, 6, d), lambda b, t: (b, jnp.minimum(t, 1), 0, 0))


def _stream_tile(ctx_ref, x_ref):
    return jnp.where(pl.program_id(1) == 0, ctx_ref[...], x_ref[...])


def _stream_specs(d):
    return [pl.BlockSpec((None, TILE, d), lambda i, j: (i, 0, 0)),
            pl.BlockSpec((None, TILE, d), lambda i, j: (i, jnp.maximum(j - 1, 0), 0))]


def _in_proj_kernel(ctx_ref, x_ref, mod_ref, g_ref, w_ref, qg_ref, kg_ref, cos_ref, sin_ref,
                    qkv_ref, conv_ref, *, n_qkv, n_rot_heads, n_q_heads):
    h = _ada_rows(_stream_tile(ctx_ref, x_ref), g_ref[...], mod_ref[0:1, :], mod_ref[1:2, :]).astype(BF16)
    nb = 512
    n_all = w_ref.shape[1]
    cos2 = cos_ref[...]
    sin2 = sin_ref[...]
    for j in range(n_all // nb):
        p = jnp.dot(h, w_ref[:, j * nb:(j + 1) * nb], preferred_element_type=F32)
        if j * nb >= n_qkv:
            conv_ref[:, j * nb - n_qkv:(j + 1) * nb - n_qkv] = p
            continue
        for i in range(nb // HEAD_DIM):
            hd = j * (nb // HEAD_DIM) + i
            t = p[:, i * HEAD_DIM:(i + 1) * HEAD_DIM]
            if hd < n_rot_heads:
                t = _rms_rows(t, qg_ref[...] if hd < n_q_heads else kg_ref[...])
                t = t * cos2 + pltpu.roll(t, HEAD_DIM // 2, 1) * sin2
            if hd < n_q_heads:
                t = t * ((HEAD_DIM ** -0.5) * LOG2_E)
            qkv_ref[:, hd * HEAD_DIM:(hd + 1) * HEAD_DIM] = t.astype(BF16)


def _in_proj(ctx, x, modsel, g, w_in, q_g, k_g, cos2, sin2):
    b, n, d = x.shape
    t = ctx.shape[1] + n
    n_all = w_in.shape[1]
    n_qkv = (N_Q_HEADS + 2 * N_KV_HEADS) * HEAD_DIM
    kern = functools.partial(_in_proj_kernel, n_qkv=n_qkv, n_rot_heads=N_Q_HEADS + N_KV_HEADS,
                             n_q_heads=N_Q_HEADS)
    return pl.pallas_call(
        kern,
        out_shape=(jax.ShapeDtypeStruct((b, t, n_qkv), BF16),
                   jax.ShapeDtypeStruct((b, t, n_all - n_qkv), F32)),
        grid=(b, t // TILE),
        in_specs=_stream_specs(d) + [
            _mod_spec(d),
            pl.BlockSpec((1, d), lambda i, j: (0, 0)),
            pl.BlockSpec((d, n_all), lambda i, j: (0, 0)),
            pl.BlockSpec((1, HEAD_DIM), lambda i, j: (0, 0)),
            pl.BlockSpec((1, HEAD_DIM), lambda i, j: (0, 0)),
            pl.BlockSpec((TILE, HEAD_DIM), lambda i, j: (j, 0)),
            pl.BlockSpec((TILE, HEAD_DIM), lambda i, j: (j, 0)),
        ],
        out_specs=(pl.BlockSpec((None, TILE, n_qkv), lambda i, j: (i, j, 0)),
                   pl.BlockSpec((None, TILE, n_all - n_qkv), lambda i, j: (i, j, 0))),
        compiler_params=_cparams(("arbitrary", "arbitrary")),
        name="in_proj",
    )(ctx, x, modsel, g, w_in, q_g, k_g, cos2, sin2)


def _attn_kernel(q_ref, k_ref, v_ref, o_ref, *, n_ctx):
    heads = [slice(g * HEAD_DIM, (g + 1) * HEAD_DIM) for g in range(Q_PER_KV)]

    def attend(nk):
        k = k_ref[0:nk, :]
        v = v_ref[0:nk, :]
        s = _dot_nt(q_ref[:, heads[0]], k)
        for g in range(Q_PER_KV):
            s_next = _dot_nt(q_ref[:, heads[g + 1]], k) if g + 1 < Q_PER_KV else None
            p = jnp.exp2(s - jnp.max(s, axis=-1, keepdims=True))
            l = jnp.sum(p, axis=-1, keepdims=True)
            o_ref[:, heads[g]] = jnp.dot(p.astype(BF16), v, preferred_element_type=F32) / l
            s = s_next

    is_ctx = pl.program_id(2) == 0

    @pl.when(is_ctx)
    def _():
        attend(n_ctx)

    @pl.when(jnp.logical_not(is_ctx))
    def _():
        attend(k_ref.shape[0])


def _attention(qkv, n_ctx):
    b, t, _ = qkv.shape
    kern = functools.partial(_attn_kernel, n_ctx=n_ctx)
    return pl.pallas_call(
        kern,
        out_shape=jax.ShapeDtypeStruct((b, t, N_Q_HEADS * HEAD_DIM), F32),
        grid=(b, N_KV_HEADS, t // TILE),
        in_specs=[
            pl.BlockSpec((None, TILE, Q_PER_KV * HEAD_DIM), lambda i, kv, j: (i, j, kv)),
            pl.BlockSpec((None, t, HEAD_DIM), lambda i, kv, j: (i, 0, N_Q_HEADS + kv)),
            pl.BlockSpec((None, t, HEAD_DIM), lambda i, kv, j: (i, 0, N_Q_HEADS + N_KV_HEADS + kv)),
        ],
        out_specs=pl.BlockSpec((None, TILE, Q_PER_KV * HEAD_DIM), lambda i, kv, j: (i, j, kv)),
        compiler_params=_cparams(("arbitrary",) * 3),
        name="attention",
    )(qkv, qkv, qkv)


def _residual_and_router(x, y, mod_ref, g2_ref, router_ref, x1_ref, h2_ref, aff_ref):
    x1 = x + mod_ref[2:3, :] * y
    x1_ref[...] = x1
    h2 = _ada_rows(x1, g2_ref[...], mod_ref[3:4, :], mod_ref[4:5, :]).astype(BF16)
    h2_ref[...] = h2
    logits = _dot_nt(router_ref[...], h2)
    e = jnp.exp(logits - jnp.max(logits, axis=0, keepdims=True))
    aff_ref[...] = e / jnp.sum(e, axis=0, keepdims=True)


def _mix_out_kernel(oa_ref, gb_ref, gc_ref, xc_ref, gcp_ref, xcp_ref, gcn_ref, xcn_ref,
                    ctx_ref, x_ref, mod_ref, wout_ref, cw_ref, g2_ref, router_ref,
                    x1_ref, h2_ref, aff_ref, *, n_tiles):
    t = pl.program_id(1)
    has_prev = (t >= 2).astype(F32)
    has_next = jnp.logical_and(t >= 1, t <= n_tiles - 2).astype(F32)
    u = gc_ref[...] * xc_ref[...]
    u_prev = gcp_ref[HALO - 1:HALO, :] * xcp_ref[HALO - 1:HALO, :] * has_prev
    u_next = gcn_ref[0:1, :] * xcn_ref[0:1, :] * has_next
    down, up = _shift_rows(u, u_prev, u_next)
    o_conv = gb_ref[...] * (down * cw_ref[0:1, :] + u * cw_ref[1:2, :] + up * cw_ref[2:3, :])
    d_attn = oa_ref.shape[1]
    y = _dot(oa_ref[...], wout_ref[0:d_attn, :]) + _dot(o_conv, wout_ref[d_attn:, :])
    _residual_and_router(_stream_tile(ctx_ref, x_ref), y, mod_ref, g2_ref, router_ref, x1_ref, h2_ref,
                         aff_ref)


def _mix_out(o_attn, conv, ctx, x, modsel, w_out, conv_w, g2, router_t):
    b, t, d = o_attn.shape
    n_tiles = t // TILE
    hb = TILE // HALO
    last = t // HALO - 1
    prev_map = lambda col: (lambda i, j: (i, jnp.maximum(j * hb - 1, 0), col))
    next_map = lambda col: (lambda i, j: (i, jnp.minimum((j + 1) * hb, last), col))
    tile_map = lambda col: (lambda i, j: (i, j, col))
    n_e = router_t.shape[0]
    kern = functools.partial(_mix_out_kernel, n_tiles=n_tiles)
    return pl.pallas_call(
        kern,
        out_shape=(jax.ShapeDtypeStruct((b, t, d), F32),
                   jax.ShapeDtypeStruct((b, t, d), BF16),
                   jax.ShapeDtypeStruct((b, n_e, t), F32)),
        grid=(b, n_tiles),
        in_specs=[
            pl.BlockSpec((None, TILE, d), tile_map(0)),
            pl.BlockSpec((None, TILE, d), tile_map(0)),
            pl.BlockSpec((None, TILE, d), tile_map(1)),
            pl.BlockSpec((None, TILE, d), tile_map(2)),
            pl.BlockSpec((None, HALO, d), prev_map(1)),
            pl.BlockSpec((None, HALO, d), prev_map(2)),
            pl.BlockSpec((None, HALO, d), next_map(1)),
            pl.BlockSpec((None, HALO, d), next_map(2)),
        ] + _stream_specs(d) + [
            _mod_spec(d),
            pl.BlockSpec(w_out.shape, lambda i, j: (0, 0)),
            pl.BlockSpec(conv_w.shape, lambda i, j: (0, 0)),
            pl.BlockSpec((1, d), lambda i, j: (0, 0)),
            pl.BlockSpec((n_e, d), lambda i, j: (0, 0)),
        ],
        out_specs=(pl.BlockSpec((None, TILE, d), tile_map(0)),
                   pl.BlockSpec((None, TILE, d), tile_map(0)),
                   pl.BlockSpec((None, n_e, TILE), lambda i, j: (i, 0, j))),
        compiler_params=_cparams(("arbitrary", "arbitrary")),
        name="mix_out",
    )(o_attn, conv, conv, conv, conv, conv, conv, conv, ctx, x, modsel, w_out, conv_w, g2, router_t)


def _slot_windows(bnd_ref, sample, tile, n_e, slot0, cap, win, group):
    starts, fits = [], []
    for e in range(n_e):
        lo = bnd_ref[sample, e, tile]
        hi = bnd_ref[sample, n_e + e, tile]
        w0 = jnp.minimum((lo >> 4) << 4, slot0 + cap - win)
        ok = hi <= w0 + win
        if e % group == 0:
            fits.append(ok)
        else:
            fits[-1] = jnp.logical_and(fits[-1], ok)
        starts.append(pl.multiple_of(w0, 16))
    return starts, fits


def _moe_gather_kernel(bnd_ref, slot_ref, h_ref, x_ref, *, sets):
    n_e = x_ref.shape[0]
    n_tok = slot_ref.shape[1]
    sample = pl.program_id(0)
    tile = pl.program_id(1)

    @pl.when(tile == 0)
    def _():
        x_ref[...] = jnp.zeros_like(x_ref)

    def onehot(e, first, n):
        rows = lax.broadcasted_iota(jnp.int32, (n, n_tok), 0) + first
        return jnp.where(slot_ref[e:e + 1, :] == rows.astype(F32), 1.0, 0.0).astype(BF16)

    for start, length, slot0, cap in sets:
        first_tile, last_tile = start // TILE, (start + length) // TILE
        win = min(GATHER_WIN, cap)

        @pl.when(jnp.logical_and(tile >= first_tile, tile < last_tile))
        def _():
            starts, (fits,) = _slot_windows(bnd_ref, sample, tile, n_e, slot0, cap, win, group=n_e)

            @pl.when(fits)
            def _():
                oh = jnp.concatenate([onehot(e, starts[e], win) for e in range(n_e)], axis=0)
                part = jnp.dot(oh, h_ref[...], preferred_element_type=F32).astype(BF16)
                for e in range(n_e):
                    x_ref[e, pl.ds(starts[e], win), :] += part[e * win:(e + 1) * win]

            @pl.when(jnp.logical_not(fits))
            def _():
                for e in range(n_e):
                    x_ref[e, slot0:slot0 + cap, :] += jnp.dot(
                        onehot(e, slot0, cap), h_ref[...], preferred_element_type=F32).astype(BF16)


def _moe_gather(slots, bounds, h2, sets, n_slots):
    b, n_e, t = slots.shape
    d = h2.shape[2]
    grid_spec = pltpu.PrefetchScalarGridSpec(
        num_scalar_prefetch=1,
        grid=(b, t // TILE),
        in_specs=[
            pl.BlockSpec((None, n_e, TILE), lambda i, j, bnd: (i, 0, j)),
            pl.BlockSpec((None, TILE, d), lambda i, j, bnd: (i, j, 0)),
        ],
        out_specs=pl.BlockSpec((None, n_e, n_slots, d), lambda i, j, bnd: (i, 0, 0, 0)),
    )
    return pl.pallas_call(
        functools.partial(_moe_gather_kernel, sets=sets),
        out_shape=jax.ShapeDtypeStruct((b, n_e, n_slots, d), BF16),
        grid_spec=grid_spec,
        compiler_params=_cparams(("arbitrary", "arbitrary")),
        name="moe_gather",
    )(bounds, slots, h2)


def _moe_ffn_kernel(slot_ref, aff_ref, x_ref, wg_ref, wu_ref, wd_ref, o_ref, wg_s, wu_s, wd_s):
    @pl.when(pl.program_id(1) == 0)
    def _():
        wg_s[...] = wg_ref[...].astype(BF16)
        wu_s[...] = wu_ref[...].astype(BF16)
        wd_s[...] = wd_ref[...].astype(BF16)

    nb, r, d = x_ref.shape
    rows = lax.broadcasted_iota(jnp.int32, (r, slot_ref.shape[2]), 0).astype(F32)
    a_row = jnp.concatenate(
        [jnp.sum(jnp.where(slot_ref[i] == rows, aff_ref[i], 0.0), axis=-1, keepdims=True) for i in range(nb)],
        axis=0)
    x = x_ref[...].reshape(nb * r, d)
    gate = jnp.dot(x, wg_s[...], preferred_element_type=F32)
    up = jnp.dot(x, wu_s[...], preferred_element_type=F32)
    act = (gate * _sigmoid(gate) * up).astype(BF16)
    out = jnp.dot(act, wd_s[...], preferred_element_type=F32) * a_row
    o_ref[...] = out.astype(BF16).reshape(nb, r, d)


def _moe_ffn(slots, aff_t, xs, w_gate, w_up, w_down, layer):
    b, n_e, t = slots.shape
    _, _, r, d = xs.shape
    f = w_gate.shape[3]
    nb = MOE_FFN_SAMPLES if b % MOE_FFN_SAMPLES == 0 else 1
    row = pl.BlockSpec((nb, None, 1, t), lambda e, i: (i, e, 0, 0))
    tok = pl.BlockSpec((nb, None, r, d), lambda e, i: (i, e, 0, 0))
    return pl.pallas_call(
        _moe_ffn_kernel,
        out_shape=jax.ShapeDtypeStruct((b, n_e, r, d), BF16),
        grid=(n_e, b // nb),
        in_specs=[
            row, row, tok,
            pl.BlockSpec((None, None, d, f), lambda e, i: (layer, e, 0, 0)),
            pl.BlockSpec((None, None, d, f), lambda e, i: (layer, e, 0, 0)),
            pl.BlockSpec((None, None, f, d), lambda e, i: (layer, e, 0, 0)),
        ],
        out_specs=tok,
        scratch_shapes=[pltpu.VMEM((d, f), BF16), pltpu.VMEM((d, f), BF16), pltpu.VMEM((f, d), BF16)],
        compiler_params=_cparams(("arbitrary", "arbitrary")),
        name="moe_ffn",
    )(slots.reshape(b, n_e, 1, t), aff_t.reshape(b, n_e, 1, t), xs, w_gate, w_up, w_down)


def _prefix_count(mask, tri):
    return _prefix_count_blocks(mask, tri)[0]


def _prefix_count_blocks(mask, tri):
    blk = tri.shape[0]
    carry = jnp.zeros((mask.shape[0], 1), F32)
    parts, carries = [], [carry]
    for j in range(mask.shape[1] // blk):
        m = mask[:, j * blk:(j + 1) * blk]
        parts.append(jnp.dot(m.astype(BF16), tri, preferred_element_type=F32) + carry)
        carry = carry + jnp.sum(m, axis=-1, keepdims=True)
        carries.append(carry)
    return jnp.concatenate(parts, axis=-1), carries


def _route_kernel(aff_ref, slot_ref, bound_ref, *, sets):
    blk = TILE
    n_e = aff_ref.shape[0]
    tri = jnp.where(lax.broadcasted_iota(jnp.int32, (blk, blk), 0)
                    <= lax.broadcasted_iota(jnp.int32, (blk, blk), 1), 1.0, 0.0).astype(BF16)
    lane = lax.broadcasted_iota(jnp.int32, (n_e, LANES), 1)
    lo = jnp.zeros((n_e, LANES), F32)
    hi = jnp.zeros((n_e, LANES), F32)
    for start, length, slot0, cap in sets:
        cap = float(cap)
        x = aff_ref[:, start:start + length]
        bits = pltpu.bitcast(x, jnp.int32)
        thr = jnp.zeros((bits.shape[0], 1), jnp.int32)
        for bit in range(30, -1, -1):
            cand = thr | (1 << bit)
            cnt = jnp.sum(jnp.where(bits >= cand, 1.0, 0.0), axis=-1, keepdims=True)
            thr = jnp.where(cnt >= cap, cand, thr)
        pivot = jnp.max(jnp.where(bits <= thr, x, -1.0), axis=-1, keepdims=True)

        def counts(p):
            return (jnp.sum(jnp.where(x > p, 1.0, 0.0), axis=-1, keepdims=True),
                    jnp.sum(jnp.where(x >= p, 1.0, 0.0), axis=-1, keepdims=True))

        def misplaced(p):
            n_gt, n_ge = counts(p)
            return jnp.max(jnp.where(jnp.logical_or(n_gt >= cap, n_ge < cap), 1.0, 0.0)) > 0.0

        def walk(p):
            n_gt, n_ge = counts(p)
            higher = jnp.min(jnp.where(x > p, x, jnp.inf), axis=-1, keepdims=True)
            lower = jnp.max(jnp.where(x < p, x, -1.0), axis=-1, keepdims=True)
            return jnp.where(n_gt >= cap, higher, jnp.where(n_ge < cap, lower, p))

        pivot = lax.while_loop(misplaced, walk, pivot)
        above = jnp.where(x > pivot, 1.0, 0.0)
        tied = jnp.where(x == pivot, 1.0, 0.0)
        need = cap - jnp.sum(above, axis=-1, keepdims=True)
        chosen = above + jnp.where(_prefix_count(tied, tri) <= need, tied, 0.0)
        count, before = _prefix_count_blocks(chosen, tri)
        slot_ref[:, start:start + length] = jnp.where(chosen > 0.0, count - 1.0 + slot0, -1.0)
        for j in range(length // blk):
            here = lane == (start // blk + j)
            lo = jnp.where(here, before[j] + slot0, lo)
            hi = jnp.where(here, before[j + 1] + slot0, hi)
    bound_ref[0:n_e, :] = lo
    bound_ref[n_e:, :] = hi


def _route(aff_t, sets):
    b, n_e, t = aff_t.shape
    assert t // TILE <= LANES
    kern = functools.partial(_route_kernel, sets=tuple(sets))
    spec = pl.BlockSpec((None, n_e, t), lambda i: (i, 0, 0))
    bspec = pl.BlockSpec((None, 2 * n_e, LANES), lambda i: (i, 0, 0))
    return pl.pallas_call(
        kern,
        out_shape=(jax.ShapeDtypeStruct((b, n_e, t), F32),
                   jax.ShapeDtypeStruct((b, 2 * n_e, LANES), F32)),
        grid=(b,),
        in_specs=[spec],
        out_specs=(spec, bspec),
        compiler_params=_cparams(("arbitrary",)),
        name="moe_route",
    )(aff_t)


def _combine_kernel(bnd_ref, slot_ref, o_ref, x_ref, mod_ref, g_ref, out_ref, acc_ref, *, sets,
                    final_norm):
    n_e = o_ref.shape[0]
    n_tok = slot_ref.shape[1]
    sample = pl.program_id(0)
    tile = pl.program_id(1)
    tn = lambda a, w: lax.dot_general(a, w, (((0,), (0,)), ((), ())), preferred_element_type=F32)

    def onehot(e, first, n):
        rows = lax.broadcasted_iota(jnp.int32, (n, n_tok), 0) + first
        return jnp.where(slot_ref[e:e + 1, :] == rows.astype(F32), 1.0, 0.0).astype(BF16)

    def scatter_full(experts, slot0, cap):
        y = None
        for e in experts:
            term = tn(onehot(e, slot0, cap), o_ref[e, slot0:slot0 + cap, :])
            y = term if y is None else y + term
        return y

    for start, length, slot0, cap in sets:
        first_tile, last_tile = start // TILE, (start + length) // TILE

        @pl.when(jnp.logical_and(tile >= first_tile, tile < last_tile))
        def _():
            if cap <= SCATTER_WIN * SCATTER_GROUP // 2:
                acc_ref[...] = scatter_full(range(n_e), slot0, cap)
            else:
                acc_ref[...] = jnp.zeros_like(acc_ref)
                all_starts, all_fits = _slot_windows(bnd_ref, sample, tile, n_e, slot0, cap, SCATTER_WIN,
                                                     group=SCATTER_GROUP)
                for g0 in range(0, n_e, SCATTER_GROUP):
                    experts = range(g0, g0 + SCATTER_GROUP)
                    starts = all_starts[g0:g0 + SCATTER_GROUP]
                    fits = all_fits[g0 // SCATTER_GROUP]

                    @pl.when(fits)
                    def _():
                        oh = jnp.concatenate([onehot(e, w0, SCATTER_WIN) for e, w0 in zip(experts, starts)],
                                             axis=0)
                        rows = jnp.concatenate([o_ref[e, pl.ds(w0, SCATTER_WIN), :]
                                                for e, w0 in zip(experts, starts)], axis=0)
                        acc_ref[...] += tn(oh, rows)

                    @pl.when(jnp.logical_not(fits))
                    def _():
                        acc_ref[...] += scatter_full(experts, slot0, cap)

            x2 = x_ref[...] + mod_ref[5:6, :] * acc_ref[...]
            out_ref[...] = _rms_rows(x2, g_ref[...]) if final_norm else x2


def _combine(slots, bounds, outs, x1, modsel, g, sets, *, tile_offset, final_norm):
    b, t, d = x1.shape
    _, n_e, r, _ = outs.shape
    assert all(s[2] % 16 == 0 and s[3] % 16 == 0 for s in sets)
    kern = functools.partial(_combine_kernel, sets=sets, final_norm=final_norm)
    grid_spec = pltpu.PrefetchScalarGridSpec(
        num_scalar_prefetch=1,
        grid=(b, t // TILE),
        in_specs=[
            pl.BlockSpec((None, n_e, TILE), lambda i, j, bnd: (i, 0, j)),
            pl.BlockSpec((None, n_e, r, d), lambda i, j, bnd: (i, 0, 0, 0)),
            pl.BlockSpec((None, TILE, d), lambda i, j, bnd: (i, j, 0)),
            pl.BlockSpec((None, None, 6, d), lambda i, j, bnd: (i, jnp.minimum(j + tile_offset, 1), 0, 0)),
            pl.BlockSpec((1, d), lambda i, j, bnd: (0, 0)),
        ],
        out_specs=pl.BlockSpec((None, TILE, d), lambda i, j, bnd: (i, j, 0)),
        scratch_shapes=[pltpu.VMEM((TILE, d), F32)],
    )
    return pl.pallas_call(
        kern,
        out_shape=jax.ShapeDtypeStruct((b, t, d), F32),
        grid_spec=grid_spec,
        compiler_params=_cparams(("arbitrary", "arbitrary")),
        name="moe_combine",
    )(bounds, slots, outs, x1, modsel, g)


def _rwkv_feat_kernel(x_ref, xp_ref, xn_ref, mod_ref, g_ref, mu_ref, wr_ref, wk_ref, wv_ref,
                      w1_ref, a1_ref, g1_ref, w2_ref, a2_ref, g2_ref, w0_ref, a0_ref,
                      kk_ref, ka_ref, segr_ref, segb_ref,
                      r_out, v_out, kkn_out, g_out, lw_out, kd_out, bb_out, *, n_tiles):
    t = pl.program_id(1)
    has_prev = (t >= 2).astype(F32)
    has_next = jnp.logical_and(t >= 1, t <= n_tiles - 2).astype(F32)
    g = g_ref[...]
    shift = mod_ref[0:1, :]
    scale = mod_ref[1:2, :]
    h = _ada_rows(x_ref[...], g, shift, scale)
    h_prev = _ada_rows(xp_ref[HALO - 1:HALO, :], g, shift, scale) * has_prev
    h_next = _ada_rows(xn_ref[0:1, :], g, shift, scale) * has_next
    down, up = _shift_rows(h, h_prev, h_next)
    xx = 0.5 * (down + up) - h
    d = h.shape[1]

    r_out[...] = _dot(h + xx * mu_ref[0:1, :], wr_ref[...])
    k = _dot(h + xx * mu_ref[2:3, :], wk_ref[...])
    v_out[...] = _dot(h + xx * mu_ref[3:4, :], wv_ref[...])
    gate = _sigmoid(_dot(h + xx * mu_ref[5:6, :], g1_ref[...]))
    g_out[...] = _dot(gate, g2_ref[...])

    kkp = k * kk_ref[...]
    ss = _segsum(kkp * kkp, segr_ref[...], segb_ref[...])
    kkn = kkp * lax.rsqrt(jnp.maximum(ss, 1e-24))
    kkn_out[...] = kkn

    lora_w = _dot(jnp.tanh(_dot(h + xx * mu_ref[1:2, :], w1_ref[...])), w2_ref[...])
    lora_a = _dot(_dot(h + xx * mu_ref[4:5, :], a1_ref[...]), a2_ref[...])
    for di in range(2):
        sl = slice(di * d, (di + 1) * d)
        lw_out[di] = -DECAY_SCALE * _sigmoid(w0_ref[di:di + 1, :] + lora_w[:, sl])
        a = _sigmoid(a0_ref[di:di + 1, :] + lora_a[:, sl])
        kd_out[di] = k * (1.0 + (a - 1.0) * ka_ref[...])
        bb_out[di] = a * kkn


def _rwkv_feat(xs, modsel, g, p):
    b, t, d = xs.shape
    n_tiles = t // TILE
    hb = TILE // HALO
    last = t // HALO - 1
    full = lambda a: pl.BlockSpec(a.shape, lambda i, j: (0,) * a.ndim)
    weights = [p["mu"], p["w_r"], p["w_k"], p["w_v"], p["w1"], p["a1"], p["g1"], p["w2"], p["a2"],
               p["g2"], p["w0"], p["a0"], p["k_k"], p["k_a"], p["seg_r"], p["seg_b"]]
    tile_spec = pl.BlockSpec((None, TILE, d), lambda i, j: (i, j, 0))
    dir_spec = pl.BlockSpec((2, None, TILE, d), lambda i, j: (0, i, j, 0))
    kern = functools.partial(_rwkv_feat_kernel, n_tiles=n_tiles)
    one = jax.ShapeDtypeStruct((b, t, d), F32)
    two = jax.ShapeDtypeStruct((2, b, t, d), F32)
    return pl.pallas_call(
        kern,
        out_shape=(one, one, one, one, two, two, two),
        grid=(b, n_tiles),
        in_specs=[
            tile_spec,
            pl.BlockSpec((None, HALO, d), lambda i, j: (i, jnp.maximum(j * hb - 1, 0), 0)),
            pl.BlockSpec((None, HALO, d), lambda i, j: (i, jnp.minimum((j + 1) * hb, last), 0)),
            _mod_spec(d),
            pl.BlockSpec((1, d), lambda i, j: (0, 0)),
        ] + [full(a) for a in weights],
        out_specs=(tile_spec, tile_spec, tile_spec, tile_spec, dir_spec, dir_spec, dir_spec),
        compiler_params=_cparams(("arbitrary", "arbitrary")),
        name="rwkv_feat",
    )(xs, xs, xs, modsel, g, *weights)


def _scan_kernel(lw_ref, kd_ref, bb_ref, kk_ref, r_ref, v_ref, y_ref, s_ref, *, rev):
    @pl.when(pl.program_id(1) == 0)
    def _():
        s_ref[...] = jnp.zeros_like(s_ref)

    L = CHUNK
    ti = lax.broadcasted_iota(jnp.int32, (L, LANES), 0)
    ii = lax.broadcasted_iota(jnp.int32, (L, LANES), 1) & (L - 1)
    before = (ii > ti) if rev else (ii < ti)
    before_eq = (ii >= ti) if rev else (ii <= ti)
    t2 = lax.broadcasted_iota(jnp.int32, (L, L), 0)
    i2 = lax.broadcasted_iota(jnp.int32, (L, L), 1)
    tri = jnp.where((i2 >= t2) if rev else (i2 <= t2), 1.0, 0.0).astype(BF16)
    r128 = lax.broadcasted_iota(jnp.int32, (LANES, LANES), 0)
    c128 = lax.broadcasted_iota(jnp.int32, (LANES, LANES), 1)
    head_eq = (r128 >> 6) == (c128 >> 6)
    eye = r128 == c128

    def same_block(log2s):
        return (ti >> log2s) == (ii >> log2s)

    def bdiag(y):
        yb = y.astype(BF16)
        return jnp.where(head_eq, jnp.concatenate([yb, yb], axis=0), jnp.zeros((), BF16))

    def hmul(x, y):
        return jnp.dot(x.astype(BF16), bdiag(y), preferred_element_type=F32)

    def hmul2(x, y, z):
        return jnp.dot(x.astype(BF16), jnp.concatenate([bdiag(y), bdiag(z)], axis=1),
                       preferred_element_type=F32)

    n_sub = lw_ref.shape[0] // L
    n_pair = lw_ref.shape[1] // LANES
    sub_order = list(range(n_sub))[::-1] if rev else list(range(n_sub))
    chains = [(slice(s * L, (s + 1) * L), slice(p * LANES, (p + 1) * LANES))
              for s in sub_order for p in range(n_pair)]

    def each(f, *cols):
        return [f(*args) for args in zip(*cols)]

    lam_sub = {s: _split_dot_left(tri, lw_ref[s * L:(s + 1) * L, :]) for s in sub_order}
    lw = [lw_ref[rs_, cs] for rs_, cs in chains]
    lam = [lam_sub[rs_.start // L][:, cs] for rs_, cs in chains]
    kk = [kk_ref[rs_, cs] for rs_, cs in chains]
    bb = [bb_ref[rs_, cs] for rs_, cs in chains]
    kd = [kd_ref[rs_, cs] for rs_, cs in chains]
    r = [r_ref[rs_, cs] for rs_, cs in chains]
    v = [v_ref[rs_, cs] for rs_, cs in chains]

    tot = each(lambda a: jnp.sum(a, axis=0, keepdims=True), lw)
    rho = each(lambda a: a[L // 2:L // 2 + 1, :], lam)
    ks = each(lambda a, b, c: a * jnp.exp(b - c), kk, lam, lw)
    rs = each(lambda a, b: a * jnp.exp(b), r, lam)
    e_neg = each(lambda a: jnp.exp(-a), rho)
    e_inv = each(lambda a, b: jnp.exp(a - b), rho, lam)
    e_end = each(lambda a, b: jnp.exp(a - b), tot, lam)

    lhs = each(lambda a, b, c: jnp.concatenate([a * c, b * c], axis=0).astype(BF16), ks, rs, e_neg)
    nt = lambda a, w: lax.dot_general(a, w, (((1,), (1,)), ((), ())), preferred_element_type=F32)
    g_bk = each(lambda a, b, c, d: nt(a, jnp.concatenate([bdiag(b * d), bdiag(c * d)], axis=0)),
                lhs, bb, kd, e_inv)
    a_b = each(lambda a: jnp.where(before, a[0:L, 0:LANES], 0.0), g_bk)
    a_k = each(lambda a: jnp.where(before, a[0:L, LANES:], 0.0), g_bk)
    m_b = each(lambda a: jnp.where(before_eq, a[L:, 0:LANES], 0.0), g_bk)
    m_k = each(lambda a: jnp.where(before_eq, a[L:, LANES:], 0.0), g_bk)
    akv = each(hmul, a_k, v)

    ident = jnp.where(ti == ii, 1.0, 0.0)
    blk8 = same_block(3)
    n1 = each(lambda a: jnp.where(blk8, -a, 0.0), a_b)
    n2 = each(hmul, n1, n1)
    n3 = each(hmul, n1, n2)
    n4 = each(hmul, n2, n2)
    x = each(lambda a, b, c: ident + a + b + c, n1, n2, n3)
    tinv = each(lambda a, b: a + hmul(a, b), x, n4)
    for log2s in (3, 4, 5):
        off_mask = jnp.logical_and(same_block(log2s + 1), jnp.logical_not(same_block(log2s)))
        off = each(lambda a: jnp.where(off_mask, a, 0.0), a_b)
        corr = each(hmul, off, tinv)
        tinv = each(lambda a, b: a - hmul(a, b), tinv, corr)

    qp = each(lambda a, b, c: -hmul2(a, b, c), tinv, ks, akv)
    q1 = each(lambda a: a[:, 0:LANES], qp)
    p0 = each(lambda a: a[:, LANES:], qp)
    mqp = each(hmul2, m_b, q1, p0)
    q2 = each(lambda a, b: a + b[:, 0:LANES], rs, mqp)
    y0 = each(lambda a, b, c: a[:, LANES:] + hmul(b, c), mqp, m_k, v)
    end = each(lambda a, b, c: jnp.concatenate([a * c, b * c], axis=0), bb, kd, e_end)
    zero = jnp.zeros((L, LANES), F32)
    mn = each(lambda a, b, c, d: _dot_tn(a, jnp.concatenate(
        [jnp.concatenate([b, c], axis=1), jnp.concatenate([zero, d], axis=1)], axis=0)),
        end, q1, p0, v)
    m_t = each(lambda a, c: jnp.where(head_eq, a[:, 0:LANES], 0.0) + jnp.where(eye, jnp.exp(c), 0.0),
               mn, tot)
    n0 = each(lambda a: jnp.where(head_eq, a[:, LANES:], 0.0), mn)
    qm = each(lambda a, b: jnp.concatenate([a, b], axis=0).astype(BF16), q2, m_t)

    state = [s_ref[p] for p in range(n_pair)]
    for idx, (rs_, cs) in enumerate(chains):
        p = idx % n_pair
        out = jnp.dot(qm[idx], state[p].astype(BF16), preferred_element_type=F32)
        y_ref[rs_, cs] = out[0:L] + y0[idx]
        state[p] = out[L:] + n0[idx]
    for p in range(n_pair):
        s_ref[p] = state[p]


def _split_dot_left(tri, x):
    acc = None
    rem = x
    for _ in range(3):
        hi = rem.astype(BF16)
        term = jnp.dot(tri, hi, preferred_element_type=F32)
        acc = term if acc is None else acc + term
        rem = rem - hi.astype(F32)
    return acc


def _rwkv_scan(lw, kd, bb, kk, r, v, n_ctx, rev):
    _, b, t, d = lw.shape
    nb_ctx = n_ctx // SCAN_BLOCK
    nb = t // SCAN_BLOCK
    di = 1 if rev else 0

    def block(j):
        return jnp.where(j < nb_ctx, nb_ctx - 1 - j, nb + nb_ctx - 1 - j) if rev else j

    dir_spec = pl.BlockSpec((None, None, SCAN_BLOCK, d), lambda i, j: (di, i, block(j), 0))
    one_spec = pl.BlockSpec((None, SCAN_BLOCK, d), lambda i, j: (i, block(j), 0))
    return pl.pallas_call(
        functools.partial(_scan_kernel, rev=rev),
        out_shape=jax.ShapeDtypeStruct((b, t, d), F32),
        grid=(b, nb),
        in_specs=[dir_spec, dir_spec, dir_spec, one_spec, one_spec, one_spec],
        out_specs=one_spec,
        scratch_shapes=[pltpu.VMEM((d // LANES, LANES, LANES), F32)],
        compiler_params=_cparams(("arbitrary", "arbitrary")),
        name="rwkv_scan_bwd" if rev else "rwkv_scan_fwd",
    )(lw, kd, bb, kk, r, v)


def _rwkv_out_kernel(yf_ref, yb_ref, r_ref, v_ref, g_ref, kdf_ref, kdb_ref, x_ref, mod_ref,
                     wo_ref, lnw_ref, lnb_ref, rk_ref, segr_ref, segb_ref, g2_ref, router_ref,
                     x1_ref, h2_ref, aff_ref):
    seg_r = segr_ref[...]
    seg_b = segb_ref[...]
    inv_n = 1.0 / RW_N
    y = yf_ref[...] + yb_ref[...]
    mean = _segsum(y, seg_r, seg_b) * inv_n
    dev = y - mean
    var = _segsum(dev * dev, seg_r, seg_b) * inv_n
    yn = dev * lax.rsqrt(var + GN_EPS) * lnw_ref[...] + lnb_ref[...]
    r = r_ref[...]
    rk = rk_ref[...]
    bonus = _segsum(r * kdf_ref[...] * rk + r * kdb_ref[...] * rk, seg_r, seg_b) * v_ref[...]
    out = _dot((yn + bonus) * g_ref[...], wo_ref[...])
    _residual_and_router(x_ref[...], out, mod_ref, g2_ref, router_ref, x1_ref, h2_ref, aff_ref)


def _rwkv_out(y_fwd, y_bwd, r, v, g, kd, xs, modsel, p, g2, router_t, n_ctx):
    b, t, d = xs.shape
    off = n_ctx // TILE
    n = t - n_ctx
    n_e = router_t.shape[0]
    tile = pl.BlockSpec((None, TILE, d), lambda i, j: (i, j + off, 0))
    dir0 = pl.BlockSpec((None, None, TILE, d), lambda i, j: (0, i, j + off, 0))
    dir1 = pl.BlockSpec((None, None, TILE, d), lambda i, j: (1, i, j + off, 0))
    vec = pl.BlockSpec((1, d), lambda i, j: (0, 0))
    out_tile = pl.BlockSpec((None, TILE, d), lambda i, j: (i, j, 0))
    return pl.pallas_call(
        _rwkv_out_kernel,
        out_shape=(jax.ShapeDtypeStruct((b, n, d), F32),
                   jax.ShapeDtypeStruct((b, n, d), BF16),
                   jax.ShapeDtypeStruct((b, n_e, n), F32)),
        grid=(b, n // TILE),
        in_specs=[tile, tile, tile, tile, tile, dir0, dir1, tile,
                  pl.BlockSpec((None, None, 6, d), lambda i, j: (i, 1, 0, 0)),
                  pl.BlockSpec((d, d), lambda i, j: (0, 0)),
                  vec, vec, vec,
                  pl.BlockSpec((d, LANES), lambda i, j: (0, 0)),
                  pl.BlockSpec((LANES, d), lambda i, j: (0, 0)),
                  vec,
                  pl.BlockSpec((n_e, d), lambda i, j: (0, 0))],
        out_specs=(out_tile, out_tile, pl.BlockSpec((None, n_e, TILE), lambda i, j: (i, 0, j))),
        compiler_params=_cparams(("arbitrary", "arbitrary")),
        name="rwkv_out",
    )(y_fwd, y_bwd, r, v, g, kd, kd, xs, modsel, p["w_o"], p["ln_w"], p["ln_b"], p["r_k"], p["seg_r"],
      p["seg_b"], g2, router_t)


def _moe_block(x1, h2, aff_t, token_sets, modsel, g, w_gate, w_up, w_down, layer, *, tile_offset,
               final_norm):
    sets, n_slots = [], 0
    for start, length in token_sets:
        cap = EC_CAPACITY * length // N_EXPERTS
        sets.append((start, length, n_slots, cap))
        n_slots += cap
    sets = tuple(sets)
    slots, bounds = _route(aff_t, sets)
    bounds = bounds[:, :, :aff_t.shape[2] // TILE].astype(jnp.int32)
    xs = _moe_gather(slots, bounds, h2, sets, n_slots)
    outs = _moe_ffn(slots, aff_t, xs, w_gate, w_up, w_down, layer)
    return _combine(slots, bounds, outs, x1, modsel, g, sets, tile_offset=tile_offset,
                    final_norm=final_norm)


def _rope_tables(n_ctx, n):
    rows = n // GRID_W
    row = jnp.broadcast_to(jnp.arange(rows)[:, None], (rows, GRID_W)).reshape(-1).astype(F32)
    col = jnp.broadcast_to(jnp.arange(GRID_W)[None, :], (rows, GRID_W)).reshape(-1).astype(F32)
    pairs = HEAD_DIM // 4
    inv = ROPE_THETA ** (-jnp.arange(pairs, dtype=F32) / pairs)
    ang = jnp.concatenate([row[:, None] * inv, col[:, None] * inv], axis=-1)
    cos, sin = jnp.cos(ang), jnp.sin(ang)
    cos2 = jnp.concatenate([cos, cos], axis=-1)
    sin2 = jnp.concatenate([-sin, sin], axis=-1)
    cos2 = jnp.concatenate([jnp.ones((n_ctx, HEAD_DIM), F32), cos2], axis=0)
    sin2 = jnp.concatenate([jnp.zeros((n_ctx, HEAD_DIM), F32), sin2], axis=0)
    return cos2, sin2


def _block_diag2(a, b):
    z = jnp.zeros_like(a)
    return jnp.concatenate([jnp.concatenate([a, z], axis=1), jnp.concatenate([z, b], axis=1)], axis=0)


def kernel(x, c, ctx, c_ctx, ada_w, ada_b, norm_g, final_g, mix_w_in, mix_w_out, q_norm_g, k_norm_g, conv_w, rw_mu, rw_w_r, rw_w_k, rw_w_v, rw_w_o, rw_w0, rw_w1, rw_w2, rw_a0, rw_a1, rw_a2, rw_g1, rw_g2, rw_k_k, rw_k_a, rw_r_k, rw_ln_w, rw_ln_b, moe_router, moe_w_gate, moe_w_up, moe_w_down):
    b, n, d = x.shape
    n_ctx = ctx.shape[1]
    depth = ada_w.shape[0]
    assert depth == 2 and n_ctx == TILE and n % TILE == 0 and d % LANES == 0

    rows = -(-(b + 1) // HALO) * HALO
    cc = jnp.concatenate([c, c_ctx[None, :], jnp.zeros((rows - b - 1, d), F32)], axis=0)
    mod = _ada_mod(cc, ada_w, ada_b)
    mod_lat = mod[:, :b].reshape(depth, b, 1, 6, d)
    mod_ctx = jnp.broadcast_to(mod[:, b].reshape(depth, 1, 1, 6, d), (depth, b, 1, 6, d))
    modsel = jnp.concatenate([mod_ctx, mod_lat], axis=2)

    router_t = jnp.swapaxes(moe_router, 1, 2).astype(BF16)

    cos2, sin2 = _rope_tables(n_ctx, n)
    qkv, conv = _in_proj(ctx, x, modsel[0], norm_g[0, 0][None], mix_w_in[0].astype(BF16),
                         q_norm_g[0][None], k_norm_g[0][None], cos2, sin2)
    o_attn = _attention(qkv, n_ctx)
    x1, h2, aff_t = _mix_out(o_attn, conv, ctx, x, modsel[0], mix_w_out[0].astype(BF16), conv_w[0],
                             norm_g[0, 1][None], router_t[0])
    xs = _moe_block(x1, h2, aff_t, [(n_ctx, n), (0, n_ctx)], modsel[0], final_g[None],
                    moe_w_gate, moe_w_up, moe_w_down, 0, tile_offset=0, final_norm=False)

    heads = d // RW_N
    assert heads <= LANES
    seg_r = (jnp.arange(d)[:, None] // RW_N == jnp.arange(LANES)[None, :]).astype(BF16)
    p = {
        "mu": rw_mu[0], "w_r": rw_w_r[0].astype(BF16), "w_k": rw_w_k[0].astype(BF16),
        "w_v": rw_w_v[0].astype(BF16), "w_o": rw_w_o[0].astype(BF16),
        "w1": jnp.concatenate([rw_w1[0, 0], rw_w1[0, 1]], axis=1).astype(BF16),
        "a1": jnp.concatenate([rw_a1[0, 0], rw_a1[0, 1]], axis=1).astype(BF16),
        "g1": rw_g1[0].astype(BF16),
        "w2": _block_diag2(rw_w2[0, 0], rw_w2[0, 1]).astype(BF16),
        "a2": _block_diag2(rw_a2[0, 0], rw_a2[0, 1]).astype(BF16),
        "g2": rw_g2[0].astype(BF16),
        "w0": rw_w0[0], "a0": rw_a0[0], "k_k": rw_k_k[0][None], "k_a": rw_k_a[0][None],
        "r_k": rw_r_k[0].reshape(1, d), "ln_w": rw_ln_w[0][None], "ln_b": rw_ln_b[0][None],
        "seg_r": seg_r, "seg_b": seg_r.T,
    }
    r, v, kk, g, lw, kd, bb = _rwkv_feat(xs, modsel[1], norm_g[1, 0][None], p)
    y_fwd = _rwkv_scan(lw, kd, bb, kk, r, v, n_ctx, rev=False)
    y_bwd = _rwkv_scan(lw, kd, bb, kk, r, v, n_ctx, rev=True)
    x1, h2, aff_t = _rwkv_out(y_fwd, y_bwd, r, v, g, kd, xs, modsel[1], p, norm_g[1, 1][None],
                              router_t[1], n_ctx)
    return _moe_block(x1, h2, aff_t, [(0, n)], modsel[1], final_g[None],
                      moe_w_gate, moe_w_up, moe_w_down, 1, tile_offset=1, final_norm=True)
```

```python
import functools

import jax
import jax.numpy as jnp
from jax import lax
from jax.experimental import pallas as pl
from jax.experimental.pallas import tpu as pltpu

F32 = jnp.float32
BF16 = jnp.bfloat16

NORM_EPS = 1e-6
GN_EPS = 64e-5
GRID_W = 64
ROPE_THETA = 10000.0
HEAD_DIM = 128
N_Q_HEADS = 8
N_KV_HEADS = 2
Q_PER_KV = N_Q_HEADS // N_KV_HEADS
RW_N = 64
N_EXPERTS = 16
EC_CAPACITY = 2
LOG2_E = 1.4426950408889634
DECAY_SCALE = 0.6065306597126334

LANES = 128
TILE = 256
HALO = 8
CHUNK = 64
SCAN_BLOCK = 256
GATHER_WIN = 64
MOE_FFN_SAMPLES = 4
SCATTER_WIN = 64
ROW_SAMPLES = 2
VMEM_LIMIT = 56 * 1024 * 1024


def _cparams(sem):
    return pltpu.CompilerParams(dimension_semantics=sem, vmem_limit_bytes=VMEM_LIMIT)


def _dot(a, b):
    return jnp.dot(a.astype(BF16), b.astype(BF16), preferred_element_type=F32)


def _dot_nt(a, b):
    return lax.dot_general(a.astype(BF16), b.astype(BF16), (((1,), (1,)), ((), ())),
                           preferred_element_type=F32)


def _dot_tn(a, b):
    return lax.dot_general(a.astype(BF16), b.astype(BF16), (((0,), (0,)), ((), ())),
                           preferred_element_type=F32)


def _split_dot(x, w, parts):
    acc = None
    rem = x
    for _ in range(parts):
        hi = rem.astype(BF16)
        term = jnp.dot(hi, w, preferred_element_type=F32)
        acc = term if acc is None else acc + term
        rem = rem - hi.astype(F32)
    return acc


def _segsum(x, seg_r, seg_b):
    return _split_dot(_split_dot(x, seg_r, 2), seg_b, 3)


def _rms_rows(x, g):
    ms = jnp.mean(x * x, axis=-1, keepdims=True)
    return x * lax.rsqrt(ms + NORM_EPS) * g


def _ada_rows(x, g, shift, scale):
    return _rms_rows(x, g) * (1.0 + scale) + shift


def _sigmoid(x):
    return 1.0 / (1.0 + jnp.exp(-x))


def _shift_rows(h, prev_row, next_row):
    n = h.shape[0]
    rows = lax.broadcasted_iota(jnp.int32, h.shape, 0)
    down = jnp.where(rows == 0, prev_row, pltpu.roll(h, 1, 0))
    up = jnp.where(rows == n - 1, next_row, pltpu.roll(h, n - 1, 0))
    return down, up


def _ada_mod_kernel(c_ref, w_ref, b_ref, o_ref):
    c = c_ref[...]
    o_ref[...] = _dot(c * _sigmoid(c), w_ref[...]) + b_ref[...]


def _ada_mod(cc, ada_w, ada_b):
    depth, d, n6 = ada_w.shape
    rows = cc.shape[0]
    nb = 1536
    return pl.pallas_call(
        _ada_mod_kernel,
        out_shape=jax.ShapeDtypeStruct((depth, rows, n6), F32),
        grid=(depth, n6 // nb),
        in_specs=[
            pl.BlockSpec((rows, d), lambda l, j: (0, 0)),
            pl.BlockSpec((None, d, nb), lambda l, j: (l, 0, j)),
            pl.BlockSpec((None, 1, nb), lambda l, j: (l, 0, j)),
        ],
        out_specs=pl.BlockSpec((None, rows, nb), lambda l, j: (l, 0, j)),
        compiler_params=_cparams(("arbitrary", "arbitrary")),
        name="ada_mod",
    )(cc, ada_w, ada_b.reshape(depth, 1, n6))


def _mod_spec(d):
    return pl.BlockSpec((None, None, 6, d), lambda b, t: (b, jnp.minimum(t, 1), 0, 0))


def _stream_tile(ctx_ref, x_ref):
    return jnp.where(pl.program_id(1) == 0, ctx_ref[...], x_ref[...])


def _stream_specs(d):
    return [pl.BlockSpec((None, TILE, d), lambda i, j: (i, 0, 0)),
            pl.BlockSpec((None, TILE, d), lambda i, j: (i, jnp.maximum(j - 1, 0), 0))]


def _in_proj_kernel(ctx_ref, x_ref, mod_ref, g_ref, w_ref, qg_ref, kg_ref, cos_ref, sin_ref,
                    qkv_ref, conv_ref, *, n_qkv, n_rot_heads, n_q_heads):
    ns = x_ref.shape[0]
    rows = [slice(s * TILE, (s + 1) * TILE) for s in range(ns)]
    h = jnp.concatenate(
        [_ada_rows(_stream_tile(ctx_ref.at[s], x_ref.at[s]), g_ref[...], mod_ref[s, 0:1, :], mod_ref[s, 1:2, :])
         for s in range(ns)], axis=0).astype(BF16)
    nb = 512
    n_all = w_ref.shape[1]
    cos2 = jnp.concatenate([cos_ref[...]] * ns, axis=0)
    sin2 = jnp.concatenate([sin_ref[...]] * ns, axis=0)
    for j in range(n_all // nb):
        p = jnp.dot(h, w_ref[:, j * nb:(j + 1) * nb], preferred_element_type=F32)
        if j * nb >= n_qkv:
            for s in range(ns):
                conv_ref[s, :, j * nb - n_qkv:(j + 1) * nb - n_qkv] = p[rows[s]]
            continue
        for i in range(nb // HEAD_DIM):
            hd = j * (nb // HEAD_DIM) + i
            t = p[:, i * HEAD_DIM:(i + 1) * HEAD_DIM]
            if hd < n_rot_heads:
                t = _rms_rows(t, qg_ref[...] if hd < n_q_heads else kg_ref[...])
                t = t * cos2 + pltpu.roll(t, HEAD_DIM // 2, 1) * sin2
            if hd < n_q_heads:
                t = t * ((HEAD_DIM ** -0.5) * LOG2_E)
            t = t.astype(BF16)
            for s in range(ns):
                qkv_ref[s, :, hd * HEAD_DIM:(hd + 1) * HEAD_DIM] = t[rows[s]]


def _in_proj(ctx, x, modsel, g, w_in, q_g, k_g, cos2, sin2):
    b, n, d = x.shape
    t = ctx.shape[1] + n
    n_all = w_in.shape[1]
    n_qkv = (N_Q_HEADS + 2 * N_KV_HEADS) * HEAD_DIM
    ns = ROW_SAMPLES if b % ROW_SAMPLES == 0 else 1
    kern = functools.partial(_in_proj_kernel, n_qkv=n_qkv, n_rot_heads=N_Q_HEADS + N_KV_HEADS,
                             n_q_heads=N_Q_HEADS)
    return pl.pallas_call(
        kern,
        out_shape=(jax.ShapeDtypeStruct((b, t, n_qkv), BF16),
                   jax.ShapeDtypeStruct((b, t, n_all - n_qkv), F32)),
        grid=(b // ns, t // TILE),
        in_specs=[
            pl.BlockSpec((ns, TILE, d), lambda i, j: (i, 0, 0)),
            pl.BlockSpec((ns, TILE, d), lambda i, j: (i, jnp.maximum(j - 1, 0), 0)),
            pl.BlockSpec((ns, None, 6, d), lambda i, j: (i, jnp.minimum(j, 1), 0, 0)),
            pl.BlockSpec((1, d), lambda i, j: (0, 0)),
            pl.BlockSpec((d, n_all), lambda i, j: (0, 0)),
            pl.BlockSpec((1, HEAD_DIM), lambda i, j: (0, 0)),
            pl.BlockSpec((1, HEAD_DIM), lambda i, j: (0, 0)),
            pl.BlockSpec((TILE, HEAD_DIM), lambda i, j: (j, 0)),
            pl.BlockSpec((TILE, HEAD_DIM), lambda i, j: (j, 0)),
        ],
        out_specs=(pl.BlockSpec((ns, TILE, n_qkv), lambda i, j: (i, j, 0)),
                   pl.BlockSpec((ns, TILE, n_all - n_qkv), lambda i, j: (i, j, 0))),
        compiler_params=_cparams(("arbitrary", "arbitrary")),
        name="in_proj",
    )(ctx, x, modsel, g, w_in, q_g, k_g, cos2, sin2)


def _attn_kernel(q_ref, k_ref, v_ref, o_ref, *, n_ctx):
    heads = [slice(g * HEAD_DIM, (g + 1) * HEAD_DIM) for g in range(Q_PER_KV)]

    def attend(nk):
        k = k_ref[0:nk, :]
        v = v_ref[0:nk, :]
        s = _dot_nt(q_ref[:, heads[0]], k)
        for g in range(Q_PER_KV):
            s_next = _dot_nt(q_ref[:, heads[g + 1]], k) if g + 1 < Q_PER_KV else None
            p = jnp.exp2(s - jnp.max(s, axis=-1, keepdims=True))
            l = jnp.sum(p, axis=-1, keepdims=True)
            o_ref[:, heads[g]] = jnp.dot(p.astype(BF16), v, preferred_element_type=F32) / l
            s = s_next

    is_ctx = pl.program_id(2) == 0

    @pl.when(is_ctx)
    def _():
        attend(n_ctx)

    @pl.when(jnp.logical_not(is_ctx))
    def _():
        attend(k_ref.shape[0])


def _attention(qkv, n_ctx):
    b, t, _ = qkv.shape
    kern = functools.partial(_attn_kernel, n_ctx=n_ctx)
    return pl.pallas_call(
        kern,
        out_shape=jax.ShapeDtypeStruct((b, t, N_Q_HEADS * HEAD_DIM), F32),
        grid=(b, N_KV_HEADS, t // TILE),
        in_specs=[
            pl.BlockSpec((None, TILE, Q_PER_KV * HEAD_DIM), lambda i, kv, j: (i, j, kv)),
            pl.BlockSpec((None, t, HEAD_DIM), lambda i, kv, j: (i, 0, N_Q_HEADS + kv)),
            pl.BlockSpec((None, t, HEAD_DIM), lambda i, kv, j: (i, 0, N_Q_HEADS + N_KV_HEADS + kv)),
        ],
        out_specs=pl.BlockSpec((None, TILE, Q_PER_KV * HEAD_DIM), lambda i, kv, j: (i, j, kv)),
        compiler_params=_cparams(("arbitrary",) * 3),
        name="attention",
    )(qkv, qkv, qkv)


def _residual_and_router(x, y, mod_ref, g2_ref, router_ref, x1_ref, h2_ref, aff_ref):
    x1 = x + mod_ref[2:3, :] * y
    x1_ref[...] = x1
    h2 = _ada_rows(x1, g2_ref[...], mod_ref[3:4, :], mod_ref[4:5, :]).astype(BF16)
    h2_ref[...] = h2
    logits = _dot_nt(router_ref[...], h2)
    e = jnp.exp(logits - jnp.max(logits, axis=0, keepdims=True))
    aff_ref[...] = e / jnp.sum(e, axis=0, keepdims=True)


def _mix_out_kernel(oa_ref, gb_ref, gc_ref, xc_ref, gcp_ref, xcp_ref, gcn_ref, xcn_ref,
                    ctx_ref, x_ref, mod_ref, wout_ref, cw_ref, g2_ref, router_ref,
                    x1_ref, h2_ref, aff_ref, *, n_tiles):
    t = pl.program_id(1)
    has_prev = (t >= 2).astype(F32)
    has_next = jnp.logical_and(t >= 1, t <= n_tiles - 2).astype(F32)
    u = gc_ref[...] * xc_ref[...]
    u_prev = gcp_ref[HALO - 1:HALO, :] * xcp_ref[HALO - 1:HALO, :] * has_prev
    u_next = gcn_ref[0:1, :] * xcn_ref[0:1, :] * has_next
    down, up = _shift_rows(u, u_prev, u_next)
    o_conv = gb_ref[...] * (down * cw_ref[0:1, :] + u * cw_ref[1:2, :] + up * cw_ref[2:3, :])
    d_attn = oa_ref.shape[1]
    y = _dot(oa_ref[...], wout_ref[0:d_attn, :]) + _dot(o_conv, wout_ref[d_attn:, :])
    _residual_and_router(_stream_tile(ctx_ref, x_ref), y, mod_ref, g2_ref, router_ref, x1_ref, h2_ref,
                         aff_ref)


def _mix_out(o_attn, conv, ctx, x, modsel, w_out, conv_w, g2, router_t):
    b, t, d = o_attn.shape
    n_tiles = t // TILE
    hb = TILE // HALO
    last = t // HALO - 1
    prev_map = lambda col: (lambda i, j: (i, jnp.maximum(j * hb - 1, 0), col))
    next_map = lambda col: (lambda i, j: (i, jnp.minimum((j + 1) * hb, last), col))
    tile_map = lambda col: (lambda i, j: (i, j, col))
    n_e = router_t.shape[0]
    kern = functools.partial(_mix_out_kernel, n_tiles=n_tiles)
    return pl.pallas_call(
        kern,
        out_shape=(jax.ShapeDtypeStruct((b, t, d), F32),
                   jax.ShapeDtypeStruct((b, t, d), BF16),
                   jax.ShapeDtypeStruct((b, n_e, t), F32)),
        grid=(b, n_tiles),
        in_specs=[
            pl.BlockSpec((None, TILE, d), tile_map(0)),
            pl.BlockSpec((None, TILE, d), tile_map(0)),
            pl.BlockSpec((None, TILE, d), tile_map(1)),
            pl.BlockSpec((None, TILE, d), tile_map(2)),
            pl.BlockSpec((None, HALO, d), prev_map(1)),
            pl.BlockSpec((None, HALO, d), prev_map(2)),
            pl.BlockSpec((None, HALO, d), next_map(1)),
            pl.BlockSpec((None, HALO, d), next_map(2)),
        ] + _stream_specs(d) + [
            _mod_spec(d),
            pl.BlockSpec(w_out.shape, lambda i, j: (0, 0)),
            pl.BlockSpec(conv_w.shape, lambda i, j: (0, 0)),
            pl.BlockSpec((1, d), lambda i, j: (0, 0)),
            pl.BlockSpec((n_e, d), lambda i, j: (0, 0)),
        ],
        out_specs=(pl.BlockSpec((None, TILE, d), tile_map(0)),
                   pl.BlockSpec((None, TILE, d), tile_map(0)),
                   pl.BlockSpec((None, n_e, TILE), lambda i, j: (i, 0, j))),
        compiler_params=_cparams(("arbitrary", "arbitrary")),
        name="mix_out",
    )(o_attn, conv, conv, conv, conv, conv, conv, conv, ctx, x, modsel, w_out, conv_w, g2, router_t)


def _slot_windows(bnd_ref, sample, tile, n_e, slot0, cap, win, group):
    starts, fits = [], []
    for e in range(n_e):
        lo = bnd_ref[sample, e, tile]
        hi = bnd_ref[sample, n_e + e, tile]
        w0 = jnp.minimum((lo >> 4) << 4, slot0 + cap - win)
        ok = hi <= w0 + win
        if e % group == 0:
            fits.append(ok)
        else:
            fits[-1] = jnp.logical_and(fits[-1], ok)
        starts.append(pl.multiple_of(w0, 16))
    return starts, fits


def _moe_gather_kernel(bnd_ref, slot_ref, h_ref, x_ref, *, sets):
    n_e = x_ref.shape[0]
    n_tok = slot_ref.shape[1]
    sample = pl.program_id(0)
    tile = pl.program_id(1)

    @pl.when(tile == 0)
    def _():
        x_ref[...] = jnp.zeros_like(x_ref)

    def onehot(e, first, n):
        rows = lax.broadcasted_iota(jnp.int32, (n, n_tok), 0) + first
        return jnp.where(slot_ref[e:e + 1, :] == rows.astype(F32), 1.0, 0.0).astype(BF16)

    for start, length, slot0, cap in sets:
        first_tile, last_tile = start // TILE, (start + length) // TILE
        win = min(GATHER_WIN, cap)

        @pl.when(jnp.logical_and(tile >= first_tile, tile < last_tile))
        def _():
            starts, (fits,) = _slot_windows(bnd_ref, sample, tile, n_e, slot0, cap, win, group=n_e)

            @pl.when(fits)
            def _():
                oh = jnp.concatenate([onehot(e, starts[e], win) for e in range(n_e)], axis=0)
                part = jnp.dot(oh, h_ref[...], preferred_element_type=F32).astype(BF16)
                for e in range(n_e):
                    x_ref[e, pl.ds(starts[e], win), :] += part[e * win:(e + 1) * win]

            @pl.when(jnp.logical_not(fits))
            def _():
                for e in range(n_e):
                    x_ref[e, slot0:slot0 + cap, :] += jnp.dot(
                        onehot(e, slot0, cap), h_ref[...], preferred_element_type=F32).astype(BF16)


def _moe_gather(slots, bounds, h2, sets, n_slots):
    b, n_e, t = slots.shape
    d = h2.shape[2]
    grid_spec = pltpu.PrefetchScalarGridSpec(
        num_scalar_prefetch=1,
        grid=(b, t // TILE),
        in_specs=[
            pl.BlockSpec((None, n_e, TILE), lambda i, j, bnd: (i, 0, j)),
            pl.BlockSpec((None, TILE, d), lambda i, j, bnd: (i, j, 0)),
        ],
        out_specs=pl.BlockSpec((None, n_e, n_slots, d), lambda i, j, bnd: (i, 0, 0, 0)),
    )
    return pl.pallas_call(
        functools.partial(_moe_gather_kernel, sets=sets),
        out_shape=jax.ShapeDtypeStruct((b, n_e, n_slots, d), BF16),
        grid_spec=grid_spec,
        compiler_params=_cparams(("arbitrary", "arbitrary")),
        name="moe_gather",
    )(bounds, slots, h2)


def _moe_ffn_kernel(slot_ref, aff_ref, x_ref, wg_ref, wu_ref, wd_ref, o_ref, wg_s, wu_s, wd_s):
    @pl.when(pl.program_id(1) == 0)
    def _():
        wg_s[...] = wg_ref[...].astype(BF16)
        wu_s[...] = wu_ref[...].astype(BF16)
        wd_s[...] = wd_ref[...].astype(BF16)

    nb, r, d = x_ref.shape
    rows = lax.broadcasted_iota(jnp.int32, (r, slot_ref.shape[2]), 0).astype(F32)
    a_row = jnp.concatenate(
        [jnp.sum(jnp.where(slot_ref[i] == rows, aff_ref[i], 0.0), axis=-1, keepdims=True) for i in range(nb)],
        axis=0)
    x = x_ref[...].reshape(nb * r, d)
    gate = jnp.dot(x, wg_s[...], preferred_element_type=F32)
    up = jnp.dot(x, wu_s[...], preferred_element_type=F32)
    act = (gate * _sigmoid(gate) * up).astype(BF16)
    out = jnp.dot(act, wd_s[...], preferred_element_type=F32) * a_row
    o_ref[...] = out.astype(BF16).reshape(nb, r, d)


def _moe_ffn(slots, aff_t, xs, w_gate, w_up, w_down, layer):
    b, n_e, t = slots.shape
    _, _, r, d = xs.shape
    f = w_gate.shape[3]
    nb = MOE_FFN_SAMPLES if b % MOE_FFN_SAMPLES == 0 else 1
    row = pl.BlockSpec((nb, None, 1, t), lambda e, i: (i, e, 0, 0))
    tok = pl.BlockSpec((nb, None, r, d), lambda e, i: (i, e, 0, 0))
    return pl.pallas_call(
        _moe_ffn_kernel,
        out_shape=jax.ShapeDtypeStruct((b, n_e, r, d), BF16),
        grid=(n_e, b // nb),
        in_specs=[
            row, row, tok,
            pl.BlockSpec((None, None, d, f), lambda e, i: (layer, e, 0, 0)),
            pl.BlockSpec((None, None, d, f), lambda e, i: (layer, e, 0, 0)),
            pl.BlockSpec((None, None, f, d), lambda e, i: (layer, e, 0, 0)),
        ],
        out_specs=tok,
        scratch_shapes=[pltpu.VMEM((d, f), BF16), pltpu.VMEM((d, f), BF16), pltpu.VMEM((f, d), BF16)],
        compiler_params=_cparams(("arbitrary", "arbitrary")),
        name="moe_ffn",
    )(slots.reshape(b, n_e, 1, t), aff_t.reshape(b, n_e, 1, t), xs, w_gate, w_up, w_down)


def _prefix_count(mask, tri):
    return _prefix_count_blocks(mask, tri)[0]


def _prefix_count_blocks(mask, tri):
    blk = tri.shape[0]
    carry = jnp.zeros((mask.shape[0], 1), F32)
    parts, carries = [], [carry]
    for j in range(mask.shape[1] // blk):
        m = mask[:, j * blk:(j + 1) * blk]
        parts.append(jnp.dot(m.astype(BF16), tri, preferred_element_type=F32) + carry)
        carry = carry + jnp.sum(m, axis=-1, keepdims=True)
        carries.append(carry)
    return jnp.concatenate(parts, axis=-1), carries


def _route_kernel(aff_ref, slot_ref, bound_ref, *, sets):
    blk = TILE
    n_e = aff_ref.shape[0]
    tri = jnp.where(lax.broadcasted_iota(jnp.int32, (blk, blk), 0)
                    <= lax.broadcasted_iota(jnp.int32, (blk, blk), 1), 1.0, 0.0).astype(BF16)
    lane = lax.broadcasted_iota(jnp.int32, (n_e, LANES), 1)
    lo = jnp.zeros((n_e, LANES), F32)
    hi = jnp.zeros((n_e, LANES), F32)
    for start, length, slot0, cap in sets:
        cap = float(cap)
        x = aff_ref[:, start:start + length]
        bits = pltpu.bitcast(x, jnp.int32)
        thr = jnp.zeros((bits.shape[0], 1), jnp.int32)
        for bit in range(30, -1, -1):
            cand = thr | (1 << bit)
            cnt = jnp.sum(jnp.where(bits >= cand, 1.0, 0.0), axis=-1, keepdims=True)
            thr = jnp.where(cnt >= cap, cand, thr)
        pivot = jnp.max(jnp.where(bits <= thr, x, -1.0), axis=-1, keepdims=True)

        def counts(p):
            return (jnp.sum(jnp.where(x > p, 1.0, 0.0), axis=-1, keepdims=True),
                    jnp.sum(jnp.where(x >= p, 1.0, 0.0), axis=-1, keepdims=True))

        def misplaced(p):
            n_gt, n_ge = counts(p)
            return jnp.max(jnp.where(jnp.logical_or(n_gt >= cap, n_ge < cap), 1.0, 0.0)) > 0.0

        def walk(p):
            n_gt, n_ge = counts(p)
            higher = jnp.min(jnp.where(x > p, x, jnp.inf), axis=-1, keepdims=True)
            lower = jnp.max(jnp.where(x < p, x, -1.0), axis=-1, keepdims=True)
            return jnp.where(n_gt >= cap, higher, jnp.where(n_ge < cap, lower, p))

        pivot = lax.while_loop(misplaced, walk, pivot)
        above = jnp.where(x > pivot, 1.0, 0.0)
        tied = jnp.where(x == pivot, 1.0, 0.0)
        need = cap - jnp.sum(above, axis=-1, keepdims=True)
        chosen = above + jnp.where(_prefix_count(tied, tri) <= need, tied, 0.0)
        count, before = _prefix_count_blocks(chosen, tri)
        slot_ref[:, start:start + length] = jnp.where(chosen > 0.0, count - 1.0 + slot0, -1.0)
        for j in range(length // blk):
            here = lane == (start // blk + j)
            lo = jnp.where(here, before[j] + slot0, lo)
            hi = jnp.where(here, before[j + 1] + slot0, hi)
    bound_ref[0:n_e, :] = lo
    bound_ref[n_e:, :] = hi


def _route(aff_t, sets):
    b, n_e, t = aff_t.shape
    assert t // TILE <= LANES
    kern = functools.partial(_route_kernel, sets=tuple(sets))
    spec = pl.BlockSpec((None, n_e, t), lambda i: (i, 0, 0))
    bspec = pl.BlockSpec((None, 2 * n_e, LANES), lambda i: (i, 0, 0))
    return pl.pallas_call(
        kern,
        out_shape=(jax.ShapeDtypeStruct((b, n_e, t), F32),
                   jax.ShapeDtypeStruct((b, 2 * n_e, LANES), F32)),
        grid=(b,),
        in_specs=[spec],
        out_specs=(spec, bspec),
        compiler_params=_cparams(("arbitrary",)),
        name="moe_route",
    )(aff_t)


def _combine_kernel(bnd_ref, slot_ref, o_ref, x_ref, mod_ref, g_ref, out_ref, *, sets, final_norm):
    n_e = o_ref.shape[0]
    n_tok = slot_ref.shape[1]
    sample = pl.program_id(0)
    tile = pl.program_id(1)
    tn = lambda a, w: lax.dot_general(a, w, (((0,), (0,)), ((), ())), preferred_element_type=F32)

    def onehot(e, first, n):
        rows = lax.broadcasted_iota(jnp.int32, (n, n_tok), 0) + first
        return jnp.where(slot_ref[e:e + 1, :] == rows.astype(F32), 1.0, 0.0).astype(BF16)

    def scatter_full(experts, slot0, cap):
        y = None
        for e in experts:
            term = tn(onehot(e, slot0, cap), o_ref[e, slot0:slot0 + cap, :])
            y = term if y is None else y + term
        return y

    for start, length, slot0, cap in sets:
        first_tile, last_tile = start // TILE, (start + length) // TILE

        def finish(y):
            x2 = x_ref[...] + mod_ref[5:6, :] * y
            out_ref[...] = _rms_rows(x2, g_ref[...]) if final_norm else x2

        @pl.when(jnp.logical_and(tile >= first_tile, tile < last_tile))
        def _():
            win = min(SCATTER_WIN, cap)
            starts, (fits,) = _slot_windows(bnd_ref, sample, tile, n_e, slot0, cap, win, group=n_e)

            @pl.when(fits)
            def _():
                oh = jnp.concatenate([onehot(e, starts[e], win) for e in range(n_e)], axis=0)
                rows = jnp.concatenate([o_ref[e, pl.ds(starts[e], win), :] for e in range(n_e)], axis=0)
                finish(tn(oh, rows))

            @pl.when(jnp.logical_not(fits))
            def _():
                finish(scatter_full(range(n_e), slot0, cap))


def _combine(slots, bounds, outs, x1, modsel, g, sets, *, tile_offset, final_norm):
    b, t, d = x1.shape
    _, n_e, r, _ = outs.shape
    assert all(s[2] % 16 == 0 and s[3] % 16 == 0 for s in sets)
    kern = functools.partial(_combine_kernel, sets=sets, final_norm=final_norm)
    grid_spec = pltpu.PrefetchScalarGridSpec(
        num_scalar_prefetch=1,
        grid=(b, t // TILE),
        in_specs=[
            pl.BlockSpec((None, n_e, TILE), lambda i, j, bnd: (i, 0, j)),
            pl.BlockSpec((None, n_e, r, d), lambda i, j, bnd: (i, 0, 0, 0)),
            pl.BlockSpec((None, TILE, d), lambda i, j, bnd: (i, j, 0)),
            pl.BlockSpec((None, None, 6, d), lambda i, j, bnd: (i, jnp.minimum(j + tile_offset, 1), 0, 0)),
            pl.BlockSpec((1, d), lambda i, j, bnd: (0, 0)),
        ],
        out_specs=pl.BlockSpec((None, TILE, d), lambda i, j, bnd: (i, j, 0)),
    )
    return pl.pallas_call(
        kern,
        out_shape=jax.ShapeDtypeStruct((b, t, d), F32),
        grid_spec=grid_spec,
        compiler_params=_cparams(("arbitrary", "arbitrary")),
        name="moe_combine",
    )(bounds, slots, outs, x1, modsel, g)


def _rwkv_feat_kernel(x_ref, xp_ref, xn_ref, mod_ref, g_ref, mu_ref, wr_ref, wk_ref, wv_ref,
                      w1_ref, a1_ref, g1_ref, w2_ref, a2_ref, g2_ref, w0_ref, a0_ref,
                      kk_ref, ka_ref, segr_ref, segb_ref,
                      r_out, v_out, kkn_out, g_out, lw_out, kd_out, bb_out, *, n_tiles):
    t = pl.program_id(1)
    has_prev = (t >= 2).astype(F32)
    has_next = jnp.logical_and(t >= 1, t <= n_tiles - 2).astype(F32)
    g = g_ref[...]
    shift = mod_ref[0:1, :]
    scale = mod_ref[1:2, :]
    h = _ada_rows(x_ref[...], g, shift, scale)
    h_prev = _ada_rows(xp_ref[HALO - 1:HALO, :], g, shift, scale) * has_prev
    h_next = _ada_rows(xn_ref[0:1, :], g, shift, scale) * has_next
    down, up = _shift_rows(h, h_prev, h_next)
    xx = 0.5 * (down + up) - h
    d = h.shape[1]

    r_out[...] = _dot(h + xx * mu_ref[0:1, :], wr_ref[...])
    k = _dot(h + xx * mu_ref[2:3, :], wk_ref[...])
    v_out[...] = _dot(h + xx * mu_ref[3:4, :], wv_ref[...])
    gate = _sigmoid(_dot(h + xx * mu_ref[5:6, :], g1_ref[...]))
    g_out[...] = _dot(gate, g2_ref[...])

    kkp = k * kk_ref[...]
    ss = _segsum(kkp * kkp, segr_ref[...], segb_ref[...])
    kkn = kkp * lax.rsqrt(jnp.maximum(ss, 1e-24))
    kkn_out[...] = kkn

    lora_w = _dot(jnp.tanh(_dot(h + xx * mu_ref[1:2, :], w1_ref[...])), w2_ref[...])
    lora_a = _dot(_dot(h + xx * mu_ref[4:5, :], a1_ref[...]), a2_ref[...])
    for di in range(2):
        sl = slice(di * d, (di + 1) * d)
        lw_out[di] = -DECAY_SCALE * _sigmoid(w0_ref[di:di + 1, :] + lora_w[:, sl])
        a = _sigmoid(a0_ref[di:di + 1, :] + lora_a[:, sl])
        kd_out[di] = k * (1.0 + (a - 1.0) * ka_ref[...])
        bb_out[di] = a * kkn


def _rwkv_feat(xs, modsel, g, p):
    b, t, d = xs.shape
    n_tiles = t // TILE
    hb = TILE // HALO
    last = t // HALO - 1
    full = lambda a: pl.BlockSpec(a.shape, lambda i, j: (0,) * a.ndim)
    weights = [p["mu"], p["w_r"], p["w_k"], p["w_v"], p["w1"], p["a1"], p["g1"], p["w2"], p["a2"],
               p["g2"], p["w0"], p["a0"], p["k_k"], p["k_a"], p["seg_r"], p["seg_b"]]
    tile_spec = pl.BlockSpec((None, TILE, d), lambda i, j: (i, j, 0))
    dir_spec = pl.BlockSpec((2, None, TILE, d), lambda i, j: (0, i, j, 0))
    kern = functools.partial(_rwkv_feat_kernel, n_tiles=n_tiles)
    one = jax.ShapeDtypeStruct((b, t, d), F32)
    two = jax.ShapeDtypeStruct((2, b, t, d), F32)
    return pl.pallas_call(
        kern,
        out_shape=(one, one, one, one, two, two, two),
        grid=(b, n_tiles),
        in_specs=[
            tile_spec,
            pl.BlockSpec((None, HALO, d), lambda i, j: (i, jnp.maximum(j * hb - 1, 0), 0)),
            pl.BlockSpec((None, HALO, d), lambda i, j: (i, jnp.minimum((j + 1) * hb, last), 0)),
            _mod_spec(d),
            pl.BlockSpec((1, d), lambda i, j: (0, 0)),
        ] + [full(a) for a in weights],
        out_specs=(tile_spec, tile_spec, tile_spec, tile_spec, dir_spec, dir_spec, dir_spec),
        compiler_params=_cparams(("arbitrary", "arbitrary")),
        name="rwkv_feat",
    )(xs, xs, xs, modsel, g, *weights)


def _scan_kernel(lw_ref, kd_ref, bb_ref, kk_ref, r_ref, v_ref, y_ref, s_ref, *, rev):
    @pl.when(pl.program_id(1) == 0)
    def _():
        s_ref[...] = jnp.zeros_like(s_ref)

    L = CHUNK
    ti = lax.broadcasted_iota(jnp.int32, (L, LANES), 0)
    ii = lax.broadcasted_iota(jnp.int32, (L, LANES), 1) & (L - 1)
    before = (ii > ti) if rev else (ii < ti)
    before_eq = (ii >= ti) if rev else (ii <= ti)
    t2 = lax.broadcasted_iota(jnp.int32, (L, L), 0)
    i2 = lax.broadcasted_iota(jnp.int32, (L, L), 1)
    tri = jnp.where((i2 >= t2) if rev else (i2 <= t2), 1.0, 0.0).astype(BF16)
    r128 = lax.broadcasted_iota(jnp.int32, (LANES, LANES), 0)
    c128 = lax.broadcasted_iota(jnp.int32, (LANES, LANES), 1)
    head_eq = (r128 >> 6) == (c128 >> 6)
    eye = r128 == c128

    def same_block(log2s):
        return (ti >> log2s) == (ii >> log2s)

    def bdiag(y):
        yb = y.astype(BF16)
        return jnp.where(head_eq, jnp.concatenate([yb, yb], axis=0), jnp.zeros((), BF16))

    def hmul(x, y):
        return jnp.dot(x.astype(BF16), bdiag(y), preferred_element_type=F32)

    def hmul2(x, y, z):
        return jnp.dot(x.astype(BF16), jnp.concatenate([bdiag(y), bdiag(z)], axis=1),
                       preferred_element_type=F32)

    n_sub = lw_ref.shape[0] // L
    n_pair = lw_ref.shape[1] // LANES
    sub_order = list(range(n_sub))[::-1] if rev else list(range(n_sub))
    chains = [(slice(s * L, (s + 1) * L), slice(p * LANES, (p + 1) * LANES))
              for s in sub_order for p in range(n_pair)]

    def each(f, *cols):
        return [f(*args) for args in zip(*cols)]

    lam_sub = {s: _split_dot_left(tri, lw_ref[s * L:(s + 1) * L, :]) for s in sub_order}
    lw = [lw_ref[rs_, cs] for rs_, cs in chains]
    lam = [lam_sub[rs_.start // L][:, cs] for rs_, cs in chains]
    kk = [kk_ref[rs_, cs] for rs_, cs in chains]
    bb = [bb_ref[rs_, cs] for rs_, cs in chains]
    kd = [kd_ref[rs_, cs] for rs_, cs in chains]
    r = [r_ref[rs_, cs] for rs_, cs in chains]
    v = [v_ref[rs_, cs] for rs_, cs in chains]

    tot = each(lambda a: jnp.sum(a, axis=0, keepdims=True), lw)
    rho = each(lambda a: a[L // 2:L // 2 + 1, :], lam)
    ks = each(lambda a, b, c: a * jnp.exp(b - c), kk, lam, lw)
    rs = each(lambda a, b: a * jnp.exp(b), r, lam)
    e_neg = each(lambda a: jnp.exp(-a), rho)
    e_inv = each(lambda a, b: jnp.exp(a - b), rho, lam)
    e_end = each(lambda a, b: jnp.exp(a - b), tot, lam)

    lhs = each(lambda a, b, c: jnp.concatenate([a * c, b * c], axis=0).astype(BF16), ks, rs, e_neg)
    nt = lambda a, w: lax.dot_general(a, w, (((1,), (1,)), ((), ())), preferred_element_type=F32)
    g_bk = each(lambda a, b, c, d: nt(a, jnp.concatenate([bdiag(b * d), bdiag(c * d)], axis=0)),
                lhs, bb, kd, e_inv)
    a_b = each(lambda a: jnp.where(before, a[0:L, 0:LANES], 0.0), g_bk)
    a_k = each(lambda a: jnp.where(before, a[0:L, LANES:], 0.0), g_bk)
    m_b = each(lambda a: jnp.where(before_eq, a[L:, 0:LANES], 0.0), g_bk)
    m_k = each(lambda a: jnp.where(before_eq, a[L:, LANES:], 0.0), g_bk)
    akv = each(hmul, a_k, v)

    ident = jnp.where(ti == ii, 1.0, 0.0)
    blk8 = same_block(3)
    n1 = each(lambda a: jnp.where(blk8, -a, 0.0), a_b)
    n2 = each(hmul, n1, n1)
    n3 = each(hmul, n1, n2)
    n4 = each(hmul, n2, n2)
    x = each(lambda a, b, c: ident + a + b + c, n1, n2, n3)
    tinv = each(lambda a, b: a + hmul(a, b), x, n4)
    for log2s in (3, 4, 5):
        off_mask = jnp.logical_and(same_block(log2s + 1), jnp.logical_not(same_block(log2s)))
        off = each(lambda a: jnp.where(off_mask, a, 0.0), a_b)
        corr = each(hmul, off, tinv)
        tinv = each(lambda a, b: a - hmul(a, b), tinv, corr)

    qp = each(lambda a, b, c: -hmul2(a, b, c), tinv, ks, akv)
    q1 = each(lambda a: a[:, 0:LANES], qp)
    p0 = each(lambda a: a[:, LANES:], qp)
    mqp = each(hmul2, m_b, q1, p0)
    q2 = each(lambda a, b: a + b[:, 0:LANES], rs, mqp)
    y0 = each(lambda a, b, c: a[:, LANES:] + hmul(b, c), mqp, m_k, v)
    end = each(lambda a, b, c: jnp.concatenate([a * c, b * c], axis=0), bb, kd, e_end)
    zero = jnp.zeros((L, LANES), F32)
    mn = each(lambda a, b, c, d: _dot_tn(a, jnp.concatenate(
        [jnp.concatenate([b, c], axis=1), jnp.concatenate([zero, d], axis=1)], axis=0)),
        end, q1, p0, v)
    m_t = each(lambda a, c: jnp.where(head_eq, a[:, 0:LANES], 0.0) + jnp.where(eye, jnp.exp(c), 0.0),
               mn, tot)
    n0 = each(lambda a: jnp.where(head_eq, a[:, LANES:], 0.0), mn)
    qm = each(lambda a, b: jnp.concatenate([a, b], axis=0).astype(BF16), q2, m_t)

    state = [s_ref[p] for p in range(n_pair)]
    for idx, (rs_, cs) in enumerate(chains):
        p = idx % n_pair
        out = jnp.dot(qm[idx], state[p].astype(BF16), preferred_element_type=F32)
        y_ref[rs_, cs] = out[0:L] + y0[idx]
        state[p] = out[L:] + n0[idx]
    for p in range(n_pair):
        s_ref[p] = state[p]


def _split_dot_left(tri, x):
    acc = None
    rem = x
    for _ in range(3):
        hi = rem.astype(BF16)
        term = jnp.dot(tri, hi, preferred_element_type=F32)
        acc = term if acc is None else acc + term
        rem = rem - hi.astype(F32)
    return acc


def _rwkv_scan(lw, kd, bb, kk, r, v, n_ctx, rev):
    _, b, t, d = lw.shape
    nb_ctx = n_ctx // SCAN_BLOCK
    nb = t // SCAN_BLOCK
    di = 1 if rev else 0

    def block(j):
        return jnp.where(j < nb_ctx, nb_ctx - 1 - j, nb + nb_ctx - 1 - j) if rev else j

    dir_spec = pl.BlockSpec((None, None, SCAN_BLOCK, d), lambda i, j: (di, i, block(j), 0))
    one_spec = pl.BlockSpec((None, SCAN_BLOCK, d), lambda i, j: (i, block(j), 0))
    return pl.pallas_call(
        functools.partial(_scan_kernel, rev=rev),
        out_shape=jax.ShapeDtypeStruct((b, t, d), F32),
        grid=(b, nb),
        in_specs=[dir_spec, dir_spec, dir_spec, one_spec, one_spec, one_spec],
        out_specs=one_spec,
        scratch_shapes=[pltpu.VMEM((d // LANES, LANES, LANES), F32)],
        compiler_params=_cparams(("arbitrary", "arbitrary")),
        name="rwkv_scan_bwd" if rev else "rwkv_scan_fwd",
    )(lw, kd, bb, kk, r, v)


def _rwkv_out_kernel(yf_ref, yb_ref, r_ref, v_ref, g_ref, kdf_ref, kdb_ref, x_ref, mod_ref,
                     wo_ref, lnw_ref, lnb_ref, rk_ref, segr_ref, segb_ref, g2_ref, router_ref,
                     x1_ref, h2_ref, aff_ref):
    seg_r = segr_ref[...]
    seg_b = segb_ref[...]
    inv_n = 1.0 / RW_N
    y = yf_ref[...] + yb_ref[...]
    mean = _segsum(y, seg_r, seg_b) * inv_n
    dev = y - mean
    var = _segsum(dev * dev, seg_r, seg_b) * inv_n
    yn = dev * lax.rsqrt(var + GN_EPS) * lnw_ref[...] + lnb_ref[...]
    r = r_ref[...]
    rk = rk_ref[...]
    bonus = _segsum(r * kdf_ref[...] * rk + r * kdb_ref[...] * rk, seg_r, seg_b) * v_ref[...]
    out = _dot((yn + bonus) * g_ref[...], wo_ref[...])
    _residual_and_router(x_ref[...], out, mod_ref, g2_ref, router_ref, x1_ref, h2_ref, aff_ref)


def _rwkv_out(y_fwd, y_bwd, r, v, g, kd, xs, modsel, p, g2, router_t, n_ctx):
    b, t, d = xs.shape
    off = n_ctx // TILE
    n = t - n_ctx
    n_e = router_t.shape[0]
    tile = pl.BlockSpec((None, TILE, d), lambda i, j: (i, j + off, 0))
    dir0 = pl.BlockSpec((None, None, TILE, d), lambda i, j: (0, i, j + off, 0))
    dir1 = pl.BlockSpec((None, None, TILE, d), lambda i, j: (1, i, j + off, 0))
    vec = pl.BlockSpec((1, d), lambda i, j: (0, 0))
    out_tile = pl.BlockSpec((None, TILE, d), lambda i, j: (i, j, 0))
    return pl.pallas_call(
        _rwkv_out_kernel,
        out_shape=(jax.ShapeDtypeStruct((b, n, d), F32),
                   jax.ShapeDtypeStruct((b, n, d), BF16),
                   jax.ShapeDtypeStruct((b, n_e, n), F32)),
        grid=(b, n // TILE),
        in_specs=[tile, tile, tile, tile, tile, dir0, dir1, tile,
                  pl.BlockSpec((None, None, 6, d), lambda i, j: (i, 1, 0, 0)),
                  pl.BlockSpec((d, d), lambda i, j: (0, 0)),
                  vec, vec, vec,
                  pl.BlockSpec((d, LANES), lambda i, j: (0, 0)),
                  pl.BlockSpec((LANES, d), lambda i, j: (0, 0)),
                  vec,
                  pl.BlockSpec((n_e, d), lambda i, j: (0, 0))],
        out_specs=(out_tile, out_tile, pl.BlockSpec((None, n_e, TILE), lambda i, j: (i, 0, j))),
        compiler_params=_cparams(("arbitrary", "arbitrary")),
        name="rwkv_out",
    )(y_fwd, y_bwd, r, v, g, kd, kd, xs, modsel, p["w_o"], p["ln_w"], p["ln_b"], p["r_k"], p["seg_r"],
      p["seg_b"], g2, router_t)


def _moe_block(x1, h2, aff_t, token_sets, modsel, g, w_gate, w_up, w_down, layer, *, tile_offset,
               final_norm):
    sets, n_slots = [], 0
    for start, length in token_sets:
        cap = EC_CAPACITY * length // N_EXPERTS
        sets.append((start, length, n_slots, cap))
        n_slots += cap
    sets = tuple(sets)
    slots, bounds = _route(aff_t, sets)
    bounds = bounds[:, :, :aff_t.shape[2] // TILE].astype(jnp.int32)
    xs = _moe_gather(slots, bounds, h2, sets, n_slots)
    outs = _moe_ffn(slots, aff_t, xs, w_gate, w_up, w_down, layer)
    return _combine(slots, bounds, outs, x1, modsel, g, sets, tile_offset=tile_offset,
                    final_norm=final_norm)


def _rope_tables(n_ctx, n):
    rows = n // GRID_W
    row = jnp.broadcast_to(jnp.arange(rows)[:, None], (rows, GRID_W)).reshape(-1).astype(F32)
    col = jnp.broadcast_to(jnp.arange(GRID_W)[None, :], (rows, GRID_W)).reshape(-1).astype(F32)
    pairs = HEAD_DIM // 4
    inv = ROPE_THETA ** (-jnp.arange(pairs, dtype=F32) / pairs)
    ang = jnp.concatenate([row[:, None] * inv, col[:, None] * inv], axis=-1)
    cos, sin = jnp.cos(ang), jnp.sin(ang)
    cos2 = jnp.concatenate([cos, cos], axis=-1)
    sin2 = jnp.concatenate([-sin, sin], axis=-1)
    cos2 = jnp.concatenate([jnp.ones((n_ctx, HEAD_DIM), F32), cos2], axis=0)
    sin2 = jnp.concatenate([jnp.zeros((n_ctx, HEAD_DIM), F32), sin2], axis=0)
    return cos2, sin2


def _block_diag2(a, b):
    z = jnp.zeros_like(a)
    return jnp.concatenate([jnp.concatenate([a, z], axis=1), jnp.concatenate([z, b], axis=1)], axis=0)


def kernel(x, c, ctx, c_ctx, ada_w, ada_b, norm_g, final_g, mix_w_in, mix_w_out, q_norm_g, k_norm_g, conv_w, rw_mu, rw_w_r, rw_w_k, rw_w_v, rw_w_o, rw_w0, rw_w1, rw_w2, rw_a0, rw_a1, rw_a2, rw_g1, rw_g2, rw_k_k, rw_k_a, rw_r_k, rw_ln_w, rw_ln_b, moe_router, moe_w_gate, moe_w_up, moe_w_down):
    b, n, d = x.shape
    n_ctx = ctx.shape[1]
    depth = ada_w.shape[0]
    assert depth == 2 and n_ctx == TILE and n % TILE == 0 and d % LANES == 0

    rows = -(-(b + 1) // HALO) * HALO
    cc = jnp.concatenate([c, c_ctx[None, :], jnp.zeros((rows - b - 1, d), F32)], axis=0)
    mod = _ada_mod(cc, ada_w, ada_b)
    mod_lat = mod[:, :b].reshape(depth, b, 1, 6, d)
    mod_ctx = jnp.broadcast_to(mod[:, b].reshape(depth, 1, 1, 6, d), (depth, b, 1, 6, d))
    modsel = jnp.concatenate([mod_ctx, mod_lat], axis=2)

    router_t = jnp.swapaxes(moe_router, 1, 2).astype(BF16)

    cos2, sin2 = _rope_tables(n_ctx, n)
    qkv, conv = _in_proj(ctx, x, modsel[0], norm_g[0, 0][None], mix_w_in[0].astype(BF16),
                         q_norm_g[0][None], k_norm_g[0][None], cos2, sin2)
    o_attn = _attention(qkv, n_ctx)
    x1, h2, aff_t = _mix_out(o_attn, conv, ctx, x, modsel[0], mix_w_out[0].astype(BF16), conv_w[0],
                             norm_g[0, 1][None], router_t[0])
    xs = _moe_block(x1, h2, aff_t, [(n_ctx, n), (0, n_ctx)], modsel[0], final_g[None],
                    moe_w_gate, moe_w_up, moe_w_down, 0, tile_offset=0, final_norm=False)

    heads = d // RW_N
    assert heads <= LANES
    seg_r = (jnp.arange(d)[:, None] // RW_N == jnp.arange(LANES)[None, :]).astype(BF16)
    p = {
        "mu": rw_mu[0], "w_r": rw_w_r[0].astype(BF16), "w_k": rw_w_k[0].astype(BF16),
        "w_v": rw_w_v[0].astype(BF16), "w_o": rw_w_o[0].astype(BF16),
        "w1": jnp.concatenate([rw_w1[0, 0], rw_w1[0, 1]], axis=1).astype(BF16),
        "a1": jnp.concatenate([rw_a1[0, 0], rw_a1[0, 1]], axis=1).astype(BF16),
        "g1": rw_g1[0].astype(BF16),
        "w2": _block_diag2(rw_w2[0, 0], rw_w2[0, 1]).astype(BF16),
        "a2": _block_diag2(rw_a2[0, 0], rw_a2[0, 1]).astype(BF16),
        "g2": rw_g2[0].astype(BF16),
        "w0": rw_w0[0], "a0": rw_a0[0], "k_k": rw_k_k[0][None], "k_a": rw_k_a[0][None],
        "r_k": rw_r_k[0].reshape(1, d), "ln_w": rw_ln_w[0][None], "ln_b": rw_ln_b[0][None],
        "seg_r": seg_r, "seg_b": seg_r.T,
    }
    r, v, kk, g, lw, kd, bb = _rwkv_feat(xs, modsel[1], norm_g[1, 0][None], p)
    y_fwd = _rwkv_scan(lw, kd, bb, kk, r, v, n_ctx, rev=False)
    y_bwd = _rwkv_scan(lw, kd, bb, kk, r, v, n_ctx, rev=True)
    x1, h2, aff_t = _rwkv_out(y_fwd, y_bwd, r, v, g, kd, xs, modsel[1], p, norm_g[1, 1][None],
                              router_t[1], n_ctx)
    return _moe_block(x1, h2, aff_t, [(0, n)], modsel[1], final_g[None],
                      moe_w_gate, moe_w_up, moe_w_down, 1, tile_offset=1, final_norm=True)
```

```python
import functools

import jax
import jax.numpy as jnp
from jax import lax
from jax.experimental import pallas as pl
from jax.experimental.pallas import tpu as pltpu

F32 = jnp.float32
BF16 = jnp.bfloat16

NORM_EPS = 1e-6
GN_EPS = 64e-5
GRID_W = 64
ROPE_THETA = 10000.0
HEAD_DIM = 128
N_Q_HEADS = 8
N_KV_HEADS = 2
Q_PER_KV = N_Q_HEADS // N_KV_HEADS
RW_N = 64
N_EXPERTS = 16
EC_CAPACITY = 2
LOG2_E = 1.4426950408889634
DECAY_SCALE = 0.6065306597126334

LANES = 128
TILE = 256
HALO = 8
CHUNK = 64
SCAN_BLOCK = 256
GATHER_WIN = 64
MOE_FFN_SAMPLES = 4
SCATTER_WIN = 64
ROW_SAMPLES = 2
VMEM_LIMIT = 56 * 1024 * 1024


def _cparams(sem):
    return pltpu.CompilerParams(dimension_semantics=sem, vmem_limit_bytes=VMEM_LIMIT)


def _dot(a, b):
    return jnp.dot(a.astype(BF16), b.astype(BF16), preferred_element_type=F32)


def _dot_nt(a, b):
    return lax.dot_general(a.astype(BF16), b.astype(BF16), (((1,), (1,)), ((), ())),
                           preferred_element_type=F32)


def _dot_tn(a, b):
    return lax.dot_general(a.astype(BF16), b.astype(BF16), (((0,), (0,)), ((), ())),
                           preferred_element_type=F32)


def _split_dot(x, w, parts):
    acc = None
    rem = x
    for _ in range(parts):
        hi = rem.astype(BF16)
        term = jnp.dot(hi, w, preferred_element_type=F32)
        acc = term if acc is None else acc + term
        rem = rem - hi.astype(F32)
    return acc


def _segsum(x, seg_r, seg_b):
    return _split_dot(_split_dot(x, seg_r, 2), seg_b, 3)


def _rms_rows(x, g):
    ms = jnp.mean(x * x, axis=-1, keepdims=True)
    return x * lax.rsqrt(ms + NORM_EPS) * g


def _ada_rows(x, g, shift, scale):
    return _rms_rows(x, g) * (1.0 + scale) + shift


def _sigmoid(x):
    return 1.0 / (1.0 + jnp.exp(-x))


def _shift_rows(h, prev_row, next_row):
    n = h.shape[0]
    rows = lax.broadcasted_iota(jnp.int32, h.shape, 0)
    down = jnp.where(rows == 0, prev_row, pltpu.roll(h, 1, 0))
    up = jnp.where(rows == n - 1, next_row, pltpu.roll(h, n - 1, 0))
    return down, up


def _ada_mod_kernel(c_ref, w_ref, b_ref, o_ref):
    c = c_ref[...]
    o_ref[...] = _dot(c * _sigmoid(c), w_ref[...]) + b_ref[...]


def _ada_mod(cc, ada_w, ada_b):
    depth, d, n6 = ada_w.shape
    rows = cc.shape[0]
    nb = 1536
    return pl.pallas_call(
        _ada_mod_kernel,
        out_shape=jax.ShapeDtypeStruct((depth, rows, n6), F32),
        grid=(depth, n6 // nb),
        in_specs=[
            pl.BlockSpec((rows, d), lambda l, j: (0, 0)),
            pl.BlockSpec((None, d, nb), lambda l, j: (l, 0, j)),
            pl.BlockSpec((None, 1, nb), lambda l, j: (l, 0, j)),
        ],
        out_specs=pl.BlockSpec((None, rows, nb), lambda l, j: (l, 0, j)),
        compiler_params=_cparams(("arbitrary", "arbitrary")),
        name="ada_mod",
    )(cc, ada_w, ada_b.reshape(depth, 1, n6))


def _mod_spec(d):
    return pl.BlockSpec((None, None, 6, d), lambda b, t: (b, jnp.minimum(t, 1), 0, 0))


def _stream_tile(ctx_ref, x_ref):
    return jnp.where(pl.program_id(1) == 0, ctx_ref[...], x_ref[...])


def _stream_specs(d):
    return [pl.BlockSpec((None, TILE, d), lambda i, j: (i, 0, 0)),
            pl.BlockSpec((None, TILE, d), lambda i, j: (i, jnp.maximum(j - 1, 0), 0))]


def _in_proj_kernel(ctx_ref, x_ref, mod_ref, g_ref, w_ref, qg_ref, kg_ref, cos_ref, sin_ref,
                    qkv_ref, conv_ref, *, n_qkv, n_rot_heads, n_q_heads):
    ns = x_ref.shape[0]
    rows = [slice(s * TILE, (s + 1) * TILE) for s in range(ns)]
    h = jnp.concatenate(
        [_ada_rows(_stream_tile(ctx_ref.at[s], x_ref.at[s]), g_ref[...], mod_ref[s, 0:1, :], mod_ref[s, 1:2, :])
         for s in range(ns)], axis=0).astype(BF16)
    nb = 512
    n_all = w_ref.shape[1]
    cos2 = jnp.concatenate([cos_ref[...]] * ns, axis=0)
    sin2 = jnp.concatenate([sin_ref[...]] * ns, axis=0)
    for j in range(n_all // nb):
        p = jnp.dot(h, w_ref[:, j * nb:(j + 1) * nb], preferred_element_type=F32)
        if j * nb >= n_qkv:
            for s in range(ns):
                conv_ref[s, :, j * nb - n_qkv:(j + 1) * nb - n_qkv] = p[rows[s]]
            continue
        for i in range(nb // HEAD_DIM):
            hd = j * (nb // HEAD_DIM) + i
            t = p[:, i * HEAD_DIM:(i + 1) * HEAD_DIM]
            if hd < n_rot_heads:
                t = _rms_rows(t, qg_ref[...] if hd < n_q_heads else kg_ref[...])
                t = t * cos2 + pltpu.roll(t, HEAD_DIM // 2, 1) * sin2
            if hd < n_q_heads:
                t = t * ((HEAD_DIM ** -0.5) * LOG2_E)
            t = t.astype(BF16)
            for s in range(ns):
                qkv_ref[s, :, hd * HEAD_DIM:(hd + 1) * HEAD_DIM] = t[rows[s]]


def _in_proj(ctx, x, modsel, g, w_in, q_g, k_g, cos2, sin2):
    b, n, d = x.shape
    t = ctx.shape[1] + n
    n_all = w_in.shape[1]
    n_qkv = (N_Q_HEADS + 2 * N_KV_HEADS) * HEAD_DIM
    ns = ROW_SAMPLES if b % ROW_SAMPLES == 0 else 1
    kern = functools.partial(_in_proj_kernel, n_qkv=n_qkv, n_rot_heads=N_Q_HEADS + N_KV_HEADS,
                             n_q_heads=N_Q_HEADS)
    return pl.pallas_call(
        kern,
        out_shape=(jax.ShapeDtypeStruct((b, t, n_qkv), BF16),
                   jax.ShapeDtypeStruct((b, t, n_all - n_qkv), F32)),
        grid=(b // ns, t // TILE),
        in_specs=[
            pl.BlockSpec((ns, TILE, d), lambda i, j: (i, 0, 0)),
            pl.BlockSpec((ns, TILE, d), lambda i, j: (i, jnp.maximum(j - 1, 0), 0)),
            pl.BlockSpec((ns, None, 6, d), lambda i, j: (i, jnp.minimum(j, 1), 0, 0)),
            pl.BlockSpec((1, d), lambda i, j: (0, 0)),
            pl.BlockSpec((d, n_all), lambda i, j: (0, 0)),
            pl.BlockSpec((1, HEAD_DIM), lambda i, j: (0, 0)),
            pl.BlockSpec((1, HEAD_DIM), lambda i, j: (0, 0)),
            pl.BlockSpec((TILE, HEAD_DIM), lambda i, j: (j, 0)),
            pl.BlockSpec((TILE, HEAD_DIM), lambda i, j: (j, 0)),
        ],
        out_specs=(pl.BlockSpec((ns, TILE, n_qkv), lambda i, j: (i, j, 0)),
                   pl.BlockSpec((ns, TILE, n_all - n_qkv), lambda i, j: (i, j, 0))),
        compiler_params=_cparams(("arbitrary", "arbitrary")),
        name="in_proj",
    )(ctx, x, modsel, g, w_in, q_g, k_g, cos2, sin2)


def _attn_kernel(q_ref, k_ref, v_ref, o_ref, *, n_ctx):
    heads = [slice(g * HEAD_DIM, (g + 1) * HEAD_DIM) for g in range(Q_PER_KV)]

    def attend(nk):
        k = k_ref[0:nk, :]
        v = v_ref[0:nk, :]
        s = _dot_nt(q_ref[:, heads[0]], k)
        for g in range(Q_PER_KV):
            s_next = _dot_nt(q_ref[:, heads[g + 1]], k) if g + 1 < Q_PER_KV else None
            p = jnp.exp2(s - jnp.max(s, axis=-1, keepdims=True))
            l = jnp.sum(p, axis=-1, keepdims=True)
            o_ref[:, heads[g]] = (jnp.dot(p.astype(BF16), v, preferred_element_type=F32) / l).astype(BF16)
            s = s_next

    is_ctx = pl.program_id(2) == 0

    @pl.when(is_ctx)
    def _():
        attend(n_ctx)

    @pl.when(jnp.logical_not(is_ctx))
    def _():
        attend(k_ref.shape[0])


def _attention(qkv, n_ctx):
    b, t, _ = qkv.shape
    kern = functools.partial(_attn_kernel, n_ctx=n_ctx)
    return pl.pallas_call(
        kern,
        out_shape=jax.ShapeDtypeStruct((b, t, N_Q_HEADS * HEAD_DIM), BF16),
        grid=(b, N_KV_HEADS, t // TILE),
        in_specs=[
            pl.BlockSpec((None, TILE, Q_PER_KV * HEAD_DIM), lambda i, kv, j: (i, j, kv)),
            pl.BlockSpec((None, t, HEAD_DIM), lambda i, kv, j: (i, 0, N_Q_HEADS + kv)),
            pl.BlockSpec((None, t, HEAD_DIM), lambda i, kv, j: (i, 0, N_Q_HEADS + N_KV_HEADS + kv)),
        ],
        out_specs=pl.BlockSpec((None, TILE, Q_PER_KV * HEAD_DIM), lambda i, kv, j: (i, j, kv)),
        compiler_params=_cparams(("arbitrary",) * 3),
        name="attention",
    )(qkv, qkv, qkv)


def _residual_and_router(x, y, mod_ref, g2_ref, router_ref, x1_ref, h2_ref, aff_ref):
    x1 = x + mod_ref[2:3, :] * y
    x1_ref[...] = x1
    h2 = _ada_rows(x1, g2_ref[...], mod_ref[3:4, :], mod_ref[4:5, :]).astype(BF16)
    h2_ref[...] = h2
    logits = _dot_nt(router_ref[...], h2)
    e = jnp.exp(logits - jnp.max(logits, axis=0, keepdims=True))
    aff_ref[...] = e / jnp.sum(e, axis=0, keepdims=True)


def _mix_out_kernel(oa_ref, gb_ref, gc_ref, xc_ref, gcp_ref, xcp_ref, gcn_ref, xcn_ref,
                    ctx_ref, x_ref, mod_ref, wout_ref, cw_ref, g2_ref, router_ref,
                    x1_ref, h2_ref, aff_ref, *, n_tiles):
    t = pl.program_id(1)
    has_prev = (t >= 2).astype(F32)
    has_next = jnp.logical_and(t >= 1, t <= n_tiles - 2).astype(F32)
    u = gc_ref[...] * xc_ref[...]
    u_prev = gcp_ref[HALO - 1:HALO, :] * xcp_ref[HALO - 1:HALO, :] * has_prev
    u_next = gcn_ref[0:1, :] * xcn_ref[0:1, :] * has_next
    down, up = _shift_rows(u, u_prev, u_next)
    o_conv = gb_ref[...] * (down * cw_ref[0:1, :] + u * cw_ref[1:2, :] + up * cw_ref[2:3, :])
    d_attn = oa_ref.shape[1]
    y = _dot(oa_ref[...], wout_ref[0:d_attn, :]) + _dot(o_conv, wout_ref[d_attn:, :])
    _residual_and_router(_stream_tile(ctx_ref, x_ref), y, mod_ref, g2_ref, router_ref, x1_ref, h2_ref,
                         aff_ref)


def _mix_out(o_attn, conv, ctx, x, modsel, w_out, conv_w, g2, router_t):
    b, t, d = o_attn.shape
    n_tiles = t // TILE
    hb = TILE // HALO
    last = t // HALO - 1
    prev_map = lambda col: (lambda i, j: (i, jnp.maximum(j * hb - 1, 0), col))
    next_map = lambda col: (lambda i, j: (i, jnp.minimum((j + 1) * hb, last), col))
    tile_map = lambda col: (lambda i, j: (i, j, col))
    n_e = router_t.shape[0]
    kern = functools.partial(_mix_out_kernel, n_tiles=n_tiles)
    return pl.pallas_call(
        kern,
        out_shape=(jax.ShapeDtypeStruct((b, t, d), F32),
                   jax.ShapeDtypeStruct((b, t, d), BF16),
                   jax.ShapeDtypeStruct((b, n_e, t), F32)),
        grid=(b, n_tiles),
        in_specs=[
            pl.BlockSpec((None, TILE, d), tile_map(0)),
            pl.BlockSpec((None, TILE, d), tile_map(0)),
            pl.BlockSpec((None, TILE, d), tile_map(1)),
            pl.BlockSpec((None, TILE, d), tile_map(2)),
            pl.BlockSpec((None, HALO, d), prev_map(1)),
            pl.BlockSpec((None, HALO, d), prev_map(2)),
            pl.BlockSpec((None, HALO, d), next_map(1)),
            pl.BlockSpec((None, HALO, d), next_map(2)),
        ] + _stream_specs(d) + [
            _mod_spec(d),
            pl.BlockSpec(w_out.shape, lambda i, j: (0, 0)),
            pl.BlockSpec(conv_w.shape, lambda i, j: (0, 0)),
            pl.BlockSpec((1, d), lambda i, j: (0, 0)),
            pl.BlockSpec((n_e, d), lambda i, j: (0, 0)),
        ],
        out_specs=(pl.BlockSpec((None, TILE, d), tile_map(0)),
                   pl.BlockSpec((None, TILE, d), tile_map(0)),
                   pl.BlockSpec((None, n_e, TILE), lambda i, j: (i, 0, j))),
        compiler_params=_cparams(("arbitrary", "arbitrary")),
        name="mix_out",
    )(o_attn, conv, conv, conv, conv, conv, conv, conv, ctx, x, modsel, w_out, conv_w, g2, router_t)


def _slot_windows(bnd_ref, sample, tile, n_e, slot0, cap, win, group):
    starts, fits = [], []
    for e in range(n_e):
        lo = bnd_ref[sample, e, tile]
        hi = bnd_ref[sample, n_e + e, tile]
        w0 = jnp.minimum((lo >> 4) << 4, slot0 + cap - win)
        ok = hi <= w0 + win
        if e % group == 0:
            fits.append(ok)
        else:
            fits[-1] = jnp.logical_and(fits[-1], ok)
        starts.append(pl.multiple_of(w0, 16))
    return starts, fits


def _moe_gather_kernel(bnd_ref, slot_ref, h_ref, x_ref, *, sets):
    n_e = x_ref.shape[0]
    n_tok = slot_ref.shape[1]
    sample = pl.program_id(0)
    tile = pl.program_id(1)

    @pl.when(tile == 0)
    def _():
        x_ref[...] = jnp.zeros_like(x_ref)

    def onehot(e, first, n):
        rows = lax.broadcasted_iota(jnp.int32, (n, n_tok), 0) + first
        return jnp.where(slot_ref[e:e + 1, :] == rows.astype(F32), 1.0, 0.0).astype(BF16)

    for start, length, slot0, cap in sets:
        first_tile, last_tile = start // TILE, (start + length) // TILE
        win = min(GATHER_WIN, cap)

        @pl.when(jnp.logical_and(tile >= first_tile, tile < last_tile))
        def _():
            starts, (fits,) = _slot_windows(bnd_ref, sample, tile, n_e, slot0, cap, win, group=n_e)

            @pl.when(fits)
            def _():
                oh = jnp.concatenate([onehot(e, starts[e], win) for e in range(n_e)], axis=0)
                part = jnp.dot(oh, h_ref[...], preferred_element_type=F32).astype(BF16)
                for e in range(n_e):
                    x_ref[e, pl.ds(starts[e], win), :] += part[e * win:(e + 1) * win]

            @pl.when(jnp.logical_not(fits))
            def _():
                for e in range(n_e):
                    x_ref[e, slot0:slot0 + cap, :] += jnp.dot(
                        onehot(e, slot0, cap), h_ref[...], preferred_element_type=F32).astype(BF16)


def _moe_gather(slots, bounds, h2, sets, n_slots):
    b, n_e, t = slots.shape
    d = h2.shape[2]
    grid_spec = pltpu.PrefetchScalarGridSpec(
        num_scalar_prefetch=1,
        grid=(b, t // TILE),
        in_specs=[
            pl.BlockSpec((None, n_e, TILE), lambda i, j, bnd: (i, 0, j)),
            pl.BlockSpec((None, TILE, d), lambda i, j, bnd: (i, j, 0)),
        ],
        out_specs=pl.BlockSpec((None, n_e, n_slots, d), lambda i, j, bnd: (i, 0, 0, 0)),
    )
    return pl.pallas_call(
        functools.partial(_moe_gather_kernel, sets=sets),
        out_shape=jax.ShapeDtypeStruct((b, n_e, n_slots, d), BF16),
        grid_spec=grid_spec,
        compiler_params=_cparams(("arbitrary", "arbitrary")),
        name="moe_gather",
    )(bounds, slots, h2)


def _moe_ffn_kernel(slot_ref, aff_ref, x_ref, wg_ref, wu_ref, wd_ref, o_ref, wg_s, wu_s, wd_s):
    @pl.when(pl.program_id(1) == 0)
    def _():
        wg_s[...] = wg_ref[...].astype(BF16)
        wu_s[...] = wu_ref[...].astype(BF16)
        wd_s[...] = wd_ref[...].astype(BF16)

    nb, r, d = x_ref.shape
    rows = lax.broadcasted_iota(jnp.int32, (r, slot_ref.shape[2]), 0).astype(F32)
    a_row = jnp.concatenate(
        [jnp.sum(jnp.where(slot_ref[i] == rows, aff_ref[i], 0.0), axis=-1, keepdims=True) for i in range(nb)],
        axis=0)
    x = x_ref[...].reshape(nb * r, d)
    gate = jnp.dot(x, wg_s[...], preferred_element_type=F32)
    up = jnp.dot(x, wu_s[...], preferred_element_type=F32)
    act = (gate * _sigmoid(gate) * up).astype(BF16)
    out = jnp.dot(act, wd_s[...], preferred_element_type=F32) * a_row
    o_ref[...] = out.astype(BF16).reshape(nb, r, d)


def _moe_ffn(slots, aff_t, xs, w_gate, w_up, w_down, layer):
    b, n_e, t = slots.shape
    _, _, r, d = xs.shape
    f = w_gate.shape[3]
    nb = MOE_FFN_SAMPLES if b % MOE_FFN_SAMPLES == 0 else 1
    row = pl.BlockSpec((nb, None, 1, t), lambda e, i: (i, e, 0, 0))
    tok = pl.BlockSpec((nb, None, r, d), lambda e, i: (i, e, 0, 0))
    return pl.pallas_call(
        _moe_ffn_kernel,
        out_shape=jax.ShapeDtypeStruct((b, n_e, r, d), BF16),
        grid=(n_e, b // nb),
        in_specs=[
            row, row, tok,
            pl.BlockSpec((None, None, d, f), lambda e, i: (layer, e, 0, 0)),
            pl.BlockSpec((None, None, d, f), lambda e, i: (layer, e, 0, 0)),
            pl.BlockSpec((None, None, f, d), lambda e, i: (layer, e, 0, 0)),
        ],
        out_specs=tok,
        scratch_shapes=[pltpu.VMEM((d, f), BF16), pltpu.VMEM((d, f), BF16), pltpu.VMEM((f, d), BF16)],
        compiler_params=_cparams(("arbitrary", "arbitrary")),
        name="moe_ffn",
    )(slots.reshape(b, n_e, 1, t), aff_t.reshape(b, n_e, 1, t), xs, w_gate, w_up, w_down)


def _prefix_count(mask, tri):
    return _prefix_count_blocks(mask, tri)[0]


def _prefix_count_blocks(mask, tri):
    blk = tri.shape[0]
    carry = jnp.zeros((mask.shape[0], 1), F32)
    parts, carries = [], [carry]
    for j in range(mask.shape[1] // blk):
        m = mask[:, j * blk:(j + 1) * blk]
        parts.append(jnp.dot(m.astype(BF16), tri, preferred_element_type=F32) + carry)
        carry = carry + jnp.sum(m, axis=-1, keepdims=True)
        carries.append(carry)
    return jnp.concatenate(parts, axis=-1), carries


def _route_kernel(aff_ref, slot_ref, bound_ref, *, sets):
    blk = TILE
    n_e = aff_ref.shape[0]
    tri = jnp.where(lax.broadcasted_iota(jnp.int32, (blk, blk), 0)
                    <= lax.broadcasted_iota(jnp.int32, (blk, blk), 1), 1.0, 0.0).astype(BF16)
    lane = lax.broadcasted_iota(jnp.int32, (n_e, LANES), 1)
    lo = jnp.zeros((n_e, LANES), F32)
    hi = jnp.zeros((n_e, LANES), F32)
    for start, length, slot0, cap in sets:
        cap = float(cap)
        x = aff_ref[:, start:start + length]
        bits = pltpu.bitcast(x, jnp.int32)
        thr = jnp.zeros((bits.shape[0], 1), jnp.int32)
        for bit in range(30, -1, -1):
            cand = thr | (1 << bit)
            cnt = jnp.sum(jnp.where(bits >= cand, 1.0, 0.0), axis=-1, keepdims=True)
            thr = jnp.where(cnt >= cap, cand, thr)
        pivot = jnp.max(jnp.where(bits <= thr, x, -1.0), axis=-1, keepdims=True)

        def counts(p):
            return (jnp.sum(jnp.where(x > p, 1.0, 0.0), axis=-1, keepdims=True),
                    jnp.sum(jnp.where(x >= p, 1.0, 0.0), axis=-1, keepdims=True))

        def misplaced(p):
            n_gt, n_ge = counts(p)
            return jnp.max(jnp.where(jnp.logical_or(n_gt >= cap, n_ge < cap), 1.0, 0.0)) > 0.0

        def walk(p):
            n_gt, n_ge = counts(p)
            higher = jnp.min(jnp.where(x > p, x, jnp.inf), axis=-1, keepdims=True)
            lower = jnp.max(jnp.where(x < p, x, -1.0), axis=-1, keepdims=True)
            return jnp.where(n_gt >= cap, higher, jnp.where(n_ge < cap, lower, p))

        pivot = lax.while_loop(misplaced, walk, pivot)
        above = jnp.where(x > pivot, 1.0, 0.0)
        tied = jnp.where(x == pivot, 1.0, 0.0)
        need = cap - jnp.sum(above, axis=-1, keepdims=True)
        chosen = above + jnp.where(_prefix_count(tied, tri) <= need, tied, 0.0)
        count, before = _prefix_count_blocks(chosen, tri)
        slot_ref[:, start:start + length] = jnp.where(chosen > 0.0, count - 1.0 + slot0, -1.0)
        for j in range(length // blk):
            here = lane == (start // blk + j)
            lo = jnp.where(here, before[j] + slot0, lo)
            hi = jnp.where(here, before[j + 1] + slot0, hi)
    bound_ref[0:n_e, :] = lo
    bound_ref[n_e:, :] = hi


def _route(aff_t, sets):
    b, n_e, t = aff_t.shape
    assert t // TILE <= LANES
    kern = functools.partial(_route_kernel, sets=tuple(sets))
    spec = pl.BlockSpec((None, n_e, t), lambda i: (i, 0, 0))
    bspec = pl.BlockSpec((None, 2 * n_e, LANES), lambda i: (i, 0, 0))
    return pl.pallas_call(
        kern,
        out_shape=(jax.ShapeDtypeStruct((b, n_e, t), F32),
                   jax.ShapeDtypeStruct((b, 2 * n_e, LANES), F32)),
        grid=(b,),
        in_specs=[spec],
        out_specs=(spec, bspec),
        compiler_params=_cparams(("arbitrary",)),
        name="moe_route",
    )(aff_t)


def _combine_kernel(bnd_ref, slot_ref, o_ref, x_ref, mod_ref, g_ref, out_ref, *, sets, final_norm):
    n_e = o_ref.shape[0]
    n_tok = slot_ref.shape[1]
    sample = pl.program_id(0)
    tile = pl.program_id(1)
    tn = lambda a, w: lax.dot_general(a, w, (((0,), (0,)), ((), ())), preferred_element_type=F32)

    def onehot(e, first, n):
        rows = lax.broadcasted_iota(jnp.int32, (n, n_tok), 0) + first
        return jnp.where(slot_ref[e:e + 1, :] == rows.astype(F32), 1.0, 0.0).astype(BF16)

    def scatter_full(experts, slot0, cap):
        y = None
        for e in experts:
            term = tn(onehot(e, slot0, cap), o_ref[e, slot0:slot0 + cap, :])
            y = term if y is None else y + term
        return y

    for start, length, slot0, cap in sets:
        first_tile, last_tile = start // TILE, (start + length) // TILE

        def finish(y):
            x2 = x_ref[...] + mod_ref[5:6, :] * y
            out_ref[...] = _rms_rows(x2, g_ref[...]) if final_norm else x2

        @pl.when(jnp.logical_and(tile >= first_tile, tile < last_tile))
        def _():
            win = min(SCATTER_WIN, cap)
            starts, (fits,) = _slot_windows(bnd_ref, sample, tile, n_e, slot0, cap, win, group=n_e)

            @pl.when(fits)
            def _():
                oh = jnp.concatenate([onehot(e, starts[e], win) for e in range(n_e)], axis=0)
                rows = jnp.concatenate([o_ref[e, pl.ds(starts[e], win), :] for e in range(n_e)], axis=0)
                finish(tn(oh, rows))

            @pl.when(jnp.logical_not(fits))
            def _():
                finish(scatter_full(range(n_e), slot0, cap))


def _combine(slots, bounds, outs, x1, modsel, g, sets, *, tile_offset, final_norm):
    b, t, d = x1.shape
    _, n_e, r, _ = outs.shape
    assert all(s[2] % 16 == 0 and s[3] % 16 == 0 for s in sets)
    kern = functools.partial(_combine_kernel, sets=sets, final_norm=final_norm)
    grid_spec = pltpu.PrefetchScalarGridSpec(
        num_scalar_prefetch=1,
        grid=(b, t // TILE),
        in_specs=[
            pl.BlockSpec((None, n_e, TILE), lambda i, j, bnd: (i, 0, j)),
            pl.BlockSpec((None, n_e, r, d), lambda i, j, bnd: (i, 0, 0, 0)),
            pl.BlockSpec((None, TILE, d), lambda i, j, bnd: (i, j, 0)),
            pl.BlockSpec((None, None, 6, d), lambda i, j, bnd: (i, jnp.minimum(j + tile_offset, 1), 0, 0)),
            pl.BlockSpec((1, d), lambda i, j, bnd: (0, 0)),
        ],
        out_specs=pl.BlockSpec((None, TILE, d), lambda i, j, bnd: (i, j, 0)),
    )
    return pl.pallas_call(
        kern,
        out_shape=jax.ShapeDtypeStruct((b, t, d), F32),
        grid_spec=grid_spec,
        compiler_params=_cparams(("arbitrary", "arbitrary")),
        name="moe_combine",
    )(bounds, slots, outs, x1, modsel, g)


def _rwkv_feat_kernel(x_ref, xp_ref, xn_ref, mod_ref, g_ref, mu_ref, wr_ref, wk_ref, wv_ref,
                      w1_ref, a1_ref, g1_ref, w2_ref, a2_ref, g2_ref, w0_ref, a0_ref,
                      kk_ref, ka_ref, segr_ref, segb_ref,
                      r_out, v_out, kkn_out, g_out, lw_out, kd_out, bb_out, *, n_tiles):
    t = pl.program_id(1)
    has_prev = (t >= 2).astype(F32)
    has_next = jnp.logical_and(t >= 1, t <= n_tiles - 2).astype(F32)
    g = g_ref[...]
    shift = mod_ref[0:1, :]
    scale = mod_ref[1:2, :]
    h = _ada_rows(x_ref[...], g, shift, scale)
    h_prev = _ada_rows(xp_ref[HALO - 1:HALO, :], g, shift, scale) * has_prev
    h_next = _ada_rows(xn_ref[0:1, :], g, shift, scale) * has_next
    down, up = _shift_rows(h, h_prev, h_next)
    xx = 0.5 * (down + up) - h
    d = h.shape[1]

    r_out[...] = _dot(h + xx * mu_ref[0:1, :], wr_ref[...])
    k = _dot(h + xx * mu_ref[2:3, :], wk_ref[...])
    v_out[...] = _dot(h + xx * mu_ref[3:4, :], wv_ref[...])
    gate = _sigmoid(_dot(h + xx * mu_ref[5:6, :], g1_ref[...]))
    g_out[...] = _dot(gate, g2_ref[...])

    kkp = k * kk_ref[...]
    ss = _segsum(kkp * kkp, segr_ref[...], segb_ref[...])
    kkn = kkp * lax.rsqrt(jnp.maximum(ss, 1e-24))
    kkn_out[...] = kkn

    lora_w = _dot(jnp.tanh(_dot(h + xx * mu_ref[1:2, :], w1_ref[...])), w2_ref[...])
    lora_a = _dot(_dot(h + xx * mu_ref[4:5, :], a1_ref[...]), a2_ref[...])
    for di in range(2):
        sl = slice(di * d, (di + 1) * d)
        lw_out[di] = -DECAY_SCALE * _sigmoid(w0_ref[di:di + 1, :] + lora_w[:, sl])
        a = _sigmoid(a0_ref[di:di + 1, :] + lora_a[:, sl])
        kd_out[di] = k * (1.0 + (a - 1.0) * ka_ref[...])
        bb_out[di] = a * kkn


def _rwkv_feat(xs, modsel, g, p):
    b, t, d = xs.shape
    n_tiles = t // TILE
    hb = TILE // HALO
    last = t // HALO - 1
    full = lambda a: pl.BlockSpec(a.shape, lambda i, j: (0,) * a.ndim)
    weights = [p["mu"], p["w_r"], p["w_k"], p["w_v"], p["w1"], p["a1"], p["g1"], p["w2"], p["a2"],
               p["g2"], p["w0"], p["a0"], p["k_k"], p["k_a"], p["seg_r"], p["seg_b"]]
    tile_spec = pl.BlockSpec((None, TILE, d), lambda i, j: (i, j, 0))
    dir_spec = pl.BlockSpec((2, None, TILE, d), lambda i, j: (0, i, j, 0))
    kern = functools.partial(_rwkv_feat_kernel, n_tiles=n_tiles)
    one = jax.ShapeDtypeStruct((b, t, d), F32)
    two = jax.ShapeDtypeStruct((2, b, t, d), F32)
    return pl.pallas_call(
        kern,
        out_shape=(one, one, one, one, two, two, two),
        grid=(b, n_tiles),
        in_specs=[
            tile_spec,
            pl.BlockSpec((None, HALO, d), lambda i, j: (i, jnp.maximum(j * hb - 1, 0), 0)),
            pl.BlockSpec((None, HALO, d), lambda i, j: (i, jnp.minimum((j + 1) * hb, last), 0)),
            _mod_spec(d),
            pl.BlockSpec((1, d), lambda i, j: (0, 0)),
        ] + [full(a) for a in weights],
        out_specs=(tile_spec, tile_spec, tile_spec, tile_spec, dir_spec, dir_spec, dir_spec),
        compiler_params=_cparams(("arbitrary", "arbitrary")),
        name="rwkv_feat",
    )(xs, xs, xs, modsel, g, *weights)


def _scan_kernel(lw_ref, kd_ref, bb_ref, kk_ref, r_ref, v_ref, *rest, rev):
    add_ref = rest[0] if len(rest) == 3 else None
    y_ref, s_ref = rest[-2:]

    @pl.when(pl.program_id(1) == 0)
    def _():
        s_ref[...] = jnp.zeros_like(s_ref)

    L = CHUNK
    ti = lax.broadcasted_iota(jnp.int32, (L, LANES), 0)
    ii = lax.broadcasted_iota(jnp.int32, (L, LANES), 1) & (L - 1)
    before = (ii > ti) if rev else (ii < ti)
    before_eq = (ii >= ti) if rev else (ii <= ti)
    t2 = lax.broadcasted_iota(jnp.int32, (L, L), 0)
    i2 = lax.broadcasted_iota(jnp.int32, (L, L), 1)
    tri = jnp.where((i2 >= t2) if rev else (i2 <= t2), 1.0, 0.0).astype(BF16)
    r128 = lax.broadcasted_iota(jnp.int32, (LANES, LANES), 0)
    c128 = lax.broadcasted_iota(jnp.int32, (LANES, LANES), 1)
    head_eq = (r128 >> 6) == (c128 >> 6)
    eye = r128 == c128

    def same_block(log2s):
        return (ti >> log2s) == (ii >> log2s)

    def bdiag(y):
        yb = y.astype(BF16)
        return jnp.where(head_eq, jnp.concatenate([yb, yb], axis=0), jnp.zeros((), BF16))

    def hmul(x, y):
        return jnp.dot(x.astype(BF16), bdiag(y), preferred_element_type=F32)

    def hmul2(x, y, z):
        return jnp.dot(x.astype(BF16), jnp.concatenate([bdiag(y), bdiag(z)], axis=1),
                       preferred_element_type=F32)

    n_sub = lw_ref.shape[0] // L
    n_pair = lw_ref.shape[1] // LANES
    sub_order = list(range(n_sub))[::-1] if rev else list(range(n_sub))
    chains = [(slice(s * L, (s + 1) * L), slice(p * LANES, (p + 1) * LANES))
              for s in sub_order for p in range(n_pair)]

    def each(f, *cols):
        return [f(*args) for args in zip(*cols)]

    lam_sub = {s: _split_dot_left(tri, lw_ref[s * L:(s + 1) * L, :]) for s in sub_order}
    lw = [lw_ref[rs_, cs] for rs_, cs in chains]
    lam = [lam_sub[rs_.start // L][:, cs] for rs_, cs in chains]
    kk = [kk_ref[rs_, cs] for rs_, cs in chains]
    bb = [bb_ref[rs_, cs] for rs_, cs in chains]
    kd = [kd_ref[rs_, cs] for rs_, cs in chains]
    r = [r_ref[rs_, cs] for rs_, cs in chains]
    v = [v_ref[rs_, cs] for rs_, cs in chains]

    tot = each(lambda a: jnp.sum(a, axis=0, keepdims=True), lw)
    rho = each(lambda a: a[L // 2:L // 2 + 1, :], lam)
    ks = each(lambda a, b, c: a * jnp.exp(b - c), kk, lam, lw)
    rs = each(lambda a, b: a * jnp.exp(b), r, lam)
    e_neg = each(lambda a: jnp.exp(-a), rho)
    e_inv = each(lambda a, b: jnp.exp(a - b), rho, lam)
    e_end = each(lambda a, b: jnp.exp(a - b), tot, lam)

    lhs = each(lambda a, b, c: jnp.concatenate([a * c, b * c], axis=0).astype(BF16), ks, rs, e_neg)
    nt = lambda a, w: lax.dot_general(a, w, (((1,), (1,)), ((), ())), preferred_element_type=F32)
    g_bk = each(lambda a, b, c, d: nt(a, jnp.concatenate([bdiag(b * d), bdiag(c * d)], axis=0)),
                lhs, bb, kd, e_inv)
    a_b = each(lambda a: jnp.where(before, a[0:L, 0:LANES], 0.0), g_bk)
    a_k = each(lambda a: jnp.where(before, a[0:L, LANES:], 0.0), g_bk)
    m_b = each(lambda a: jnp.where(before_eq, a[L:, 0:LANES], 0.0), g_bk)
    m_k = each(lambda a: jnp.where(before_eq, a[L:, LANES:], 0.0), g_bk)
    akv = each(hmul, a_k, v)

    ident = jnp.where(ti == ii, 1.0, 0.0)
    blk8 = same_block(3)
    n1 = each(lambda a: jnp.where(blk8, -a, 0.0), a_b)
    n2 = each(hmul, n1, n1)
    n3 = each(hmul, n1, n2)
    n4 = each(hmul, n2, n2)
    x = each(lambda a, b, c: ident + a + b + c, n1, n2, n3)
    tinv = each(lambda a, b: a + hmul(a, b), x, n4)
    for log2s in (3, 4, 5):
        off_mask = jnp.logical_and(same_block(log2s + 1), jnp.logical_not(same_block(log2s)))
        off = each(lambda a: jnp.where(off_mask, a, 0.0), a_b)
        corr = each(hmul, off, tinv)
        tinv = each(lambda a, b: a - hmul(a, b), tinv, corr)

    qp = each(lambda a, b, c: -hmul2(a, b, c), tinv, ks, akv)
    q1 = each(lambda a: a[:, 0:LANES], qp)
    p0 = each(lambda a: a[:, LANES:], qp)
    mqp = each(hmul2, m_b, q1, p0)
    q2 = each(lambda a, b: a + b[:, 0:LANES], rs, mqp)
    y0 = each(lambda a, b, c: a[:, LANES:] + hmul(b, c), mqp, m_k, v)
    end = each(lambda a, b, c: jnp.concatenate([a * c, b * c], axis=0), bb, kd, e_end)
    zero = jnp.zeros((L, LANES), F32)
    mn = each(lambda a, b, c, d: _dot_tn(a, jnp.concatenate(
        [jnp.concatenate([b, c], axis=1), jnp.concatenate([zero, d], axis=1)], axis=0)),
        end, q1, p0, v)
    m_t = each(lambda a, c: jnp.where(head_eq, a[:, 0:LANES], 0.0) + jnp.where(eye, jnp.exp(c), 0.0),
               mn, tot)
    n0 = each(lambda a: jnp.where(head_eq, a[:, LANES:], 0.0), mn)
    qm = each(lambda a, b: jnp.concatenate([a, b], axis=0).astype(BF16), q2, m_t)

    state = [s_ref[p] for p in range(n_pair)]
    for idx, (rs_, cs) in enumerate(chains):
        p = idx % n_pair
        out = jnp.dot(qm[idx], state[p].astype(BF16), preferred_element_type=F32)
        y = out[0:L] + y0[idx]
        y_ref[rs_, cs] = y if add_ref is None else add_ref[rs_, cs] + y
        state[p] = out[L:] + n0[idx]
    for p in range(n_pair):
        s_ref[p] = state[p]


def _split_dot_left(tri, x):
    acc = None
    rem = x
    for _ in range(3):
        hi = rem.astype(BF16)
        term = jnp.dot(tri, hi, preferred_element_type=F32)
        acc = term if acc is None else acc + term
        rem = rem - hi.astype(F32)
    return acc


def _rwkv_scan(lw, kd, bb, kk, r, v, n_ctx, rev, add=None):
    _, b, t, d = lw.shape
    nb_ctx = n_ctx // SCAN_BLOCK
    nb = t // SCAN_BLOCK
    di = 1 if rev else 0

    def block(j):
        return jnp.where(j < nb_ctx, nb_ctx - 1 - j, nb + nb_ctx - 1 - j) if rev else j

    dir_spec = pl.BlockSpec((None, None, SCAN_BLOCK, d), lambda i, j: (di, i, block(j), 0))
    one_spec = pl.BlockSpec((None, SCAN_BLOCK, d), lambda i, j: (i, block(j), 0))
    extra = [] if add is None else [add]
    return pl.pallas_call(
        functools.partial(_scan_kernel, rev=rev),
        out_shape=jax.ShapeDtypeStruct((b, t, d), F32),
        grid=(b, nb),
        in_specs=[dir_spec, dir_spec, dir_spec, one_spec, one_spec, one_spec] + ([one_spec] if extra else []),
        out_specs=one_spec,
        scratch_shapes=[pltpu.VMEM((d // LANES, LANES, LANES), F32)],
        compiler_params=_cparams(("arbitrary", "arbitrary")),
        name="rwkv_scan_bwd" if rev else "rwkv_scan_fwd",
    )(lw, kd, bb, kk, r, v, *extra)


def _rwkv_out_kernel(y_ref, r_ref, v_ref, g_ref, kdf_ref, kdb_ref, x_ref, mod_ref,
                     wo_ref, lnw_ref, lnb_ref, rk_ref, segr_ref, segb_ref, g2_ref, router_ref,
                     x1_ref, h2_ref, aff_ref):
    seg_r = segr_ref[...]
    seg_b = segb_ref[...]
    inv_n = 1.0 / RW_N
    y = y_ref[...]
    mean = _segsum(y, seg_r, seg_b) * inv_n
    dev = y - mean
    var = _segsum(dev * dev, seg_r, seg_b) * inv_n
    yn = dev * lax.rsqrt(var + GN_EPS) * lnw_ref[...] + lnb_ref[...]
    r = r_ref[...]
    rk = rk_ref[...]
    bonus = _segsum(r * kdf_ref[...] * rk + r * kdb_ref[...] * rk, seg_r, seg_b) * v_ref[...]
    out = _dot((yn + bonus) * g_ref[...], wo_ref[...])
    _residual_and_router(x_ref[...], out, mod_ref, g2_ref, router_ref, x1_ref, h2_ref, aff_ref)


def _rwkv_out(y, r, v, g, kd, xs, modsel, p, g2, router_t, n_ctx):
    b, t, d = xs.shape
    off = n_ctx // TILE
    n = t - n_ctx
    n_e = router_t.shape[0]
    tile = pl.BlockSpec((None, TILE, d), lambda i, j: (i, j + off, 0))
    dir0 = pl.BlockSpec((None, None, TILE, d), lambda i, j: (0, i, j + off, 0))
    dir1 = pl.BlockSpec((None, None, TILE, d), lambda i, j: (1, i, j + off, 0))
    vec = pl.BlockSpec((1, d), lambda i, j: (0, 0))
    out_tile = pl.BlockSpec((None, TILE, d), lambda i, j: (i, j, 0))
    return pl.pallas_call(
        _rwkv_out_kernel,
        out_shape=(jax.ShapeDtypeStruct((b, n, d), F32),
                   jax.ShapeDtypeStruct((b, n, d), BF16),
                   jax.ShapeDtypeStruct((b, n_e, n), F32)),
        grid=(b, n // TILE),
        in_specs=[tile, tile, tile, tile, dir0, dir1, tile,
                  pl.BlockSpec((None, None, 6, d), lambda i, j: (i, 1, 0, 0)),
                  pl.BlockSpec((d, d), lambda i, j: (0, 0)),
                  vec, vec, vec,
                  pl.BlockSpec((d, LANES), lambda i, j: (0, 0)),
                  pl.BlockSpec((LANES, d), lambda i, j: (0, 0)),
                  vec,
                  pl.BlockSpec((n_e, d), lambda i, j: (0, 0))],
        out_specs=(out_tile, out_tile, pl.BlockSpec((None, n_e, TILE), lambda i, j: (i, 0, j))),
        compiler_params=_cparams(("arbitrary", "arbitrary")),
        name="rwkv_out",
    )(y, r, v, g, kd, kd, xs, modsel, p["w_o"], p["ln_w"], p["ln_b"], p["r_k"], p["seg_r"],
      p["seg_b"], g2, router_t)


def _moe_block(x1, h2, aff_t, token_sets, modsel, g, w_gate, w_up, w_down, layer, *, tile_offset,
               final_norm):
    sets, n_slots = [], 0
    for start, length in token_sets:
        cap = EC_CAPACITY * length // N_EXPERTS
        sets.append((start, length, n_slots, cap))
        n_slots += cap
    sets = tuple(sets)
    slots, bounds = _route(aff_t, sets)
    bounds = bounds[:, :, :aff_t.shape[2] // TILE].astype(jnp.int32)
    xs = _moe_gather(slots, bounds, h2, sets, n_slots)
    outs = _moe_ffn(slots, aff_t, xs, w_gate, w_up, w_down, layer)
    return _combine(slots, bounds, outs, x1, modsel, g, sets, tile_offset=tile_offset,
                    final_norm=final_norm)


def _rope_tables(n_ctx, n):
    rows = n // GRID_W
    row = jnp.broadcast_to(jnp.arange(rows)[:, None], (rows, GRID_W)).reshape(-1).astype(F32)
    col = jnp.broadcast_to(jnp.arange(GRID_W)[None, :], (rows, GRID_W)).reshape(-1).astype(F32)
    pairs = HEAD_DIM // 4
    inv = ROPE_THETA ** (-jnp.arange(pairs, dtype=F32) / pairs)
    ang = jnp.concatenate([row[:, None] * inv, col[:, None] * inv], axis=-1)
    cos, sin = jnp.cos(ang), jnp.sin(ang)
    cos2 = jnp.concatenate([cos, cos], axis=-1)
    sin2 = jnp.concatenate([-sin, sin], axis=-1)
    cos2 = jnp.concatenate([jnp.ones((n_ctx, HEAD_DIM), F32), cos2], axis=0)
    sin2 = jnp.concatenate([jnp.zeros((n_ctx, HEAD_DIM), F32), sin2], axis=0)
    return cos2, sin2


def _block_diag2(a, b):
    z = jnp.zeros_like(a)
    return jnp.concatenate([jnp.concatenate([a, z], axis=1), jnp.concatenate([z, b], axis=1)], axis=0)


def kernel(x, c, ctx, c_ctx, ada_w, ada_b, norm_g, final_g, mix_w_in, mix_w_out, q_norm_g, k_norm_g, conv_w, rw_mu, rw_w_r, rw_w_k, rw_w_v, rw_w_o, rw_w0, rw_w1, rw_w2, rw_a0, rw_a1, rw_a2, rw_g1, rw_g2, rw_k_k, rw_k_a, rw_r_k, rw_ln_w, rw_ln_b, moe_router, moe_w_gate, moe_w_up, moe_w_down):
    b, n, d = x.shape
    n_ctx = ctx.shape[1]
    depth = ada_w.shape[0]
    assert depth == 2 and n_ctx == TILE and n % TILE == 0 and d % LANES == 0

    rows = -(-(b + 1) // HALO) * HALO
    cc = jnp.concatenate([c, c_ctx[None, :], jnp.zeros((rows - b - 1, d), F32)], axis=0)
    mod = _ada_mod(cc, ada_w, ada_b)
    mod_lat = mod[:, :b].reshape(depth, b, 1, 6, d)
    mod_ctx = jnp.broadcast_to(mod[:, b].reshape(depth, 1, 1, 6, d), (depth, b, 1, 6, d))
    modsel = jnp.concatenate([mod_ctx, mod_lat], axis=2)

    router_t = jnp.swapaxes(moe_router, 1, 2).astype(BF16)

    cos2, sin2 = _rope_tables(n_ctx, n)
    qkv, conv = _in_proj(ctx, x, modsel[0], norm_g[0, 0][None], mix_w_in[0].astype(BF16),
                         q_norm_g[0][None], k_norm_g[0][None], cos2, sin2)
    o_attn = _attention(qkv, n_ctx)
    x1, h2, aff_t = _mix_out(o_attn, conv, ctx, x, modsel[0], mix_w_out[0].astype(BF16), conv_w[0],
                             norm_g[0, 1][None], router_t[0])
    xs = _moe_block(x1, h2, aff_t, [(n_ctx, n), (0, n_ctx)], modsel[0], final_g[None],
                    moe_w_gate, moe_w_up, moe_w_down, 0, tile_offset=0, final_norm=False)

    heads = d // RW_N
    assert heads <= LANES
    seg_r = (jnp.arange(d)[:, None] // RW_N == jnp.arange(LANES)[None, :]).astype(BF16)
    p = {
        "mu": rw_mu[0], "w_r": rw_w_r[0].astype(BF16), "w_k": rw_w_k[0].astype(BF16),
        "w_v": rw_w_v[0].astype(BF16), "w_o": rw_w_o[0].astype(BF16),
        "w1": jnp.concatenate([rw_w1[0, 0], rw_w1[0, 1]], axis=1).astype(BF16),
        "a1": jnp.concatenate([rw_a1[0, 0], rw_a1[0, 1]], axis=1).astype(BF16),
        "g1": rw_g1[0].astype(BF16),
        "w2": _block_diag2(rw_w2[0, 0], rw_w2[0, 1]).astype(BF16),
        "a2": _block_diag2(rw_a2[0, 0], rw_a2[0, 1]).astype(BF16),
        "g2": rw_g2[0].astype(BF16),
        "w0": rw_w0[0], "a0": rw_a0[0], "k_k": rw_k_k[0][None], "k_a": rw_k_a[0][None],
        "r_k": rw_r_k[0].reshape(1, d), "ln_w": rw_ln_w[0][None], "ln_b": rw_ln_b[0][None],
        "seg_r": seg_r, "seg_b": seg_r.T,
    }
    r, v, kk, g, lw, kd, bb = _rwkv_feat(xs, modsel[1], norm_g[1, 0][None], p)
    y_fwd = _rwkv_scan(lw, kd, bb, kk, r, v, n_ctx, rev=False)
    y = _rwkv_scan(lw, kd, bb, kk, r, v, n_ctx, rev=True, add=y_fwd)
    x1, h2, aff_t = _rwkv_out(y, r, v, g, kd, xs, modsel[1], p, norm_g[1, 1][None], router_t[1], n_ctx)
    return _moe_block(x1, h2, aff_t, [(0, n)], modsel[1], final_g[None],
                      moe_w_gate, moe_w_up, moe_w_down, 1, tile_offset=1, final_norm=True)
```

```python
import functools

import jax
import jax.numpy as jnp
from jax import lax
from jax.experimental import pallas as pl
from jax.experimental.pallas import tpu as pltpu

F32 = jnp.float32
BF16 = jnp.bfloat16

NORM_EPS = 1e-6
GN_EPS = 64e-5
GRID_W = 64
ROPE_THETA = 10000.0
HEAD_DIM = 128
N_Q_HEADS = 8
N_KV_HEADS = 2
Q_PER_KV = N_Q_HEADS // N_KV_HEADS
RW_N = 64
N_EXPERTS = 16
EC_CAPACITY = 2
LOG2_E = 1.4426950408889634
DECAY_SCALE = 0.6065306597126334

LANES = 128
TILE = 256
HALO = 8
CHUNK = 64
SCAN_BLOCK = 256
GATHER_WIN = 64
MOE_FFN_SAMPLES = 4
SCATTER_WIN = 64
ROW_SAMPLES = 2
VMEM_LIMIT = 56 * 1024 * 1024


def _cparams(sem):
    return pltpu.CompilerParams(dimension_semantics=sem, vmem_limit_bytes=VMEM_LIMIT)


def _dot(a, b):
    return jnp.dot(a.astype(BF16), b.astype(BF16), preferred_element_type=F32)


def _dot_nt(a, b):
    return lax.dot_general(a.astype(BF16), b.astype(BF16), (((1,), (1,)), ((), ())),
                           preferred_element_type=F32)


def _dot_tn(a, b):
    return lax.dot_general(a.astype(BF16), b.astype(BF16), (((0,), (0,)), ((), ())),
                           preferred_element_type=F32)


def _split_dot(x, w, parts):
    acc = None
    rem = x
    for _ in range(parts):
        hi = rem.astype(BF16)
        term = jnp.dot(hi, w, preferred_element_type=F32)
        acc = term if acc is None else acc + term
        rem = rem - hi.astype(F32)
    return acc


def _segsum(x, seg_r, seg_b):
    return _split_dot(_split_dot(x, seg_r, 2), seg_b, 3)


def _rms_rows(x, g):
    ms = jnp.mean(x * x, axis=-1, keepdims=True)
    return x * lax.rsqrt(ms + NORM_EPS) * g


def _ada_rows(x, g, shift, scale):
    return _rms_rows(x, g) * (1.0 + scale) + shift


def _sigmoid(x):
    return 1.0 / (1.0 + jnp.exp(-x))


def _shift_rows(h, prev_row, next_row):
    n = h.shape[0]
    rows = lax.broadcasted_iota(jnp.int32, h.shape, 0)
    down = jnp.where(rows == 0, prev_row, pltpu.roll(h, 1, 0))
    up = jnp.where(rows == n - 1, next_row, pltpu.roll(h, n - 1, 0))
    return down, up


def _ada_mod_kernel(c_ref, w_ref, b_ref, o_ref):
    c = c_ref[...]
    o_ref[...] = _dot(c * _sigmoid(c), w_ref[...]) + b_ref[...]


def _ada_mod(cc, ada_w, ada_b):
    depth, d, n6 = ada_w.shape
    rows = cc.shape[0]
    nb = 1536
    return pl.pallas_call(
        _ada_mod_kernel,
        out_shape=jax.ShapeDtypeStruct((depth, rows, n6), F32),
        grid=(depth, n6 // nb),
        in_specs=[
            pl.BlockSpec((rows, d), lambda l, j: (0, 0)),
            pl.BlockSpec((None, d, nb), lambda l, j: (l, 0, j)),
            pl.BlockSpec((None, 1, nb), lambda l, j: (l, 0, j)),
        ],
        out_specs=pl.BlockSpec((None, rows, nb), lambda l, j: (l, 0, j)),
        compiler_params=_cparams(("arbitrary", "arbitrary")),
        name="ada_mod",
    )(cc, ada_w, ada_b.reshape(depth, 1, n6))


def _mod_spec(d):
    return pl.BlockSpec((None, None, 6, d), lambda b, t: (b, jnp.minimum(t, 1), 0, 0))


def _stream_tile(ctx_ref, x_ref):
    return jnp.where(pl.program_id(1) == 0, ctx_ref[...], x_ref[...])


def _stream_specs(d):
    return [pl.BlockSpec((None, TILE, d), lambda i, j: (i, 0, 0)),
            pl.BlockSpec((None, TILE, d), lambda i, j: (i, jnp.maximum(j - 1, 0), 0))]


def _in_proj_kernel(ctx_ref, x_ref, mod_ref, g_ref, w_ref, qg_ref, kg_ref, cos_ref, sin_ref,
                    qkv_ref, conv_ref, *, n_qkv, n_rot_heads, n_q_heads):
    ns = x_ref.shape[0]
    rows = [slice(s * TILE, (s + 1) * TILE) for s in range(ns)]
    h = jnp.concatenate(
        [_ada_rows(_stream_tile(ctx_ref.at[s], x_ref.at[s]), g_ref[...], mod_ref[s, 0:1, :], mod_ref[s, 1:2, :])
         for s in range(ns)], axis=0).astype(BF16)
    nb = 512
    n_all = w_ref.shape[1]
    cos2 = jnp.concatenate([cos_ref[...]] * ns, axis=0)
    sin2 = jnp.concatenate([sin_ref[...]] * ns, axis=0)
    for j in range(n_all // nb):
        p = jnp.dot(h, w_ref[:, j * nb:(j + 1) * nb], preferred_element_type=F32)
        if j * nb >= n_qkv:
            for s in range(ns):
                conv_ref[s, :, j * nb - n_qkv:(j + 1) * nb - n_qkv] = p[rows[s]]
            continue
        for i in range(nb // HEAD_DIM):
            hd = j * (nb // HEAD_DIM) + i
            t = p[:, i * HEAD_DIM:(i + 1) * HEAD_DIM]
            if hd < n_rot_heads:
                t = _rms_rows(t, qg_ref[...] if hd < n_q_heads else kg_ref[...])
                t = t * cos2 + pltpu.roll(t, HEAD_DIM // 2, 1) * sin2
            if hd < n_q_heads:
                t = t * ((HEAD_DIM ** -0.5) * LOG2_E)
            t = t.astype(BF16)
            for s in range(ns):
                qkv_ref[s, :, hd * HEAD_DIM:(hd + 1) * HEAD_DIM] = t[rows[s]]


def _in_proj(ctx, x, modsel, g, w_in, q_g, k_g, cos2, sin2):
    b, n, d = x.shape
    t = ctx.shape[1] + n
    n_all = w_in.shape[1]
    n_qkv = (N_Q_HEADS + 2 * N_KV_HEADS) * HEAD_DIM
    ns = ROW_SAMPLES if b % ROW_SAMPLES == 0 else 1
    kern = functools.partial(_in_proj_kernel, n_qkv=n_qkv, n_rot_heads=N_Q_HEADS + N_KV_HEADS,
                             n_q_heads=N_Q_HEADS)
    return pl.pallas_call(
        kern,
        out_shape=(jax.ShapeDtypeStruct((b, t, n_qkv), BF16),
                   jax.ShapeDtypeStruct((b, t, n_all - n_qkv), F32)),
        grid=(b // ns, t // TILE),
        in_specs=[
            pl.BlockSpec((ns, TILE, d), lambda i, j: (i, 0, 0)),
            pl.BlockSpec((ns, TILE, d), lambda i, j: (i, jnp.maximum(j - 1, 0), 0)),
            pl.BlockSpec((ns, None, 6, d), lambda i, j: (i, jnp.minimum(j, 1), 0, 0)),
            pl.BlockSpec((1, d), lambda i, j: (0, 0)),
            pl.BlockSpec((d, n_all), lambda i, j: (0, 0)),
            pl.BlockSpec((1, HEAD_DIM), lambda i, j: (0, 0)),
            pl.BlockSpec((1, HEAD_DIM), lambda i, j: (0, 0)),
            pl.BlockSpec((TILE, HEAD_DIM), lambda i, j: (j, 0)),
            pl.BlockSpec((TILE, HEAD_DIM), lambda i, j: (j, 0)),
        ],
        out_specs=(pl.BlockSpec((ns, TILE, n_qkv), lambda i, j: (i, j, 0)),
                   pl.BlockSpec((ns, TILE, n_all - n_qkv), lambda i, j: (i, j, 0))),
        compiler_params=_cparams(("arbitrary", "arbitrary")),
        name="in_proj",
    )(ctx, x, modsel, g, w_in, q_g, k_g, cos2, sin2)


def _attn_kernel(q_ref, k_ref, v_ref, o_ref, *, n_ctx):
    heads = [slice(g * HEAD_DIM, (g + 1) * HEAD_DIM) for g in range(Q_PER_KV)]

    def attend(nk):
        k = k_ref[0:nk, :]
        v = v_ref[0:nk, :]
        s = _dot_nt(q_ref[:, heads[0]], k)
        for g in range(Q_PER_KV):
            s_next = _dot_nt(q_ref[:, heads[g + 1]], k) if g + 1 < Q_PER_KV else None
            p = jnp.exp2(s - jnp.max(s, axis=-1, keepdims=True))
            l = jnp.sum(p, axis=-1, keepdims=True)
            o_ref[:, heads[g]] = (jnp.dot(p.astype(BF16), v, preferred_element_type=F32) / l).astype(BF16)
            s = s_next

    is_ctx = pl.program_id(2) == 0

    @pl.when(is_ctx)
    def _():
        attend(n_ctx)

    @pl.when(jnp.logical_not(is_ctx))
    def _():
        attend(k_ref.shape[0])


def _attention(qkv, n_ctx):
    b, t, _ = qkv.shape
    kern = functools.partial(_attn_kernel, n_ctx=n_ctx)
    return pl.pallas_call(
        kern,
        out_shape=jax.ShapeDtypeStruct((b, t, N_Q_HEADS * HEAD_DIM), BF16),
        grid=(b, N_KV_HEADS, t // TILE),
        in_specs=[
            pl.BlockSpec((None, TILE, Q_PER_KV * HEAD_DIM), lambda i, kv, j: (i, j, kv)),
            pl.BlockSpec((None, t, HEAD_DIM), lambda i, kv, j: (i, 0, N_Q_HEADS + kv)),
            pl.BlockSpec((None, t, HEAD_DIM), lambda i, kv, j: (i, 0, N_Q_HEADS + N_KV_HEADS + kv)),
        ],
        out_specs=pl.BlockSpec((None, TILE, Q_PER_KV * HEAD_DIM), lambda i, kv, j: (i, j, kv)),
        compiler_params=_cparams(("arbitrary",) * 3),
        name="attention",
    )(qkv, qkv, qkv)


def _residual_and_router(x, y, mod_ref, g2_ref, router_ref, x1_ref, h2_ref, aff_ref):
    x1 = x + mod_ref[2:3, :] * y
    x1_ref[...] = x1
    h2 = _ada_rows(x1, g2_ref[...], mod_ref[3:4, :], mod_ref[4:5, :]).astype(BF16)
    h2_ref[...] = h2
    logits = _dot_nt(router_ref[...], h2)
    e = jnp.exp(logits - jnp.max(logits, axis=0, keepdims=True))
    aff_ref[...] = e / jnp.sum(e, axis=0, keepdims=True)


def _mix_out_kernel(oa_ref, gb_ref, gc_ref, xc_ref, gcp_ref, xcp_ref, gcn_ref, xcn_ref,
                    ctx_ref, x_ref, mod_ref, wout_ref, cw_ref, g2_ref, router_ref,
                    x1_ref, h2_ref, aff_ref, *, n_tiles):
    t = pl.program_id(1)
    has_prev = (t >= 2).astype(F32)
    has_next = jnp.logical_and(t >= 1, t <= n_tiles - 2).astype(F32)
    u = gc_ref[...] * xc_ref[...]
    u_prev = gcp_ref[HALO - 1:HALO, :] * xcp_ref[HALO - 1:HALO, :] * has_prev
    u_next = gcn_ref[0:1, :] * xcn_ref[0:1, :] * has_next
    down, up = _shift_rows(u, u_prev, u_next)
    o_conv = gb_ref[...] * (down * cw_ref[0:1, :] + u * cw_ref[1:2, :] + up * cw_ref[2:3, :])
    d_attn = oa_ref.shape[1]
    y = _dot(oa_ref[...], wout_ref[0:d_attn, :]) + _dot(o_conv, wout_ref[d_attn:, :])
    _residual_and_router(_stream_tile(ctx_ref, x_ref), y, mod_ref, g2_ref, router_ref, x1_ref, h2_ref,
                         aff_ref)


def _mix_out(o_attn, conv, ctx, x, modsel, w_out, conv_w, g2, router_t):
    b, t, d = o_attn.shape
    n_tiles = t // TILE
    hb = TILE // HALO
    last = t // HALO - 1
    prev_map = lambda col: (lambda i, j: (i, jnp.maximum(j * hb - 1, 0), col))
    next_map = lambda col: (lambda i, j: (i, jnp.minimum((j + 1) * hb, last), col))
    tile_map = lambda col: (lambda i, j: (i, j, col))
    n_e = router_t.shape[0]
    kern = functools.partial(_mix_out_kernel, n_tiles=n_tiles)
    return pl.pallas_call(
        kern,
        out_shape=(jax.ShapeDtypeStruct((b, t, d), F32),
                   jax.ShapeDtypeStruct((b, t, d), BF16),
                   jax.ShapeDtypeStruct((b, n_e, t), F32)),
        grid=(b, n_tiles),
        in_specs=[
            pl.BlockSpec((None, TILE, d), tile_map(0)),
            pl.BlockSpec((None, TILE, d), tile_map(0)),
            pl.BlockSpec((None, TILE, d), tile_map(1)),
            pl.BlockSpec((None, TILE, d), tile_map(2)),
            pl.BlockSpec((None, HALO, d), prev_map(1)),
            pl.BlockSpec((None, HALO, d), prev_map(2)),
            pl.BlockSpec((None, HALO, d), next_map(1)),
            pl.BlockSpec((None, HALO, d), next_map(2)),
        ] + _stream_specs(d) + [
            _mod_spec(d),
            pl.BlockSpec(w_out.shape, lambda i, j: (0, 0)),
            pl.BlockSpec(conv_w.shape, lambda i, j: (0, 0)),
            pl.BlockSpec((1, d), lambda i, j: (0, 0)),
            pl.BlockSpec((n_e, d), lambda i, j: (0, 0)),
        ],
        out_specs=(pl.BlockSpec((None, TILE, d), tile_map(0)),
                   pl.BlockSpec((None, TILE, d), tile_map(0)),
                   pl.BlockSpec((None, n_e, TILE), lambda i, j: (i, 0, j))),
        compiler_params=_cparams(("arbitrary", "arbitrary")),
        name="mix_out",
    )(o_attn, conv, conv, conv, conv, conv, conv, conv, ctx, x, modsel, w_out, conv_w, g2, router_t)


def _slot_windows(bnd_ref, sample, tile, n_e, slot0, cap, win, group):
    starts, fits = [], []
    for e in range(n_e):
        lo = bnd_ref[sample, e, tile]
        hi = bnd_ref[sample, n_e + e, tile]
        w0 = jnp.minimum((lo >> 4) << 4, slot0 + cap - win)
        ok = hi <= w0 + win
        if e % group == 0:
            fits.append(ok)
        else:
            fits[-1] = jnp.logical_and(fits[-1], ok)
        starts.append(pl.multiple_of(w0, 16))
    return starts, fits


def _moe_gather_kernel(bnd_ref, slot_ref, h_ref, x_ref, *, sets):
    n_e = x_ref.shape[0]
    n_tok = slot_ref.shape[1]
    sample = pl.program_id(0)
    tile = pl.program_id(1)

    @pl.when(tile == 0)
    def _():
        x_ref[...] = jnp.zeros_like(x_ref)

    def onehot(e, first, n):
        rows = lax.broadcasted_iota(jnp.int32, (n, n_tok), 0) + first
        return jnp.where(slot_ref[e:e + 1, :] == rows.astype(F32), 1.0, 0.0).astype(BF16)

    for start, length, slot0, cap in sets:
        first_tile, last_tile = start // TILE, (start + length) // TILE
        win = min(GATHER_WIN, cap)

        @pl.when(jnp.logical_and(tile >= first_tile, tile < last_tile))
        def _():
            starts, (fits,) = _slot_windows(bnd_ref, sample, tile, n_e, slot0, cap, win, group=n_e)

            @pl.when(fits)
            def _():
                oh = jnp.concatenate([onehot(e, starts[e], win) for e in range(n_e)], axis=0)
                part = jnp.dot(oh, h_ref[...], preferred_element_type=F32).astype(BF16)
                for e in range(n_e):
                    x_ref[e, pl.ds(starts[e], win), :] += part[e * win:(e + 1) * win]

            @pl.when(jnp.logical_not(fits))
            def _():
                for e in range(n_e):
                    x_ref[e, slot0:slot0 + cap, :] += jnp.dot(
                        onehot(e, slot0, cap), h_ref[...], preferred_element_type=F32).astype(BF16)


def _moe_gather(slots, bounds, h2, sets, n_slots):
    b, n_e, t = slots.shape
    d = h2.shape[2]
    grid_spec = pltpu.PrefetchScalarGridSpec(
        num_scalar_prefetch=1,
        grid=(b, t // TILE),
        in_specs=[
            pl.BlockSpec((None, n_e, TILE), lambda i, j, bnd: (i, 0, j)),
            pl.BlockSpec((None, TILE, d), lambda i, j, bnd: (i, j, 0)),
        ],
        out_specs=pl.BlockSpec((None, n_e, n_slots, d), lambda i, j, bnd: (i, 0, 0, 0)),
    )
    return pl.pallas_call(
        functools.partial(_moe_gather_kernel, sets=sets),
        out_shape=jax.ShapeDtypeStruct((b, n_e, n_slots, d), BF16),
        grid_spec=grid_spec,
        compiler_params=_cparams(("arbitrary", "arbitrary")),
        name="moe_gather",
    )(bounds, slots, h2)


def _moe_ffn_kernel(slot_ref, aff_ref, x_ref, wg_ref, wu_ref, wd_ref, o_ref, wg_s, wu_s, wd_s):
    @pl.when(pl.program_id(1) == 0)
    def _():
        wg_s[...] = wg_ref[...].astype(BF16)
        wu_s[...] = wu_ref[...].astype(BF16)
        wd_s[...] = wd_ref[...].astype(BF16)

    nb, r, d = x_ref.shape
    rows = lax.broadcasted_iota(jnp.int32, (r, slot_ref.shape[2]), 0).astype(F32)
    a_row = jnp.concatenate(
        [jnp.sum(jnp.where(slot_ref[i] == rows, aff_ref[i], 0.0), axis=-1, keepdims=True) for i in range(nb)],
        axis=0)
    x = x_ref[...].reshape(nb * r, d)
    gate = jnp.dot(x, wg_s[...], preferred_element_type=F32)
    up = jnp.dot(x, wu_s[...], preferred_element_type=F32)
    act = (gate * _sigmoid(gate) * up).astype(BF16)
    out = jnp.dot(act, wd_s[...], preferred_element_type=F32) * a_row
    o_ref[...] = out.astype(BF16).reshape(nb, r, d)


def _moe_ffn(slots, aff_t, xs, w_gate, w_up, w_down, layer):
    b, n_e, t = slots.shape
    _, _, r, d = xs.shape
    f = w_gate.shape[3]
    nb = MOE_FFN_SAMPLES if b % MOE_FFN_SAMPLES == 0 else 1
    row = pl.BlockSpec((nb, None, 1, t), lambda e, i: (i, e, 0, 0))
    tok = pl.BlockSpec((nb, None, r, d), lambda e, i: (i, e, 0, 0))
    return pl.pallas_call(
        _moe_ffn_kernel,
        out_shape=jax.ShapeDtypeStruct((b, n_e, r, d), BF16),
        grid=(n_e, b // nb),
        in_specs=[
            row, row, tok,
            pl.BlockSpec((None, None, d, f), lambda e, i: (layer, e, 0, 0)),
            pl.BlockSpec((None, None, d, f), lambda e, i: (layer, e, 0, 0)),
            pl.BlockSpec((None, None, f, d), lambda e, i: (layer, e, 0, 0)),
        ],
        out_specs=tok,
        scratch_shapes=[pltpu.VMEM((d, f), BF16), pltpu.VMEM((d, f), BF16), pltpu.VMEM((f, d), BF16)],
        compiler_params=_cparams(("arbitrary", "arbitrary")),
        name="moe_ffn",
    )(slots.reshape(b, n_e, 1, t), aff_t.reshape(b, n_e, 1, t), xs, w_gate, w_up, w_down)


def _prefix_count(mask, tri):
    return _prefix_count_blocks(mask, tri)[0]


def _prefix_count_blocks(mask, tri):
    blk = tri.shape[0]
    carry = jnp.zeros((mask.shape[0], 1), F32)
    parts, carries = [], [carry]
    for j in range(mask.shape[1] // blk):
        m = mask[:, j * blk:(j + 1) * blk]
        parts.append(jnp.dot(m.astype(BF16), tri, preferred_element_type=F32) + carry)
        carry = carry + jnp.sum(m, axis=-1, keepdims=True)
        carries.append(carry)
    return jnp.concatenate(parts, axis=-1), carries


def _route_kernel(aff_ref, slot_ref, bound_ref, *, sets):
    blk = TILE
    n_e = aff_ref.shape[0]
    tri = jnp.where(lax.broadcasted_iota(jnp.int32, (blk, blk), 0)
                    <= lax.broadcasted_iota(jnp.int32, (blk, blk), 1), 1.0, 0.0).astype(BF16)
    lane = lax.broadcasted_iota(jnp.int32, (n_e, LANES), 1)
    lo = jnp.zeros((n_e, LANES), F32)
    hi = jnp.zeros((n_e, LANES), F32)
    for start, length, slot0, cap in sets:
        cap = float(cap)
        x = aff_ref[:, start:start + length]
        bits = pltpu.bitcast(x, jnp.int32)
        thr = jnp.zeros((bits.shape[0], 1), jnp.int32)
        for bit in range(30, -1, -1):
            cand = thr | (1 << bit)
            cnt = jnp.sum(jnp.where(bits >= cand, 1.0, 0.0), axis=-1, keepdims=True)
            thr = jnp.where(cnt >= cap, cand, thr)
        pivot = jnp.max(jnp.where(bits <= thr, x, -1.0), axis=-1, keepdims=True)

        def counts(p):
            return (jnp.sum(jnp.where(x > p, 1.0, 0.0), axis=-1, keepdims=True),
                    jnp.sum(jnp.where(x >= p, 1.0, 0.0), axis=-1, keepdims=True))

        def misplaced(p):
            n_gt, n_ge = counts(p)
            return jnp.max(jnp.where(jnp.logical_or(n_gt >= cap, n_ge < cap), 1.0, 0.0)) > 0.0

        def walk(p):
            n_gt, n_ge = counts(p)
            higher = jnp.min(jnp.where(x > p, x, jnp.inf), axis=-1, keepdims=True)
            lower = jnp.max(jnp.where(x < p, x, -1.0), axis=-1, keepdims=True)
            return jnp.where(n_gt >= cap, higher, jnp.where(n_ge < cap, lower, p))

        pivot = lax.while_loop(misplaced, walk, pivot)
        above = jnp.where(x > pivot, 1.0, 0.0)
        tied = jnp.where(x == pivot, 1.0, 0.0)
        need = cap - jnp.sum(above, axis=-1, keepdims=True)
        chosen = above + jnp.where(_prefix_count(tied, tri) <= need, tied, 0.0)
        count, before = _prefix_count_blocks(chosen, tri)
        slot_ref[:, start:start + length] = jnp.where(chosen > 0.0, count - 1.0 + slot0, -1.0)
        for j in range(length // blk):
            here = lane == (start // blk + j)
            lo = jnp.where(here, before[j] + slot0, lo)
            hi = jnp.where(here, before[j + 1] + slot0, hi)
    bound_ref[0:n_e, :] = lo
    bound_ref[n_e:, :] = hi


def _route(aff_t, sets):
    b, n_e, t = aff_t.shape
    assert t // TILE <= LANES
    kern = functools.partial(_route_kernel, sets=tuple(sets))
    spec = pl.BlockSpec((None, n_e, t), lambda i: (i, 0, 0))
    bspec = pl.BlockSpec((None, 2 * n_e, LANES), lambda i: (i, 0, 0))
    return pl.pallas_call(
        kern,
        out_shape=(jax.ShapeDtypeStruct((b, n_e, t), F32),
                   jax.ShapeDtypeStruct((b, 2 * n_e, LANES), F32)),
        grid=(b,),
        in_specs=[spec],
        out_specs=(spec, bspec),
        compiler_params=_cparams(("arbitrary",)),
        name="moe_route",
    )(aff_t)


def _combine_kernel(bnd_ref, slot_ref, o_ref, x_ref, mod_ref, g_ref, out_ref, *, sets, final_norm):
    n_e = o_ref.shape[0]
    n_tok = slot_ref.shape[1]
    sample = pl.program_id(0)
    tile = pl.program_id(1)
    tn = lambda a, w: lax.dot_general(a, w, (((0,), (0,)), ((), ())), preferred_element_type=F32)

    def onehot(e, first, n):
        rows = lax.broadcasted_iota(jnp.int32, (n, n_tok), 0) + first
        return jnp.where(slot_ref[e:e + 1, :] == rows.astype(F32), 1.0, 0.0).astype(BF16)

    def scatter_full(experts, slot0, cap):
        y = None
        for e in experts:
            term = tn(onehot(e, slot0, cap), o_ref[e, slot0:slot0 + cap, :])
            y = term if y is None else y + term
        return y

    for start, length, slot0, cap in sets:
        first_tile, last_tile = start // TILE, (start + length) // TILE

        def finish(y):
            x2 = x_ref[...] + mod_ref[5:6, :] * y
            out_ref[...] = _rms_rows(x2, g_ref[...]) if final_norm else x2

        @pl.when(jnp.logical_and(tile >= first_tile, tile < last_tile))
        def _():
            win = min(SCATTER_WIN, cap)
            starts, (fits,) = _slot_windows(bnd_ref, sample, tile, n_e, slot0, cap, win, group=n_e)

            @pl.when(fits)
            def _():
                oh = jnp.concatenate([onehot(e, starts[e], win) for e in range(n_e)], axis=0)
                rows = jnp.concatenate([o_ref[e, pl.ds(starts[e], win), :] for e in range(n_e)], axis=0)
                finish(tn(oh, rows))

            @pl.when(jnp.logical_not(fits))
            def _():
                finish(scatter_full(range(n_e), slot0, cap))


def _combine(slots, bounds, outs, x1, modsel, g, sets, *, tile_offset, final_norm):
    b, t, d = x1.shape
    _, n_e, r, _ = outs.shape
    assert all(s[2] % 16 == 0 and s[3] % 16 == 0 for s in sets)
    kern = functools.partial(_combine_kernel, sets=sets, final_norm=final_norm)
    grid_spec = pltpu.PrefetchScalarGridSpec(
        num_scalar_prefetch=1,
        grid=(b, t // TILE),
        in_specs=[
            pl.BlockSpec((None, n_e, TILE), lambda i, j, bnd: (i, 0, j)),
            pl.BlockSpec((None, n_e, r, d), lambda i, j, bnd: (i, 0, 0, 0)),
            pl.BlockSpec((None, TILE, d), lambda i, j, bnd: (i, j, 0)),
            pl.BlockSpec((None, None, 6, d), lambda i, j, bnd: (i, jnp.minimum(j + tile_offset, 1), 0, 0)),
            pl.BlockSpec((1, d), lambda i, j, bnd: (0, 0)),
        ],
        out_specs=pl.BlockSpec((None, TILE, d), lambda i, j, bnd: (i, j, 0)),
    )
    return pl.pallas_call(
        kern,
        out_shape=jax.ShapeDtypeStruct((b, t, d), F32),
        grid_spec=grid_spec,
        compiler_params=_cparams(("arbitrary", "arbitrary")),
        name="moe_combine",
    )(bounds, slots, outs, x1, modsel, g)


def _rwkv_feat_kernel(x_ref, xp_ref, xn_ref, mod_ref, g_ref, mu_ref, wr_ref, wk_ref, wv_ref,
                      w1_ref, a1_ref, g1_ref, w2_ref, a2_ref, g2_ref, w0_ref, a0_ref,
                      kk_ref, ka_ref, segr_ref, segb_ref,
                      r_out, v_out, kkn_out, g_out, lw_out, kd_out, bb_out, *, n_tiles):
    t = pl.program_id(1)
    has_prev = (t >= 2).astype(F32)
    has_next = jnp.logical_and(t >= 1, t <= n_tiles - 2).astype(F32)
    g = g_ref[...]
    shift = mod_ref[0:1, :]
    scale = mod_ref[1:2, :]
    h = _ada_rows(x_ref[...], g, shift, scale)
    h_prev = _ada_rows(xp_ref[HALO - 1:HALO, :], g, shift, scale) * has_prev
    h_next = _ada_rows(xn_ref[0:1, :], g, shift, scale) * has_next
    down, up = _shift_rows(h, h_prev, h_next)
    xx = 0.5 * (down + up) - h
    d = h.shape[1]

    r_out[...] = _dot(h + xx * mu_ref[0:1, :], wr_ref[...])
    k = _dot(h + xx * mu_ref[2:3, :], wk_ref[...])
    v_out[...] = _dot(h + xx * mu_ref[3:4, :], wv_ref[...])
    gate = _sigmoid(_dot(h + xx * mu_ref[5:6, :], g1_ref[...]))
    g_out[...] = _dot(gate, g2_ref[...])

    kkp = k * kk_ref[...]
    ss = _segsum(kkp * kkp, segr_ref[...], segb_ref[...])
    kkn = kkp * lax.rsqrt(jnp.maximum(ss, 1e-24))
    kkn_out[...] = kkn

    lora_w = _dot(jnp.tanh(_dot(h + xx * mu_ref[1:2, :], w1_ref[...])), w2_ref[...])
    lora_a = _dot(_dot(h + xx * mu_ref[4:5, :], a1_ref[...]), a2_ref[...])
    for di in range(2):
        sl = slice(di * d, (di + 1) * d)
        lw_out[di] = -DECAY_SCALE * _sigmoid(w0_ref[di:di + 1, :] + lora_w[:, sl])
        a = _sigmoid(a0_ref[di:di + 1, :] + lora_a[:, sl])
        kd_out[di] = k * (1.0 + (a - 1.0) * ka_ref[...])
        bb_out[di] = a * kkn


def _rwkv_feat(xs, modsel, g, p):
    b, t, d = xs.shape
    n_tiles = t // TILE
    hb = TILE // HALO
    last = t // HALO - 1
    full = lambda a: pl.BlockSpec(a.shape, lambda i, j: (0,) * a.ndim)
    weights = [p["mu"], p["w_r"], p["w_k"], p["w_v"], p["w1"], p["a1"], p["g1"], p["w2"], p["a2"],
               p["g2"], p["w0"], p["a0"], p["k_k"], p["k_a"], p["seg_r"], p["seg_b"]]
    tile_spec = pl.BlockSpec((None, TILE, d), lambda i, j: (i, j, 0))
    dir_spec = pl.BlockSpec((2, None, TILE, d), lambda i, j: (0, i, j, 0))
    kern = functools.partial(_rwkv_feat_kernel, n_tiles=n_tiles)
    one = jax.ShapeDtypeStruct((b, t, d), F32)
    two = jax.ShapeDtypeStruct((2, b, t, d), F32)
    return pl.pallas_call(
        kern,
        out_shape=(one, one, one, one, two, two, two),
        grid=(b, n_tiles),
        in_specs=[
            tile_spec,
            pl.BlockSpec((None, HALO, d), lambda i, j: (i, jnp.maximum(j * hb - 1, 0), 0)),
            pl.BlockSpec((None, HALO, d), lambda i, j: (i, jnp.minimum((j + 1) * hb, last), 0)),
            _mod_spec(d),
            pl.BlockSpec((1, d), lambda i, j: (0, 0)),
        ] + [full(a) for a in weights],
        out_specs=(tile_spec, tile_spec, tile_spec, tile_spec, dir_spec, dir_spec, dir_spec),
        compiler_params=_cparams(("arbitrary", "arbitrary")),
        name="rwkv_feat",
    )(xs, xs, xs, modsel, g, *weights)


def _scan_kernel(lw_ref, kd_ref, bb_ref, kk_ref, r_ref, v_ref, *rest, rev):
    add_ref = rest[0] if len(rest) == 3 else None
    y_ref, s_ref = rest[-2:]

    @pl.when(pl.program_id(1) == 0)
    def _():
        s_ref[...] = jnp.zeros_like(s_ref)

    L = CHUNK
    ti = lax.broadcasted_iota(jnp.int32, (L, LANES), 0)
    ii = lax.broadcasted_iota(jnp.int32, (L, LANES), 1) & (L - 1)
    before = (ii > ti) if rev else (ii < ti)
    before_eq = (ii >= ti) if rev else (ii <= ti)
    t2 = lax.broadcasted_iota(jnp.int32, (L, L), 0)
    i2 = lax.broadcasted_iota(jnp.int32, (L, L), 1)
    tri = jnp.where((i2 >= t2) if rev else (i2 <= t2), 1.0, 0.0).astype(BF16)
    r128 = lax.broadcasted_iota(jnp.int32, (LANES, LANES), 0)
    c128 = lax.broadcasted_iota(jnp.int32, (LANES, LANES), 1)
    head_eq = (r128 >> 6) == (c128 >> 6)
    eye = r128 == c128

    def same_block(log2s):
        return (ti >> log2s) == (ii >> log2s)

    def bdiag(y):
        yb = y.astype(BF16)
        return jnp.where(head_eq, jnp.concatenate([yb, yb], axis=0), jnp.zeros((), BF16))

    def hmul(x, y):
        return jnp.dot(x.astype(BF16), bdiag(y), preferred_element_type=F32)

    def hmul2(x, y, z):
        return jnp.dot(x.astype(BF16), jnp.concatenate([bdiag(y), bdiag(z)], axis=1),
                       preferred_element_type=F32)

    n_sub = lw_ref.shape[0] // L
    n_pair = lw_ref.shape[1] // LANES
    sub_order = list(range(n_sub))[::-1] if rev else list(range(n_sub))
    chains = [(slice(s * L, (s + 1) * L), slice(p * LANES, (p + 1) * LANES))
              for s in sub_order for p in range(n_pair)]

    def each(f, *cols):
        return [f(*args) for args in zip(*cols)]

    lam_sub = {s: _split_dot_left(tri, lw_ref[s * L:(s + 1) * L, :]) for s in sub_order}
    lw = [lw_ref[rs_, cs] for rs_, cs in chains]
    lam = [lam_sub[rs_.start // L][:, cs] for rs_, cs in chains]
    kk = [kk_ref[rs_, cs] for rs_, cs in chains]
    bb = [bb_ref[rs_, cs] for rs_, cs in chains]
    kd = [kd_ref[rs_, cs] for rs_, cs in chains]
    r = [r_ref[rs_, cs] for rs_, cs in chains]
    v = [v_ref[rs_, cs] for rs_, cs in chains]

    tot = each(lambda a: jnp.sum(a, axis=0, keepdims=True), lw)
    rho = each(lambda a: a[L // 2:L // 2 + 1, :], lam)
    ks = each(lambda a, b, c: a * jnp.exp(b - c), kk, lam, lw)
    rs = each(lambda a, b: a * jnp.exp(b), r, lam)
    e_neg = each(lambda a: jnp.exp(-a), rho)
    e_inv = each(lambda a, b: jnp.exp(a - b), rho, lam)
    e_end = each(lambda a, b: jnp.exp(a - b), tot, lam)

    lhs = each(lambda a, b, c: jnp.concatenate([a * c, b * c], axis=0).astype(BF16), ks, rs, e_neg)
    nt = lambda a, w: lax.dot_general(a, w, (((1,), (1,)), ((), ())), preferred_element_type=F32)
    g_bk = each(lambda a, b, c, d: nt(a, jnp.concatenate([bdiag(b * d), bdiag(c * d)], axis=0)),
                lhs, bb, kd, e_inv)
    a_b = each(lambda a: jnp.where(before, a[0:L, 0:LANES], 0.0), g_bk)
    m_b = each(lambda a: jnp.where(before_eq, a[L:, 0:LANES], 0.0), g_bk)
    mask_am = jnp.concatenate([before, before_eq], axis=0)
    amv = each(lambda a, b: hmul(jnp.where(mask_am, a[:, LANES:], 0.0), b), g_bk, v)
    akv = each(lambda a: a[0:L], amv)
    mkv = each(lambda a: a[L:], amv)

    ident = jnp.where(ti == ii, 1.0, 0.0)
    blk8 = same_block(3)
    n1 = each(lambda a: jnp.where(blk8, -a, 0.0), a_b)
    n2 = each(hmul, n1, n1)
    n34 = each(lambda a, b: hmul(jnp.concatenate([a, b], axis=0), b), n1, n2)
    n3 = each(lambda a: a[0:L], n34)
    n4 = each(lambda a: a[L:], n34)
    x = each(lambda a, b, c: ident + a + b + c, n1, n2, n3)
    tinv = each(lambda a, b: a + hmul(a, b), x, n4)
    for log2s in (3, 4, 5):
        off_mask = jnp.logical_and(same_block(log2s + 1), jnp.logical_not(same_block(log2s)))
        off = each(lambda a: jnp.where(off_mask, a, 0.0), a_b)
        corr = each(hmul, off, tinv)
        tinv = each(lambda a, b: a - hmul(a, b), tinv, corr)

    qp = each(lambda a, b, c: -hmul2(a, b, c), tinv, ks, akv)
    q1 = each(lambda a: a[:, 0:LANES], qp)
    p0 = each(lambda a: a[:, LANES:], qp)
    mqp = each(hmul2, m_b, q1, p0)
    q2 = each(lambda a, b: a + b[:, 0:LANES], rs, mqp)
    y0 = each(lambda a, b: a[:, LANES:] + b, mqp, mkv)
    end = each(lambda a, b, c: jnp.concatenate([a * c, b * c], axis=0), bb, kd, e_end)
    zero = jnp.zeros((L, LANES), F32)
    mn = each(lambda a, b, c, d: _dot_tn(a, jnp.concatenate(
        [jnp.concatenate([b, c], axis=1), jnp.concatenate([zero, d], axis=1)], axis=0)),
        end, q1, p0, v)
    m_t = each(lambda a, c: jnp.where(head_eq, a[:, 0:LANES], 0.0) + jnp.where(eye, jnp.exp(c), 0.0),
               mn, tot)
    n0 = each(lambda a: jnp.where(head_eq, a[:, LANES:], 0.0), mn)
    qm = each(lambda a, b: jnp.concatenate([a, b], axis=0).astype(BF16), q2, m_t)

    state = [s_ref[p] for p in range(n_pair)]
    for idx, (rs_, cs) in enumerate(chains):
        p = idx % n_pair
        out = jnp.dot(qm[idx], state[p].astype(BF16), preferred_element_type=F32)
        y = out[0:L] + y0[idx]
        y_ref[rs_, cs] = y if add_ref is None else add_ref[rs_, cs] + y
        state[p] = out[L:] + n0[idx]
    for p in range(n_pair):
        s_ref[p] = state[p]


def _split_dot_left(tri, x):
    acc = None
    rem = x
    for _ in range(3):
        hi = rem.astype(BF16)
        term = jnp.dot(tri, hi, preferred_element_type=F32)
        acc = term if acc is None else acc + term
        rem = rem - hi.astype(F32)
    return acc


def _rwkv_scan(lw, kd, bb, kk, r, v, n_ctx, rev, add=None):
    _, b, t, d = lw.shape
    nb_ctx = n_ctx // SCAN_BLOCK
    nb = t // SCAN_BLOCK
    di = 1 if rev else 0

    def block(j):
        return jnp.where(j < nb_ctx, nb_ctx - 1 - j, nb + nb_ctx - 1 - j) if rev else j

    dir_spec = pl.BlockSpec((None, None, SCAN_BLOCK, d), lambda i, j: (di, i, block(j), 0))
    one_spec = pl.BlockSpec((None, SCAN_BLOCK, d), lambda i, j: (i, block(j), 0))
    extra = [] if add is None else [add]
    return pl.pallas_call(
        functools.partial(_scan_kernel, rev=rev),
        out_shape=jax.ShapeDtypeStruct((b, t, d), F32),
        grid=(b, nb),
        in_specs=[dir_spec, dir_spec, dir_spec, one_spec, one_spec, one_spec] + ([one_spec] if extra else []),
        out_specs=one_spec,
        scratch_shapes=[pltpu.VMEM((d // LANES, LANES, LANES), F32)],
        compiler_params=_cparams(("arbitrary", "arbitrary")),
        name="rwkv_scan_bwd" if rev else "rwkv_scan_fwd",
    )(lw, kd, bb, kk, r, v, *extra)


def _rwkv_out_kernel(y_ref, r_ref, v_ref, g_ref, kdf_ref, kdb_ref, x_ref, mod_ref,
                     wo_ref, lnw_ref, lnb_ref, rk_ref, segr_ref, segb_ref, g2_ref, router_ref,
                     x1_ref, h2_ref, aff_ref):
    seg_r = segr_ref[...]
    seg_b = segb_ref[...]
    inv_n = 1.0 / RW_N
    y = y_ref[...]
    mean = _segsum(y, seg_r, seg_b) * inv_n
    dev = y - mean
    var = _segsum(dev * dev, seg_r, seg_b) * inv_n
    yn = dev * lax.rsqrt(var + GN_EPS) * lnw_ref[...] + lnb_ref[...]
    r = r_ref[...]
    rk = rk_ref[...]
    bonus = _segsum(r * kdf_ref[...] * rk + r * kdb_ref[...] * rk, seg_r, seg_b) * v_ref[...]
    out = _dot((yn + bonus) * g_ref[...], wo_ref[...])
    _residual_and_router(x_ref[...], out, mod_ref, g2_ref, router_ref, x1_ref, h2_ref, aff_ref)


def _rwkv_out(y, r, v, g, kd, xs, modsel, p, g2, router_t, n_ctx):
    b, t, d = xs.shape
    off = n_ctx // TILE
    n = t - n_ctx
    n_e = router_t.shape[0]
    tile = pl.BlockSpec((None, TILE, d), lambda i, j: (i, j + off, 0))
    dir0 = pl.BlockSpec((None, None, TILE, d), lambda i, j: (0, i, j + off, 0))
    dir1 = pl.BlockSpec((None, None, TILE, d), lambda i, j: (1, i, j + off, 0))
    vec = pl.BlockSpec((1, d), lambda i, j: (0, 0))
    out_tile = pl.BlockSpec((None, TILE, d), lambda i, j: (i, j, 0))
    return pl.pallas_call(
        _rwkv_out_kernel,
        out_shape=(jax.ShapeDtypeStruct((b, n, d), F32),
                   jax.ShapeDtypeStruct((b, n, d), BF16),
                   jax.ShapeDtypeStruct((b, n_e, n), F32)),
        grid=(b, n // TILE),
        in_specs=[tile, tile, tile, tile, dir0, dir1, tile,
                  pl.BlockSpec((None, None, 6, d), lambda i, j: (i, 1, 0, 0)),
                  pl.BlockSpec((d, d), lambda i, j: (0, 0)),
                  vec, vec, vec,
                  pl.BlockSpec((d, LANES), lambda i, j: (0, 0)),
                  pl.BlockSpec((LANES, d), lambda i, j: (0, 0)),
                  vec,
                  pl.BlockSpec((n_e, d), lambda i, j: (0, 0))],
        out_specs=(out_tile, out_tile, pl.BlockSpec((None, n_e, TILE), lambda i, j: (i, 0, j))),
        compiler_params=_cparams(("arbitrary", "arbitrary")),
        name="rwkv_out",
    )(y, r, v, g, kd, kd, xs, modsel, p["w_o"], p["ln_w"], p["ln_b"], p["r_k"], p["seg_r"],
      p["seg_b"], g2, router_t)


def _moe_block(x1, h2, aff_t, token_sets, modsel, g, w_gate, w_up, w_down, layer, *, tile_offset,
               final_norm):
    sets, n_slots = [], 0
    for start, length in token_sets:
        cap = EC_CAPACITY * length // N_EXPERTS
        sets.append((start, length, n_slots, cap))
        n_slots += cap
    sets = tuple(sets)
    slots, bounds = _route(aff_t, sets)
    bounds = bounds[:, :, :aff_t.shape[2] // TILE].astype(jnp.int32)
    xs = _moe_gather(slots, bounds, h2, sets, n_slots)
    outs = _moe_ffn(slots, aff_t, xs, w_gate, w_up, w_down, layer)
    return _combine(slots, bounds, outs, x1, modsel, g, sets, tile_offset=tile_offset,
                    final_norm=final_norm)


def _rope_tables(n_ctx, n):
    rows = n // GRID_W
    row = jnp.broadcast_to(jnp.arange(rows)[:, None], (rows, GRID_W)).reshape(-1).astype(F32)
    col = jnp.broadcast_to(jnp.arange(GRID_W)[None, :], (rows, GRID_W)).reshape(-1).astype(F32)
    pairs = HEAD_DIM // 4
    inv = ROPE_THETA ** (-jnp.arange(pairs, dtype=F32) / pairs)
    ang = jnp.concatenate([row[:, None] * inv, col[:, None] * inv], axis=-1)
    cos, sin = jnp.cos(ang), jnp.sin(ang)
    cos2 = jnp.concatenate([cos, cos], axis=-1)
    sin2 = jnp.concatenate([-sin, sin], axis=-1)
    cos2 = jnp.concatenate([jnp.ones((n_ctx, HEAD_DIM), F32), cos2], axis=0)
    sin2 = jnp.concatenate([jnp.zeros((n_ctx, HEAD_DIM), F32), sin2], axis=0)
    return cos2, sin2


def _block_diag2(a, b):
    z = jnp.zeros_like(a)
    return jnp.concatenate([jnp.concatenate([a, z], axis=1), jnp.concatenate([z, b], axis=1)], axis=0)


def kernel(x, c, ctx, c_ctx, ada_w, ada_b, norm_g, final_g, mix_w_in, mix_w_out, q_norm_g, k_norm_g, conv_w, rw_mu, rw_w_r, rw_w_k, rw_w_v, rw_w_o, rw_w0, rw_w1, rw_w2, rw_a0, rw_a1, rw_a2, rw_g1, rw_g2, rw_k_k, rw_k_a, rw_r_k, rw_ln_w, rw_ln_b, moe_router, moe_w_gate, moe_w_up, moe_w_down):
    b, n, d = x.shape
    n_ctx = ctx.shape[1]
    depth = ada_w.shape[0]
    assert depth == 2 and n_ctx == TILE and n % TILE == 0 and d % LANES == 0

    rows = -(-(b + 1) // HALO) * HALO
    cc = jnp.concatenate([c, c_ctx[None, :], jnp.zeros((rows - b - 1, d), F32)], axis=0)
    mod = _ada_mod(cc, ada_w, ada_b)
    mod_lat = mod[:, :b].reshape(depth, b, 1, 6, d)
    mod_ctx = jnp.broadcast_to(mod[:, b].reshape(depth, 1, 1, 6, d), (depth, b, 1, 6, d))
    modsel = jnp.concatenate([mod_ctx, mod_lat], axis=2)

    router_t = jnp.swapaxes(moe_router, 1, 2).astype(BF16)

    cos2, sin2 = _rope_tables(n_ctx, n)
    qkv, conv = _in_proj(ctx, x, modsel[0], norm_g[0, 0][None], mix_w_in[0].astype(BF16),
                         q_norm_g[0][None], k_norm_g[0][None], cos2, sin2)
    o_attn = _attention(qkv, n_ctx)
    x1, h2, aff_t = _mix_out(o_attn, conv, ctx, x, modsel[0], mix_w_out[0].astype(BF16), conv_w[0],
                             norm_g[0, 1][None], router_t[0])
    xs = _moe_block(x1, h2, aff_t, [(n_ctx, n), (0, n_ctx)], modsel[0], final_g[None],
                    moe_w_gate, moe_w_up, moe_w_down, 0, tile_offset=0, final_norm=False)

    heads = d // RW_N
    assert heads <= LANES
    seg_r = (jnp.arange(d)[:, None] // RW_N == jnp.arange(LANES)[None, :]).astype(BF16)
    p = {
        "mu": rw_mu[0], "w_r": rw_w_r[0].astype(BF16), "w_k": rw_w_k[0].astype(BF16),
        "w_v": rw_w_v[0].astype(BF16), "w_o": rw_w_o[0].astype(BF16),
        "w1": jnp.concatenate([rw_w1[0, 0], rw_w1[0, 1]], axis=1).astype(BF16),
        "a1": jnp.concatenate([rw_a1[0, 0], rw_a1[0, 1]], axis=1).astype(BF16),
        "g1": rw_g1[0].astype(BF16),
        "w2": _block_diag2(rw_w2[0, 0], rw_w2[0, 1]).astype(BF16),
        "a2": _block_diag2(rw_a2[0, 0], rw_a2[0, 1]).astype(BF16),
        "g2": rw_g2[0].astype(BF16),
        "w0": rw_w0[0], "a0": rw_a0[0], "k_k": rw_k_k[0][None], "k_a": rw_k_a[0][None],
        "r_k": rw_r_k[0].reshape(1, d), "ln_w": rw_ln_w[0][None], "ln_b": rw_ln_b[0][None],
        "seg_r": seg_r, "seg_b": seg_r.T,
    }
    r, v, kk, g, lw, kd, bb = _rwkv_feat(xs, modsel[1], norm_g[1, 0][None], p)
    y_fwd = _rwkv_scan(lw, kd, bb, kk, r, v, n_ctx, rev=False)
    y = _rwkv_scan(lw, kd, bb, kk, r, v, n_ctx, rev=True, add=y_fwd)
    x1, h2, aff_t = _rwkv_out(y, r, v, g, kd, xs, modsel[1], p, norm_g[1, 1][None], router_t[1], n_ctx)
    return _moe_block(x1, h2, aff_t, [(0, n)], modsel[1], final_g[None],
                      moe_w_gate, moe_w_up, moe_w_down, 1, tile_offset=1, final_norm=True)
```

```python
import functools

import jax
import jax.numpy as jnp
from jax import lax
from jax.experimental import pallas as pl
from jax.experimental.pallas import tpu as pltpu

F32 = jnp.float32
BF16 = jnp.bfloat16

NORM_EPS = 1e-6
GN_EPS = 64e-5
GRID_W = 64
ROPE_THETA = 10000.0
HEAD_DIM = 128
N_Q_HEADS = 8
N_KV_HEADS = 2
Q_PER_KV = N_Q_HEADS // N_KV_HEADS
RW_N = 64
N_EXPERTS = 16
EC_CAPACITY = 2
LOG2_E = 1.4426950408889634
DECAY_SCALE = 0.6065306597126334

LANES = 128
TILE = 256
HALO = 8
CHUNK = 64
SCAN_BLOCK = 256
GATHER_WIN = 64
MOE_FFN_SAMPLES = 4
SCATTER_WIN = 64
ROW_SAMPLES = 2
VMEM_LIMIT = 56 * 1024 * 1024


def _cparams(sem):
    return pltpu.CompilerParams(dimension_semantics=sem, vmem_limit_bytes=VMEM_LIMIT)


def _dot(a, b):
    return jnp.dot(a.astype(BF16), b.astype(BF16), preferred_element_type=F32)


def _dot_nt(a, b):
    return lax.dot_general(a.astype(BF16), b.astype(BF16), (((1,), (1,)), ((), ())),
                           preferred_element_type=F32)


def _dot_tn(a, b):
    return lax.dot_general(a.astype(BF16), b.astype(BF16), (((0,), (0,)), ((), ())),
                           preferred_element_type=F32)


def _split_dot(x, w, parts):
    acc = None
    rem = x
    for _ in range(parts):
        hi = rem.astype(BF16)
        term = jnp.dot(hi, w, preferred_element_type=F32)
        acc = term if acc is None else acc + term
        rem = rem - hi.astype(F32)
    return acc


def _segsum(x, seg_r, seg_b):
    return _split_dot(_split_dot(x, seg_r, 2), seg_b, 3)


def _rms_rows(x, g):
    ms = jnp.mean(x * x, axis=-1, keepdims=True)
    return x * lax.rsqrt(ms + NORM_EPS) * g


def _ada_rows(x, g, shift, scale):
    return _rms_rows(x, g) * (1.0 + scale) + shift


def _sigmoid(x):
    return 1.0 / (1.0 + jnp.exp(-x))


def _shift_rows(h, prev_row, next_row):
    n = h.shape[0]
    rows = lax.broadcasted_iota(jnp.int32, h.shape, 0)
    down = jnp.where(rows == 0, prev_row, pltpu.roll(h, 1, 0))
    up = jnp.where(rows == n - 1, next_row, pltpu.roll(h, n - 1, 0))
    return down, up


def _ada_mod_kernel(c_ref, w_ref, b_ref, o_ref):
    c = c_ref[...]
    o_ref[...] = _dot(c * _sigmoid(c), w_ref[...]) + b_ref[...]


def _ada_mod(cc, ada_w, ada_b):
    depth, d, n6 = ada_w.shape
    rows = cc.shape[0]
    nb = 1536
    return pl.pallas_call(
        _ada_mod_kernel,
        out_shape=jax.ShapeDtypeStruct((depth, rows, n6), F32),
        grid=(depth, n6 // nb),
        in_specs=[
            pl.BlockSpec((rows, d), lambda l, j: (0, 0)),
            pl.BlockSpec((None, d, nb), lambda l, j: (l, 0, j)),
            pl.BlockSpec((None, 1, nb), lambda l, j: (l, 0, j)),
        ],
        out_specs=pl.BlockSpec((None, rows, nb), lambda l, j: (l, 0, j)),
        compiler_params=_cparams(("arbitrary", "arbitrary")),
        name="ada_mod",
    )(cc, ada_w, ada_b.reshape(depth, 1, n6))


def _mod_spec(d):
    return pl.BlockSpec((None, None, 6, d), lambda b, t: (b, jnp.minimum(t, 1), 0, 0))


def _stream_tile(ctx_ref, x_ref):
    return jnp.where(pl.program_id(1) == 0, ctx_ref[...], x_ref[...])


def _stream_specs(d):
    return [pl.BlockSpec((None, TILE, d), lambda i, j: (i, 0, 0)),
            pl.BlockSpec((None, TILE, d), lambda i, j: (i, jnp.maximum(j - 1, 0), 0))]


def _in_proj_kernel(ctx_ref, x_ref, mod_ref, g_ref, w_ref, qg_ref, kg_ref, cos_ref, sin_ref,
                    qkv_ref, conv_ref, *, n_qkv, n_rot_heads, n_q_heads):
    ns = x_ref.shape[0]
    rows = [slice(s * TILE, (s + 1) * TILE) for s in range(ns)]
    h = jnp.concatenate(
        [_ada_rows(_stream_tile(ctx_ref.at[s], x_ref.at[s]), g_ref[...], mod_ref[s, 0:1, :], mod_ref[s, 1:2, :])
         for s in range(ns)], axis=0).astype(BF16)
    nb = 512
    n_all = w_ref.shape[1]
    cos2 = jnp.concatenate([cos_ref[...]] * ns, axis=0)
    sin2 = jnp.concatenate([sin_ref[...]] * ns, axis=0)
    for j in range(n_all // nb):
        p = jnp.dot(h, w_ref[:, j * nb:(j + 1) * nb], preferred_element_type=F32)
        if j * nb >= n_qkv:
            for s in range(ns):
                conv_ref[s, :, j * nb - n_qkv:(j + 1) * nb - n_qkv] = p[rows[s]]
            continue
        for i in range(nb // HEAD_DIM):
            hd = j * (nb // HEAD_DIM) + i
            t = p[:, i * HEAD_DIM:(i + 1) * HEAD_DIM]
            if hd < n_rot_heads:
                t = _rms_rows(t, qg_ref[...] if hd < n_q_heads else kg_ref[...])
                t = t * cos2 + pltpu.roll(t, HEAD_DIM // 2, 1) * sin2
            if hd < n_q_heads:
                t = t * ((HEAD_DIM ** -0.5) * LOG2_E)
            t = t.astype(BF16)
            for s in range(ns):
                qkv_ref[s, :, hd * HEAD_DIM:(hd + 1) * HEAD_DIM] = t[rows[s]]


def _in_proj(ctx, x, modsel, g, w_in, q_g, k_g, cos2, sin2):
    b, n, d = x.shape
    t = ctx.shape[1] + n
    n_all = w_in.shape[1]
    n_qkv = (N_Q_HEADS + 2 * N_KV_HEADS) * HEAD_DIM
    ns = ROW_SAMPLES if b % ROW_SAMPLES == 0 else 1
    kern = functools.partial(_in_proj_kernel, n_qkv=n_qkv, n_rot_heads=N_Q_HEADS + N_KV_HEADS,
                             n_q_heads=N_Q_HEADS)
    return pl.pallas_call(
        kern,
        out_shape=(jax.ShapeDtypeStruct((b, t, n_qkv), BF16),
                   jax.ShapeDtypeStruct((b, t, n_all - n_qkv), F32)),
        grid=(b // ns, t // TILE),
        in_specs=[
            pl.BlockSpec((ns, TILE, d), lambda i, j: (i, 0, 0)),
            pl.BlockSpec((ns, TILE, d), lambda i, j: (i, jnp.maximum(j - 1, 0), 0)),
            pl.BlockSpec((ns, None, 6, d), lambda i, j: (i, jnp.minimum(j, 1), 0, 0)),
            pl.BlockSpec((1, d), lambda i, j: (0, 0)),
            pl.BlockSpec((d, n_all), lambda i, j: (0, 0)),
            pl.BlockSpec((1, HEAD_DIM), lambda i, j: (0, 0)),
            pl.BlockSpec((1, HEAD_DIM), lambda i, j: (0, 0)),
            pl.BlockSpec((TILE, HEAD_DIM), lambda i, j: (j, 0)),
            pl.BlockSpec((TILE, HEAD_DIM), lambda i, j: (j, 0)),
        ],
        out_specs=(pl.BlockSpec((ns, TILE, n_qkv), lambda i, j: (i, j, 0)),
                   pl.BlockSpec((ns, TILE, n_all - n_qkv), lambda i, j: (i, j, 0))),
        compiler_params=_cparams(("arbitrary", "arbitrary")),
        name="in_proj",
    )(ctx, x, modsel, g, w_in, q_g, k_g, cos2, sin2)


def _attn_kernel(q_ref, k_ref, v_ref, o_ref, *, n_ctx):
    heads = [slice(g * HEAD_DIM, (g + 1) * HEAD_DIM) for g in range(Q_PER_KV)]

    def attend(nk):
        k = k_ref[0:nk, :]
        v = v_ref[0:nk, :]
        s = _dot_nt(q_ref[:, heads[0]], k)
        for g in range(Q_PER_KV):
            s_next = _dot_nt(q_ref[:, heads[g + 1]], k) if g + 1 < Q_PER_KV else None
            p = jnp.exp2(s - jnp.max(s, axis=-1, keepdims=True))
            l = jnp.sum(p, axis=-1, keepdims=True)
            o_ref[:, heads[g]] = (jnp.dot(p.astype(BF16), v, preferred_element_type=F32) / l).astype(BF16)
            s = s_next

    is_ctx = pl.program_id(2) == 0

    @pl.when(is_ctx)
    def _():
        attend(n_ctx)

    @pl.when(jnp.logical_not(is_ctx))
    def _():
        attend(k_ref.shape[0])


def _attention(qkv, n_ctx):
    b, t, _ = qkv.shape
    kern = functools.partial(_attn_kernel, n_ctx=n_ctx)
    return pl.pallas_call(
        kern,
        out_shape=jax.ShapeDtypeStruct((b, t, N_Q_HEADS * HEAD_DIM), BF16),
        grid=(b, N_KV_HEADS, t // TILE),
        in_specs=[
            pl.BlockSpec((None, TILE, Q_PER_KV * HEAD_DIM), lambda i, kv, j: (i, j, kv)),
            pl.BlockSpec((None, t, HEAD_DIM), lambda i, kv, j: (i, 0, N_Q_HEADS + kv)),
            pl.BlockSpec((None, t, HEAD_DIM), lambda i, kv, j: (i, 0, N_Q_HEADS + N_KV_HEADS + kv)),
        ],
        out_specs=pl.BlockSpec((None, TILE, Q_PER_KV * HEAD_DIM), lambda i, kv, j: (i, j, kv)),
        compiler_params=_cparams(("arbitrary",) * 3),
        name="attention",
    )(qkv, qkv, qkv)


def _residual_and_router(x, y, mod_ref, g2_ref, router_ref, x1_ref, h2_ref, aff_ref):
    x1 = x + mod_ref[2:3, :] * y
    x1_ref[...] = x1
    h2 = _ada_rows(x1, g2_ref[...], mod_ref[3:4, :], mod_ref[4:5, :]).astype(BF16)
    h2_ref[...] = h2
    logits = _dot_nt(router_ref[...], h2)
    e = jnp.exp(logits - jnp.max(logits, axis=0, keepdims=True))
    aff_ref[...] = e / jnp.sum(e, axis=0, keepdims=True)


def _mix_out_kernel(oa_ref, gb_ref, gc_ref, xc_ref, gcp_ref, xcp_ref, gcn_ref, xcn_ref,
                    ctx_ref, x_ref, mod_ref, wout_ref, cw_ref, g2_ref, router_ref,
                    x1_ref, h2_ref, aff_ref, *, n_tiles):
    t = pl.program_id(1)
    has_prev = (t >= 2).astype(F32)
    has_next = jnp.logical_and(t >= 1, t <= n_tiles - 2).astype(F32)
    u = gc_ref[...] * xc_ref[...]
    u_prev = gcp_ref[HALO - 1:HALO, :] * xcp_ref[HALO - 1:HALO, :] * has_prev
    u_next = gcn_ref[0:1, :] * xcn_ref[0:1, :] * has_next
    down, up = _shift_rows(u, u_prev, u_next)
    o_conv = gb_ref[...] * (down * cw_ref[0:1, :] + u * cw_ref[1:2, :] + up * cw_ref[2:3, :])
    d_attn = oa_ref.shape[1]
    y = _dot(oa_ref[...], wout_ref[0:d_attn, :]) + _dot(o_conv, wout_ref[d_attn:, :])
    _residual_and_router(_stream_tile(ctx_ref, x_ref), y, mod_ref, g2_ref, router_ref, x1_ref, h2_ref,
                         aff_ref)


def _mix_out(o_attn, conv, ctx, x, modsel, w_out, conv_w, g2, router_t):
    b, t, d = o_attn.shape
    n_tiles = t // TILE
    hb = TILE // HALO
    last = t // HALO - 1
    prev_map = lambda col: (lambda i, j: (i, jnp.maximum(j * hb - 1, 0), col))
    next_map = lambda col: (lambda i, j: (i, jnp.minimum((j + 1) * hb, last), col))
    tile_map = lambda col: (lambda i, j: (i, j, col))
    n_e = router_t.shape[0]
    kern = functools.partial(_mix_out_kernel, n_tiles=n_tiles)
    return pl.pallas_call(
        kern,
        out_shape=(jax.ShapeDtypeStruct((b, t, d), F32),
                   jax.ShapeDtypeStruct((b, t, d), BF16),
                   jax.ShapeDtypeStruct((b, n_e, t), F32)),
        grid=(b, n_tiles),
        in_specs=[
            pl.BlockSpec((None, TILE, d), tile_map(0)),
            pl.BlockSpec((None, TILE, d), tile_map(0)),
            pl.BlockSpec((None, TILE, d), tile_map(1)),
            pl.BlockSpec((None, TILE, d), tile_map(2)),
            pl.BlockSpec((None, HALO, d), prev_map(1)),
            pl.BlockSpec((None, HALO, d), prev_map(2)),
            pl.BlockSpec((None, HALO, d), next_map(1)),
            pl.BlockSpec((None, HALO, d), next_map(2)),
        ] + _stream_specs(d) + [
            _mod_spec(d),
            pl.BlockSpec(w_out.shape, lambda i, j: (0, 0)),
            pl.BlockSpec(conv_w.shape, lambda i, j: (0, 0)),
            pl.BlockSpec((1, d), lambda i, j: (0, 0)),
            pl.BlockSpec((n_e, d), lambda i, j: (0, 0)),
        ],
        out_specs=(pl.BlockSpec((None, TILE, d), tile_map(0)),
                   pl.BlockSpec((None, TILE, d), tile_map(0)),
                   pl.BlockSpec((None, n_e, TILE), lambda i, j: (i, 0, j))),
        compiler_params=_cparams(("arbitrary", "arbitrary")),
        name="mix_out",
    )(o_attn, conv, conv, conv, conv, conv, conv, conv, ctx, x, modsel, w_out, conv_w, g2, router_t)


def _onehot_rows(slot_row, first, n):
    rel = slot_row - (first if isinstance(first, int) else first.astype(F32))
    rows = lax.broadcasted_iota(jnp.int32, (n, slot_row.shape[1]), 0).astype(F32)
    return jnp.where(rel == rows, 1.0, 0.0).astype(BF16)


def _slot_windows(bnd_ref, sample, tile, n_e, slot0, cap, win, group):
    starts, fits = [], []
    for e in range(n_e):
        lo = bnd_ref[sample, e, tile]
        hi = bnd_ref[sample, n_e + e, tile]
        w0 = jnp.minimum((lo >> 4) << 4, slot0 + cap - win)
        ok = hi <= w0 + win
        if e % group == 0:
            fits.append(ok)
        else:
            fits[-1] = jnp.logical_and(fits[-1], ok)
        starts.append(pl.multiple_of(w0, 16))
    return starts, fits


def _moe_gather_kernel(bnd_ref, slot_ref, h_ref, x_ref, *, sets):
    n_e = x_ref.shape[0]
    sample = pl.program_id(0)
    tile = pl.program_id(1)

    @pl.when(tile == 0)
    def _():
        x_ref[...] = jnp.zeros_like(x_ref)

    def onehot(e, first, n):
        return _onehot_rows(slot_ref[e:e + 1, :], first, n)

    for start, length, slot0, cap in sets:
        first_tile, last_tile = start // TILE, (start + length) // TILE
        win = min(GATHER_WIN, cap)

        @pl.when(jnp.logical_and(tile >= first_tile, tile < last_tile))
        def _():
            starts, (fits,) = _slot_windows(bnd_ref, sample, tile, n_e, slot0, cap, win, group=n_e)

            @pl.when(fits)
            def _():
                oh = jnp.concatenate([onehot(e, starts[e], win) for e in range(n_e)], axis=0)
                part = jnp.dot(oh, h_ref[...], preferred_element_type=F32).astype(BF16)
                for e in range(n_e):
                    x_ref[e, pl.ds(starts[e], win), :] += part[e * win:(e + 1) * win]

            @pl.when(jnp.logical_not(fits))
            def _():
                for e in range(n_e):
                    x_ref[e, slot0:slot0 + cap, :] += jnp.dot(
                        onehot(e, slot0, cap), h_ref[...], preferred_element_type=F32).astype(BF16)


def _moe_gather(slots, bounds, h2, sets, n_slots):
    b, n_e, t = slots.shape
    d = h2.shape[2]
    grid_spec = pltpu.PrefetchScalarGridSpec(
        num_scalar_prefetch=1,
        grid=(b, t // TILE),
        in_specs=[
            pl.BlockSpec((None, n_e, TILE), lambda i, j, bnd: (i, 0, j)),
            pl.BlockSpec((None, TILE, d), lambda i, j, bnd: (i, j, 0)),
        ],
        out_specs=pl.BlockSpec((None, n_e, n_slots, d), lambda i, j, bnd: (i, 0, 0, 0)),
    )
    return pl.pallas_call(
        functools.partial(_moe_gather_kernel, sets=sets),
        out_shape=jax.ShapeDtypeStruct((b, n_e, n_slots, d), BF16),
        grid_spec=grid_spec,
        compiler_params=_cparams(("arbitrary", "arbitrary")),
        name="moe_gather",
    )(bounds, slots, h2)


def _moe_ffn_kernel(slot_ref, aff_ref, x_ref, wg_ref, wu_ref, wd_ref, o_ref, wg_s, wu_s, wd_s):
    @pl.when(pl.program_id(1) == 0)
    def _():
        wg_s[...] = wg_ref[...].astype(BF16)
        wu_s[...] = wu_ref[...].astype(BF16)
        wd_s[...] = wd_ref[...].astype(BF16)

    nb, r, d = x_ref.shape
    rows = lax.broadcasted_iota(jnp.int32, (r, slot_ref.shape[2]), 0).astype(F32)
    a_row = jnp.concatenate(
        [jnp.sum(jnp.where(slot_ref[i] == rows, aff_ref[i], 0.0), axis=-1, keepdims=True) for i in range(nb)],
        axis=0)
    x = x_ref[...].reshape(nb * r, d)
    gate = jnp.dot(x, wg_s[...], preferred_element_type=F32)
    up = jnp.dot(x, wu_s[...], preferred_element_type=F32)
    act = (gate * _sigmoid(gate) * up).astype(BF16)
    out = jnp.dot(act, wd_s[...], preferred_element_type=F32) * a_row
    o_ref[...] = out.astype(BF16).reshape(nb, r, d)


def _moe_ffn(slots, aff_t, xs, w_gate, w_up, w_down, layer):
    b, n_e, t = slots.shape
    _, _, r, d = xs.shape
    f = w_gate.shape[3]
    nb = MOE_FFN_SAMPLES if b % MOE_FFN_SAMPLES == 0 else 1
    row = pl.BlockSpec((nb, None, 1, t), lambda e, i: (i, e, 0, 0))
    tok = pl.BlockSpec((nb, None, r, d), lambda e, i: (i, e, 0, 0))
    return pl.pallas_call(
        _moe_ffn_kernel,
        out_shape=jax.ShapeDtypeStruct((b, n_e, r, d), BF16),
        grid=(n_e, b // nb),
        in_specs=[
            row, row, tok,
            pl.BlockSpec((None, None, d, f), lambda e, i: (layer, e, 0, 0)),
            pl.BlockSpec((None, None, d, f), lambda e, i: (layer, e, 0, 0)),
            pl.BlockSpec((None, None, f, d), lambda e, i: (layer, e, 0, 0)),
        ],
        out_specs=tok,
        scratch_shapes=[pltpu.VMEM((d, f), BF16), pltpu.VMEM((d, f), BF16), pltpu.VMEM((f, d), BF16)],
        compiler_params=_cparams(("arbitrary", "arbitrary")),
        name="moe_ffn",
    )(slots.reshape(b, n_e, 1, t), aff_t.reshape(b, n_e, 1, t), xs, w_gate, w_up, w_down)


def _prefix_count(mask, tri):
    return _prefix_count_blocks(mask, tri)[0]


def _prefix_count_blocks(mask, tri):
    blk = tri.shape[0]
    carry = jnp.zeros((mask.shape[0], 1), F32)
    parts, carries = [], [carry]
    for j in range(mask.shape[1] // blk):
        m = mask[:, j * blk:(j + 1) * blk]
        parts.append(jnp.dot(m.astype(BF16), tri, preferred_element_type=F32) + carry)
        carry = carry + jnp.sum(m, axis=-1, keepdims=True)
        carries.append(carry)
    return jnp.concatenate(parts, axis=-1), carries


def _route_kernel(aff_ref, slot_ref, bound_ref, *, sets):
    blk = TILE
    n_e = aff_ref.shape[0]
    tri = jnp.where(lax.broadcasted_iota(jnp.int32, (blk, blk), 0)
                    <= lax.broadcasted_iota(jnp.int32, (blk, blk), 1), 1.0, 0.0).astype(BF16)
    lane = lax.broadcasted_iota(jnp.int32, (n_e, LANES), 1)
    lo = jnp.zeros((n_e, LANES), F32)
    hi = jnp.zeros((n_e, LANES), F32)
    for start, length, slot0, cap in sets:
        cap = float(cap)
        x = aff_ref[:, start:start + length]
        bits = pltpu.bitcast(x, jnp.int32)
        thr = jnp.zeros((bits.shape[0], 1), jnp.int32)
        for bit in range(30, -1, -1):
            cand = thr | (1 << bit)
            cnt = jnp.sum(jnp.where(bits >= cand, 1.0, 0.0), axis=-1, keepdims=True)
            thr = jnp.where(cnt >= cap, cand, thr)
        pivot = jnp.max(jnp.where(bits <= thr, x, -1.0), axis=-1, keepdims=True)

        def counts(p):
            return (jnp.sum(jnp.where(x > p, 1.0, 0.0), axis=-1, keepdims=True),
                    jnp.sum(jnp.where(x >= p, 1.0, 0.0), axis=-1, keepdims=True))

        def misplaced(p):
            n_gt, n_ge = counts(p)
            return jnp.max(jnp.where(jnp.logical_or(n_gt >= cap, n_ge < cap), 1.0, 0.0)) > 0.0

        def walk(p):
            n_gt, n_ge = counts(p)
            higher = jnp.min(jnp.where(x > p, x, jnp.inf), axis=-1, keepdims=True)
            lower = jnp.max(jnp.where(x < p, x, -1.0), axis=-1, keepdims=True)
            return jnp.where(n_gt >= cap, higher, jnp.where(n_ge < cap, lower, p))

        pivot = lax.while_loop(misplaced, walk, pivot)
        above = jnp.where(x > pivot, 1.0, 0.0)
        tied = jnp.where(x == pivot, 1.0, 0.0)
        need = cap - jnp.sum(above, axis=-1, keepdims=True)
        chosen = above + jnp.where(_prefix_count(tied, tri) <= need, tied, 0.0)
        count, before = _prefix_count_blocks(chosen, tri)
        slot_ref[:, start:start + length] = jnp.where(chosen > 0.0, count - 1.0 + slot0, -1.0)
        for j in range(length // blk):
            here = lane == (start // blk + j)
            lo = jnp.where(here, before[j] + slot0, lo)
            hi = jnp.where(here, before[j + 1] + slot0, hi)
    bound_ref[0:n_e, :] = lo
    bound_ref[n_e:, :] = hi


def _route(aff_t, sets):
    b, n_e, t = aff_t.shape
    assert t // TILE <= LANES
    kern = functools.partial(_route_kernel, sets=tuple(sets))
    spec = pl.BlockSpec((None, n_e, t), lambda i: (i, 0, 0))
    bspec = pl.BlockSpec((None, 2 * n_e, LANES), lambda i: (i, 0, 0))
    return pl.pallas_call(
        kern,
        out_shape=(jax.ShapeDtypeStruct((b, n_e, t), F32),
                   jax.ShapeDtypeStruct((b, 2 * n_e, LANES), F32)),
        grid=(b,),
        in_specs=[spec],
        out_specs=(spec, bspec),
        compiler_params=_cparams(("arbitrary",)),
        name="moe_route",
    )(aff_t)


def _combine_kernel(bnd_ref, slot_ref, o_ref, x_ref, mod_ref, g_ref, out_ref, *, sets, final_norm):
    n_e = o_ref.shape[0]
    sample = pl.program_id(0)
    tile = pl.program_id(1)
    tn = lambda a, w: lax.dot_general(a, w, (((0,), (0,)), ((), ())), preferred_element_type=F32)

    def onehot(e, first, n):
        return _onehot_rows(slot_ref[e:e + 1, :], first, n)

    def scatter_full(experts, slot0, cap):
        y = None
        for e in experts:
            term = tn(onehot(e, slot0, cap), o_ref[e, slot0:slot0 + cap, :])
            y = term if y is None else y + term
        return y

    for start, length, slot0, cap in sets:
        first_tile, last_tile = start // TILE, (start + length) // TILE

        def finish(y):
            x2 = x_ref[...] + mod_ref[5:6, :] * y
            out_ref[...] = _rms_rows(x2, g_ref[...]) if final_norm else x2

        @pl.when(jnp.logical_and(tile >= first_tile, tile < last_tile))
        def _():
            win = min(SCATTER_WIN, cap)
            starts, (fits,) = _slot_windows(bnd_ref, sample, tile, n_e, slot0, cap, win, group=n_e)

            @pl.when(fits)
            def _():
                oh = jnp.concatenate([onehot(e, starts[e], win) for e in range(n_e)], axis=0)
                rows = jnp.concatenate([o_ref[e, pl.ds(starts[e], win), :] for e in range(n_e)], axis=0)
                finish(tn(oh, rows))

            @pl.when(jnp.logical_not(fits))
            def _():
                finish(scatter_full(range(n_e), slot0, cap))


def _combine(slots, bounds, outs, x1, modsel, g, sets, *, tile_offset, final_norm):
    b, t, d = x1.shape
    _, n_e, r, _ = outs.shape
    assert all(s[2] % 16 == 0 and s[3] % 16 == 0 for s in sets)
    kern = functools.partial(_combine_kernel, sets=sets, final_norm=final_norm)
    grid_spec = pltpu.PrefetchScalarGridSpec(
        num_scalar_prefetch=1,
        grid=(b, t // TILE),
        in_specs=[
            pl.BlockSpec((None, n_e, TILE), lambda i, j, bnd: (i, 0, j)),
            pl.BlockSpec((None, n_e, r, d), lambda i, j, bnd: (i, 0, 0, 0)),
            pl.BlockSpec((None, TILE, d), lambda i, j, bnd: (i, j, 0)),
            pl.BlockSpec((None, None, 6, d), lambda i, j, bnd: (i, jnp.minimum(j + tile_offset, 1), 0, 0)),
            pl.BlockSpec((1, d), lambda i, j, bnd: (0, 0)),
        ],
        out_specs=pl.BlockSpec((None, TILE, d), lambda i, j, bnd: (i, j, 0)),
    )
    return pl.pallas_call(
        kern,
        out_shape=jax.ShapeDtypeStruct((b, t, d), F32),
        grid_spec=grid_spec,
        compiler_params=_cparams(("arbitrary", "arbitrary")),
        name="moe_combine",
    )(bounds, slots, outs, x1, modsel, g)


def _rwkv_feat_kernel(x_ref, xp_ref, xn_ref, mod_ref, g_ref, mu_ref, wr_ref, wk_ref, wv_ref,
                      w1_ref, a1_ref, g1_ref, w2_ref, a2_ref, g2_ref, w0_ref, a0_ref,
                      kk_ref, ka_ref, segr_ref, segb_ref,
                      r_out, v_out, kkn_out, g_out, lw_out, kd_out, bb_out, *, n_tiles):
    t = pl.program_id(1)
    has_prev = (t >= 2).astype(F32)
    has_next = jnp.logical_and(t >= 1, t <= n_tiles - 2).astype(F32)
    g = g_ref[...]
    shift = mod_ref[0:1, :]
    scale = mod_ref[1:2, :]
    h = _ada_rows(x_ref[...], g, shift, scale)
    h_prev = _ada_rows(xp_ref[HALO - 1:HALO, :], g, shift, scale) * has_prev
    h_next = _ada_rows(xn_ref[0:1, :], g, shift, scale) * has_next
    down, up = _shift_rows(h, h_prev, h_next)
    xx = 0.5 * (down + up) - h
    d = h.shape[1]

    r_out[...] = _dot(h + xx * mu_ref[0:1, :], wr_ref[...])
    k = _dot(h + xx * mu_ref[2:3, :], wk_ref[...])
    v_out[...] = _dot(h + xx * mu_ref[3:4, :], wv_ref[...])
    gate = _sigmoid(_dot(h + xx * mu_ref[5:6, :], g1_ref[...]))
    g_out[...] = _dot(gate, g2_ref[...])

    kkp = k * kk_ref[...]
    ss = _segsum(kkp * kkp, segr_ref[...], segb_ref[...])
    kkn = kkp * lax.rsqrt(jnp.maximum(ss, 1e-24))
    kkn_out[...] = kkn

    lora_w = _dot(jnp.tanh(_dot(h + xx * mu_ref[1:2, :], w1_ref[...])), w2_ref[...])
    lora_a = _dot(_dot(h + xx * mu_ref[4:5, :], a1_ref[...]), a2_ref[...])
    for di in range(2):
        sl = slice(di * d, (di + 1) * d)
        lw_out[di] = -DECAY_SCALE * _sigmoid(w0_ref[di:di + 1, :] + lora_w[:, sl])
        a = _sigmoid(a0_ref[di:di + 1, :] + lora_a[:, sl])
        kd_out[di] = k * (1.0 + (a - 1.0) * ka_ref[...])
        bb_out[di] = a * kkn


def _rwkv_feat(xs, modsel, g, p):
    b, t, d = xs.shape
    n_tiles = t // TILE
    hb = TILE // HALO
    last = t // HALO - 1
    full = lambda a: pl.BlockSpec(a.shape, lambda i, j: (0,) * a.ndim)
    weights = [p["mu"], p["w_r"], p["w_k"], p["w_v"], p["w1"], p["a1"], p["g1"], p["w2"], p["a2"],
               p["g2"], p["w0"], p["a0"], p["k_k"], p["k_a"], p["seg_r"], p["seg_b"]]
    tile_spec = pl.BlockSpec((None, TILE, d), lambda i, j: (i, j, 0))
    dir_spec = pl.BlockSpec((2, None, TILE, d), lambda i, j: (0, i, j, 0))
    kern = functools.partial(_rwkv_feat_kernel, n_tiles=n_tiles)
    one = jax.ShapeDtypeStruct((b, t, d), F32)
    two = jax.ShapeDtypeStruct((2, b, t, d), F32)
    return pl.pallas_call(
        kern,
        out_shape=(one, one, one, one, two, two, two),
        grid=(b, n_tiles),
        in_specs=[
            tile_spec,
            pl.BlockSpec((None, HALO, d), lambda i, j: (i, jnp.maximum(j * hb - 1, 0), 0)),
            pl.BlockSpec((None, HALO, d), lambda i, j: (i, jnp.minimum((j + 1) * hb, last), 0)),
            _mod_spec(d),
            pl.BlockSpec((1, d), lambda i, j: (0, 0)),
        ] + [full(a) for a in weights],
        out_specs=(tile_spec, tile_spec, tile_spec, tile_spec, dir_spec, dir_spec, dir_spec),
        compiler_params=_cparams(("arbitrary", "arbitrary")),
        name="rwkv_feat",
    )(xs, xs, xs, modsel, g, *weights)


def _scan_kernel(lw_ref, kd_ref, bb_ref, kk_ref, r_ref, v_ref, *rest, rev):
    add_ref = rest[0] if len(rest) == 3 else None
    y_ref, s_ref = rest[-2:]

    @pl.when(pl.program_id(1) == 0)
    def _():
        s_ref[...] = jnp.zeros_like(s_ref)

    L = CHUNK
    ti = lax.broadcasted_iota(jnp.int32, (L, LANES), 0)
    ii = lax.broadcasted_iota(jnp.int32, (L, LANES), 1) & (L - 1)
    before = (ii > ti) if rev else (ii < ti)
    before_eq = (ii >= ti) if rev else (ii <= ti)
    t2 = lax.broadcasted_iota(jnp.int32, (L, L), 0)
    i2 = lax.broadcasted_iota(jnp.int32, (L, L), 1)
    tri = jnp.where((i2 >= t2) if rev else (i2 <= t2), 1.0, 0.0).astype(BF16)
    r128 = lax.broadcasted_iota(jnp.int32, (LANES, LANES), 0)
    c128 = lax.broadcasted_iota(jnp.int32, (LANES, LANES), 1)
    head_eq = (r128 >> 6) == (c128 >> 6)
    eye = r128 == c128

    def same_block(log2s):
        return (ti >> log2s) == (ii >> log2s)

    def bdiag(y):
        yb = y.astype(BF16)
        return jnp.where(head_eq, jnp.concatenate([yb, yb], axis=0), jnp.zeros((), BF16))

    def hmul(x, y):
        return jnp.dot(x.astype(BF16), bdiag(y), preferred_element_type=F32)

    def hmul2(x, y, z):
        return jnp.dot(x.astype(BF16), jnp.concatenate([bdiag(y), bdiag(z)], axis=1),
                       preferred_element_type=F32)

    n_sub = lw_ref.shape[0] // L
    n_pair = lw_ref.shape[1] // LANES
    sub_order = list(range(n_sub))[::-1] if rev else list(range(n_sub))
    chains = [(slice(s * L, (s + 1) * L), slice(p * LANES, (p + 1) * LANES))
              for s in sub_order for p in range(n_pair)]

    def each(f, *cols):
        return [f(*args) for args in zip(*cols)]

    lam_sub = {s: _split_dot_left(tri, lw_ref[s * L:(s + 1) * L, :]) for s in sub_order}
    lw = [lw_ref[rs_, cs] for rs_, cs in chains]
    lam = [lam_sub[rs_.start // L][:, cs] for rs_, cs in chains]
    kk = [kk_ref[rs_, cs] for rs_, cs in chains]
    bb = [bb_ref[rs_, cs] for rs_, cs in chains]
    kd = [kd_ref[rs_, cs] for rs_, cs in chains]
    r = [r_ref[rs_, cs] for rs_, cs in chains]
    v = [v_ref[rs_, cs] for rs_, cs in chains]

    tot = each(lambda a: jnp.sum(a, axis=0, keepdims=True), lw)
    rho = each(lambda a: a[L // 2:L // 2 + 1, :], lam)
    ks = each(lambda a, b, c: a * jnp.exp(b - c), kk, lam, lw)
    rs = each(lambda a, b: a * jnp.exp(b), r, lam)
    e_neg = each(lambda a: jnp.exp(-a), rho)
    e_inv = each(lambda a, b: jnp.exp(a - b), rho, lam)
    e_end = each(lambda a, b: jnp.exp(a - b), tot, lam)

    lhs = each(lambda a, b, c: jnp.concatenate([a * c, b * c], axis=0).astype(BF16), ks, rs, e_neg)
    nt = lambda a, w: lax.dot_general(a, w, (((1,), (1,)), ((), ())), preferred_element_type=F32)
    g_bk = each(lambda a, b, c, d: nt(a, jnp.concatenate([bdiag(b * d), bdiag(c * d)], axis=0)),
                lhs, bb, kd, e_inv)
    a_b = each(lambda a: jnp.where(before, a[0:L, 0:LANES], 0.0), g_bk)
    m_b = each(lambda a: jnp.where(before_eq, a[L:, 0:LANES], 0.0), g_bk)
    mask_am = jnp.concatenate([before, before_eq], axis=0)
    amv = each(lambda a, b: hmul(jnp.where(mask_am, a[:, LANES:], 0.0), b), g_bk, v)
    akv = each(lambda a: a[0:L], amv)
    mkv = each(lambda a: a[L:], amv)

    ident = jnp.where(ti == ii, 1.0, 0.0)
    blk8 = same_block(3)
    n1 = each(lambda a: jnp.where(blk8, -a, 0.0), a_b)
    n2 = each(hmul, n1, n1)
    n34 = each(lambda a, b: hmul(jnp.concatenate([a, b], axis=0), b), n1, n2)
    n3 = each(lambda a: a[0:L], n34)
    n4 = each(lambda a: a[L:], n34)
    x = each(lambda a, b, c: ident + a + b + c, n1, n2, n3)
    tinv = each(lambda a, b: a + hmul(a, b), x, n4)
    for log2s in (3, 4, 5):
        off_mask = jnp.logical_and(same_block(log2s + 1), jnp.logical_not(same_block(log2s)))
        off = each(lambda a: jnp.where(off_mask, a, 0.0), a_b)
        corr = each(hmul, off, tinv)
        tinv = each(lambda a, b: a - hmul(a, b), tinv, corr)

    qp = each(lambda a, b, c: -hmul2(a, b, c), tinv, ks, akv)
    q1 = each(lambda a: a[:, 0:LANES], qp)
    p0 = each(lambda a: a[:, LANES:], qp)
    mqp = each(hmul2, m_b, q1, p0)
    q2 = each(lambda a, b: a + b[:, 0:LANES], rs, mqp)
    y0 = each(lambda a, b: a[:, LANES:] + b, mqp, mkv)
    end = each(lambda a, b, c: jnp.concatenate([a * c, b * c], axis=0), bb, kd, e_end)
    zero = jnp.zeros((L, LANES), F32)
    mn = each(lambda a, b, c, d: _dot_tn(a, jnp.concatenate(
        [jnp.concatenate([b, c], axis=1), jnp.concatenate([zero, d], axis=1)], axis=0)),
        end, q1, p0, v)
    m_t = each(lambda a, c: jnp.where(head_eq, a[:, 0:LANES], 0.0) + jnp.where(eye, jnp.exp(c), 0.0),
               mn, tot)
    n0 = each(lambda a: jnp.where(head_eq, a[:, LANES:], 0.0), mn)
    qm = each(lambda a, b: jnp.concatenate([a, b], axis=0).astype(BF16), q2, m_t)

    state = [s_ref[p] for p in range(n_pair)]
    for idx, (rs_, cs) in enumerate(chains):
        p = idx % n_pair
        out = jnp.dot(qm[idx], state[p].astype(BF16), preferred_element_type=F32)
        y = out[0:L] + y0[idx]
        y_ref[rs_, cs] = y if add_ref is None else add_ref[rs_, cs] + y
        state[p] = out[L:] + n0[idx]
    for p in range(n_pair):
        s_ref[p] = state[p]


def _split_dot_left(tri, x):
    acc = None
    rem = x
    for _ in range(2):
        hi = rem.astype(BF16)
        term = jnp.dot(tri, hi, preferred_element_type=F32)
        acc = term if acc is None else acc + term
        rem = rem - hi.astype(F32)
    return acc


def _rwkv_scan(lw, kd, bb, kk, r, v, n_ctx, rev, add=None):
    _, b, t, d = lw.shape
    assert n_ctx % SCAN_BLOCK == 0 and t % SCAN_BLOCK == 0
    nb_ctx = n_ctx // SCAN_BLOCK
    nb = t // SCAN_BLOCK
    di = 1 if rev else 0

    def block(j):
        return jnp.where(j < nb_ctx, nb_ctx - 1 - j, nb + nb_ctx - 1 - j) if rev else j

    dir_spec = pl.BlockSpec((None, None, SCAN_BLOCK, d), lambda i, j: (di, i, block(j), 0))
    one_spec = pl.BlockSpec((None, SCAN_BLOCK, d), lambda i, j: (i, block(j), 0))
    extra = [] if add is None else [add]
    return pl.pallas_call(
        functools.partial(_scan_kernel, rev=rev),
        out_shape=jax.ShapeDtypeStruct((b, t, d), F32),
        grid=(b, nb),
        in_specs=[dir_spec, dir_spec, dir_spec, one_spec, one_spec, one_spec] + ([one_spec] if extra else []),
        out_specs=one_spec,
        scratch_shapes=[pltpu.VMEM((d // LANES, LANES, LANES), F32)],
        compiler_params=_cparams(("arbitrary", "arbitrary")),
        name="rwkv_scan_bwd" if rev else "rwkv_scan_fwd",
    )(lw, kd, bb, kk, r, v, *extra)


def _rwkv_out_kernel(y_ref, r_ref, v_ref, g_ref, kdf_ref, kdb_ref, x_ref, mod_ref,
                     wo_ref, lnw_ref, lnb_ref, rk_ref, segr_ref, segb_ref, g2_ref, router_ref,
                     x1_ref, h2_ref, aff_ref):
    seg_r = segr_ref[...]
    seg_b = segb_ref[...]
    inv_n = 1.0 / RW_N
    y = y_ref[...]
    mean = _segsum(y, seg_r, seg_b) * inv_n
    dev = y - mean
    var = _segsum(dev * dev, seg_r, seg_b) * inv_n
    yn = dev * lax.rsqrt(var + GN_EPS) * lnw_ref[...] + lnb_ref[...]
    r = r_ref[...]
    rk = rk_ref[...]
    bonus = _segsum(r * kdf_ref[...] * rk + r * kdb_ref[...] * rk, seg_r, seg_b) * v_ref[...]
    out = _dot((yn + bonus) * g_ref[...], wo_ref[...])
    _residual_and_router(x_ref[...], out, mod_ref, g2_ref, router_ref, x1_ref, h2_ref, aff_ref)


def _rwkv_out(y, r, v, g, kd, xs, modsel, p, g2, router_t, n_ctx):
    b, t, d = xs.shape
    off = n_ctx // TILE
    n = t - n_ctx
    n_e = router_t.shape[0]
    tile = pl.BlockSpec((None, TILE, d), lambda i, j: (i, j + off, 0))
    dir0 = pl.BlockSpec((None, None, TILE, d), lambda i, j: (0, i, j + off, 0))
    dir1 = pl.BlockSpec((None, None, TILE, d), lambda i, j: (1, i, j + off, 0))
    vec = pl.BlockSpec((1, d), lambda i, j: (0, 0))
    out_tile = pl.BlockSpec((None, TILE, d), lambda i, j: (i, j, 0))
    return pl.pallas_call(
        _rwkv_out_kernel,
        out_shape=(jax.ShapeDtypeStruct((b, n, d), F32),
                   jax.ShapeDtypeStruct((b, n, d), BF16),
                   jax.ShapeDtypeStruct((b, n_e, n), F32)),
        grid=(b, n // TILE),
        in_specs=[tile, tile, tile, tile, dir0, dir1, tile,
                  pl.BlockSpec((None, None, 6, d), lambda i, j: (i, 1, 0, 0)),
                  pl.BlockSpec((d, d), lambda i, j: (0, 0)),
                  vec, vec, vec,
                  pl.BlockSpec((d, LANES), lambda i, j: (0, 0)),
                  pl.BlockSpec((LANES, d), lambda i, j: (0, 0)),
                  vec,
                  pl.BlockSpec((n_e, d), lambda i, j: (0, 0))],
        out_specs=(out_tile, out_tile, pl.BlockSpec((None, n_e, TILE), lambda i, j: (i, 0, j))),
        compiler_params=_cparams(("arbitrary", "arbitrary")),
        name="rwkv_out",
    )(y, r, v, g, kd, kd, xs, modsel, p["w_o"], p["ln_w"], p["ln_b"], p["r_k"], p["seg_r"],
      p["seg_b"], g2, router_t)


def _moe_block(x1, h2, aff_t, token_sets, modsel, g, w_gate, w_up, w_down, layer, *, tile_offset,
               final_norm):
    sets, n_slots = [], 0
    for start, length in token_sets:
        cap = EC_CAPACITY * length // N_EXPERTS
        sets.append((start, length, n_slots, cap))
        n_slots += cap
    sets = tuple(sets)
    slots, bounds = _route(aff_t, sets)
    bounds = bounds[:, :, :aff_t.shape[2] // TILE].astype(jnp.int32)
    xs = _moe_gather(slots, bounds, h2, sets, n_slots)
    outs = _moe_ffn(slots, aff_t, xs, w_gate, w_up, w_down, layer)
    return _combine(slots, bounds, outs, x1, modsel, g, sets, tile_offset=tile_offset,
                    final_norm=final_norm)


def _rope_tables(n_ctx, n):
    rows = n // GRID_W
    row = jnp.broadcast_to(jnp.arange(rows)[:, None], (rows, GRID_W)).reshape(-1).astype(F32)
    col = jnp.broadcast_to(jnp.arange(GRID_W)[None, :], (rows, GRID_W)).reshape(-1).astype(F32)
    pairs = HEAD_DIM // 4
    inv = ROPE_THETA ** (-jnp.arange(pairs, dtype=F32) / pairs)
    ang = jnp.concatenate([row[:, None] * inv, col[:, None] * inv], axis=-1)
    cos, sin = jnp.cos(ang), jnp.sin(ang)
    cos2 = jnp.concatenate([cos, cos], axis=-1)
    sin2 = jnp.concatenate([-sin, sin], axis=-1)
    cos2 = jnp.concatenate([jnp.ones((n_ctx, HEAD_DIM), F32), cos2], axis=0)
    sin2 = jnp.concatenate([jnp.zeros((n_ctx, HEAD_DIM), F32), sin2], axis=0)
    return cos2, sin2


def _block_diag2(a, b):
    z = jnp.zeros_like(a)
    return jnp.concatenate([jnp.concatenate([a, z], axis=1), jnp.concatenate([z, b], axis=1)], axis=0)


def kernel(x, c, ctx, c_ctx, ada_w, ada_b, norm_g, final_g, mix_w_in, mix_w_out, q_norm_g, k_norm_g, conv_w, rw_mu, rw_w_r, rw_w_k, rw_w_v, rw_w_o, rw_w0, rw_w1, rw_w2, rw_a0, rw_a1, rw_a2, rw_g1, rw_g2, rw_k_k, rw_k_a, rw_r_k, rw_ln_w, rw_ln_b, moe_router, moe_w_gate, moe_w_up, moe_w_down):
    b, n, d = x.shape
    n_ctx = ctx.shape[1]
    depth = ada_w.shape[0]
    assert depth == 2 and n_ctx == TILE and n % TILE == 0 and d % LANES == 0

    rows = -(-(b + 1) // HALO) * HALO
    cc = jnp.concatenate([c, c_ctx[None, :], jnp.zeros((rows - b - 1, d), F32)], axis=0)
    mod = _ada_mod(cc, ada_w, ada_b)
    mod_lat = mod[:, :b].reshape(depth, b, 1, 6, d)
    mod_ctx = jnp.broadcast_to(mod[:, b].reshape(depth, 1, 1, 6, d), (depth, b, 1, 6, d))
    modsel = jnp.concatenate([mod_ctx, mod_lat], axis=2)

    router_t = jnp.swapaxes(moe_router, 1, 2).astype(BF16)

    cos2, sin2 = _rope_tables(n_ctx, n)
    qkv, conv = _in_proj(ctx, x, modsel[0], norm_g[0, 0][None], mix_w_in[0].astype(BF16),
                         q_norm_g[0][None], k_norm_g[0][None], cos2, sin2)
    o_attn = _attention(qkv, n_ctx)
    x1, h2, aff_t = _mix_out(o_attn, conv, ctx, x, modsel[0], mix_w_out[0].astype(BF16), conv_w[0],
                             norm_g[0, 1][None], router_t[0])
    xs = _moe_block(x1, h2, aff_t, [(n_ctx, n), (0, n_ctx)], modsel[0], final_g[None],
                    moe_w_gate, moe_w_up, moe_w_down, 0, tile_offset=0, final_norm=False)

    heads = d // RW_N
    assert heads <= LANES
    seg_r = (jnp.arange(d)[:, None] // RW_N == jnp.arange(LANES)[None, :]).astype(BF16)
    p = {
        "mu": rw_mu[0], "w_r": rw_w_r[0].astype(BF16), "w_k": rw_w_k[0].astype(BF16),
        "w_v": rw_w_v[0].astype(BF16), "w_o": rw_w_o[0].astype(BF16),
        "w1": jnp.concatenate([rw_w1[0, 0], rw_w1[0, 1]], axis=1).astype(BF16),
        "a1": jnp.concatenate([rw_a1[0, 0], rw_a1[0, 1]], axis=1).astype(BF16),
        "g1": rw_g1[0].astype(BF16),
        "w2": _block_diag2(rw_w2[0, 0], rw_w2[0, 1]).astype(BF16),
        "a2": _block_diag2(rw_a2[0, 0], rw_a2[0, 1]).astype(BF16),
        "g2": rw_g2[0].astype(BF16),
        "w0": rw_w0[0], "a0": rw_a0[0], "k_k": rw_k_k[0][None], "k_a": rw_k_a[0][None],
        "r_k": rw_r_k[0].reshape(1, d), "ln_w": rw_ln_w[0][None], "ln_b": rw_ln_b[0][None],
        "seg_r": seg_r, "seg_b": seg_r.T,
    }
    r, v, kk, g, lw, kd, bb = _rwkv_feat(xs, modsel[1], norm_g[1, 0][None], p)
    y_fwd = _rwkv_scan(lw, kd, bb, kk, r, v, n_ctx, rev=False)
    y = _rwkv_scan(lw, kd, bb, kk, r, v, n_ctx, rev=True, add=y_fwd)
    x1, h2, aff_t = _rwkv_out(y, r, v, g, kd, xs, modsel[1], p, norm_g[1, 1][None], router_t[1], n_ctx)
    return _moe_block(x1, h2, aff_t, [(0, n)], modsel[1], final_g[None],
                      moe_w_gate, moe_w_up, moe_w_down, 1, tile_offset=1, final_norm=True)
```

```python
import functools

import jax
import jax.numpy as jnp
from jax import lax
from jax.experimental import pallas as pl
from jax.experimental.pallas import tpu as pltpu

F32 = jnp.float32
BF16 = jnp.bfloat16

NORM_EPS = 1e-6
GN_EPS = 64e-5
GRID_W = 64
ROPE_THETA = 10000.0
HEAD_DIM = 128
N_Q_HEADS = 8
N_KV_HEADS = 2
Q_PER_KV = N_Q_HEADS // N_KV_HEADS
RW_N = 64
N_EXPERTS = 16
EC_CAPACITY = 2
LOG2_E = 1.4426950408889634
DECAY_SCALE = 0.6065306597126334

LANES = 128
TILE = 256
HALO = 8
CHUNK = 64
SCAN_BLOCK = 256
GATHER_WIN = 64
MOE_FFN_SAMPLES = 4
SCATTER_WIN = 64
ROW_SAMPLES = 2
VMEM_LIMIT = 56 * 1024 * 1024


def _cparams(sem):
    return pltpu.CompilerParams(dimension_semantics=sem, vmem_limit_bytes=VMEM_LIMIT)


def _dot(a, b):
    return jnp.dot(a.astype(BF16), b.astype(BF16), preferred_element_type=F32)


def _dot_nt(a, b):
    return lax.dot_general(a.astype(BF16), b.astype(BF16), (((1,), (1,)), ((), ())),
                           preferred_element_type=F32)


def _dot_tn(a, b):
    return lax.dot_general(a.astype(BF16), b.astype(BF16), (((0,), (0,)), ((), ())),
                           preferred_element_type=F32)


def _split_dot(x, w, parts):
    acc = None
    rem = x
    for _ in range(parts):
        hi = rem.astype(BF16)
        term = jnp.dot(hi, w, preferred_element_type=F32)
        acc = term if acc is None else acc + term
        rem = rem - hi.astype(F32)
    return acc


def _segsum(x, seg_r, seg_b):
    return _split_dot(_split_dot(x, seg_r, 2), seg_b, 3)


def _rms_rows(x, g):
    ms = jnp.mean(x * x, axis=-1, keepdims=True)
    return x * lax.rsqrt(ms + NORM_EPS) * g


def _ada_rows(x, g, shift, scale):
    return _rms_rows(x, g) * (1.0 + scale) + shift


def _sigmoid(x):
    return 1.0 / (1.0 + jnp.exp(-x))


def _shift_rows(h, prev_row, next_row):
    n = h.shape[0]
    rows = lax.broadcasted_iota(jnp.int32, h.shape, 0)
    down = jnp.where(rows == 0, prev_row, pltpu.roll(h, 1, 0))
    up = jnp.where(rows == n - 1, next_row, pltpu.roll(h, n - 1, 0))
    return down, up


def _ada_mod_kernel(c_ref, w_ref, b_ref, o_ref):
    c = c_ref[...]
    o_ref[...] = _dot(c * _sigmoid(c), w_ref[...]) + b_ref[...]


def _ada_mod(cc, ada_w, ada_b):
    depth, d, n6 = ada_w.shape
    rows = cc.shape[0]
    nb = 1536
    return pl.pallas_call(
        _ada_mod_kernel,
        out_shape=jax.ShapeDtypeStruct((depth, rows, n6), F32),
        grid=(depth, n6 // nb),
        in_specs=[
            pl.BlockSpec((rows, d), lambda l, j: (0, 0)),
            pl.BlockSpec((None, d, nb), lambda l, j: (l, 0, j)),
            pl.BlockSpec((None, 1, nb), lambda l, j: (l, 0, j)),
        ],
        out_specs=pl.BlockSpec((None, rows, nb), lambda l, j: (l, 0, j)),
        compiler_params=_cparams(("arbitrary", "arbitrary")),
        name="ada_mod",
    )(cc, ada_w, ada_b.reshape(depth, 1, n6))


def _mod_spec(d):
    return pl.BlockSpec((None, None, 6, d), lambda b, t: (b, jnp.minimum(t, 1), 0, 0))


def _stream_tile(ctx_ref, x_ref):
    return jnp.where(pl.program_id(1) == 0, ctx_ref[...], x_ref[...])


def _stream_specs(d):
    return [pl.BlockSpec((None, TILE, d), lambda i, j: (i, 0, 0)),
            pl.BlockSpec((None, TILE, d), lambda i, j: (i, jnp.maximum(j - 1, 0), 0))]


def _in_proj_kernel(ctx_ref, x_ref, mod_ref, g_ref, w_ref, qg_ref, kg_ref, cos_ref, sin_ref,
                    qkv_ref, conv_ref, *, n_qkv, n_rot_heads, n_q_heads):
    ns = x_ref.shape[0]
    rows = [slice(s * TILE, (s + 1) * TILE) for s in range(ns)]
    h = jnp.concatenate(
        [_ada_rows(_stream_tile(ctx_ref.at[s], x_ref.at[s]), g_ref[...], mod_ref[s, 0:1, :], mod_ref[s, 1:2, :])
         for s in range(ns)], axis=0).astype(BF16)
    nb = 512
    n_all = w_ref.shape[1]
    cos2 = jnp.concatenate([cos_ref[...]] * ns, axis=0)
    sin2 = jnp.concatenate([sin_ref[...]] * ns, axis=0)
    for j in range(n_all // nb):
        p = jnp.dot(h, w_ref[:, j * nb:(j + 1) * nb], preferred_element_type=F32)
        if j * nb >= n_qkv:
            for s in range(ns):
                conv_ref[s, :, j * nb - n_qkv:(j + 1) * nb - n_qkv] = p[rows[s]]
            continue
        for i in range(nb // HEAD_DIM):
            hd = j * (nb // HEAD_DIM) + i
            t = p[:, i * HEAD_DIM:(i + 1) * HEAD_DIM]
            if hd < n_rot_heads:
                t = _rms_rows(t, qg_ref[...] if hd < n_q_heads else kg_ref[...])
                t = t * cos2 + pltpu.roll(t, HEAD_DIM // 2, 1) * sin2
            if hd < n_q_heads:
                t = t * ((HEAD_DIM ** -0.5) * LOG2_E)
            t = t.astype(BF16)
            for s in range(ns):
                qkv_ref[s, :, hd * HEAD_DIM:(hd + 1) * HEAD_DIM] = t[rows[s]]


def _in_proj(ctx, x, modsel, g, w_in, q_g, k_g, cos2, sin2):
    b, n, d = x.shape
    t = ctx.shape[1] + n
    n_all = w_in.shape[1]
    n_qkv = (N_Q_HEADS + 2 * N_KV_HEADS) * HEAD_DIM
    ns = ROW_SAMPLES if b % ROW_SAMPLES == 0 else 1
    kern = functools.partial(_in_proj_kernel, n_qkv=n_qkv, n_rot_heads=N_Q_HEADS + N_KV_HEADS,
                             n_q_heads=N_Q_HEADS)
    return pl.pallas_call(
        kern,
        out_shape=(jax.ShapeDtypeStruct((b, t, n_qkv), BF16),
                   jax.ShapeDtypeStruct((b, t, n_all - n_qkv), F32)),
        grid=(b // ns, t // TILE),
        in_specs=[
            pl.BlockSpec((ns, TILE, d), lambda i, j: (i, 0, 0)),
            pl.BlockSpec((ns, TILE, d), lambda i, j: (i, jnp.maximum(j - 1, 0), 0)),
            pl.BlockSpec((ns, None, 6, d), lambda i, j: (i, jnp.minimum(j, 1), 0, 0)),
            pl.BlockSpec((1, d), lambda i, j: (0, 0)),
            pl.BlockSpec((d, n_all), lambda i, j: (0, 0)),
            pl.BlockSpec((1, HEAD_DIM), lambda i, j: (0, 0)),
            pl.BlockSpec((1, HEAD_DIM), lambda i, j: (0, 0)),
            pl.BlockSpec((TILE, HEAD_DIM), lambda i, j: (j, 0)),
            pl.BlockSpec((TILE, HEAD_DIM), lambda i, j: (j, 0)),
        ],
        out_specs=(pl.BlockSpec((ns, TILE, n_qkv), lambda i, j: (i, j, 0)),
                   pl.BlockSpec((ns, TILE, n_all - n_qkv), lambda i, j: (i, j, 0))),
        compiler_params=_cparams(("arbitrary", "arbitrary")),
        name="in_proj",
    )(ctx, x, modsel, g, w_in, q_g, k_g, cos2, sin2)


def _attn_kernel(q_ref, k_ref, v_ref, o_ref, *, n_ctx):
    n_heads = q_ref.shape[1] // HEAD_DIM
    heads = [slice(g * HEAD_DIM, (g + 1) * HEAD_DIM) for g in range(n_heads)]
    group = [slice((g // Q_PER_KV) * HEAD_DIM, (g // Q_PER_KV + 1) * HEAD_DIM) for g in range(n_heads)]

    def attend(nk):
        s = _dot_nt(q_ref[:, heads[0]], k_ref[0:nk, group[0]])
        for g in range(n_heads):
            s_next = _dot_nt(q_ref[:, heads[g + 1]], k_ref[0:nk, group[g + 1]]) if g + 1 < n_heads else None
            p = jnp.exp2(s - jnp.max(s, axis=-1, keepdims=True))
            l = jnp.sum(p, axis=-1, keepdims=True)
            o = jnp.dot(p.astype(BF16), v_ref[0:nk, group[g]], preferred_element_type=F32) / l
            o_ref[:, heads[g]] = o.astype(BF16)
            s = s_next

    is_ctx = pl.program_id(1) == 0

    @pl.when(is_ctx)
    def _():
        attend(n_ctx)

    @pl.when(jnp.logical_not(is_ctx))
    def _():
        attend(k_ref.shape[0])


def _attention(qkv, n_ctx):
    b, t, _ = qkv.shape
    q_w, kv_w = N_Q_HEADS * HEAD_DIM, N_KV_HEADS * HEAD_DIM
    assert q_w % kv_w == 0
    kern = functools.partial(_attn_kernel, n_ctx=n_ctx)
    return pl.pallas_call(
        kern,
        out_shape=jax.ShapeDtypeStruct((b, t, N_Q_HEADS * HEAD_DIM), BF16),
        grid=(b, t // TILE),
        in_specs=[
            pl.BlockSpec((None, TILE, q_w), lambda i, j: (i, j, 0)),
            pl.BlockSpec((None, t, kv_w), lambda i, j: (i, 0, q_w // kv_w)),
            pl.BlockSpec((None, t, kv_w), lambda i, j: (i, 0, q_w // kv_w + 1)),
        ],
        out_specs=pl.BlockSpec((None, TILE, q_w), lambda i, j: (i, j, 0)),
        compiler_params=_cparams(("arbitrary",) * 2),
        name="attention",
    )(qkv, qkv, qkv)


def _residual_and_router(x, y, mod_ref, g2_ref, router_ref, x1_ref, h2_ref, aff_ref):
    x1 = x + mod_ref[2:3, :] * y
    x1_ref[...] = x1
    h2 = _ada_rows(x1, g2_ref[...], mod_ref[3:4, :], mod_ref[4:5, :]).astype(BF16)
    h2_ref[...] = h2
    logits = _dot_nt(router_ref[...], h2)
    e = jnp.exp(logits - jnp.max(logits, axis=0, keepdims=True))
    aff_ref[...] = e / jnp.sum(e, axis=0, keepdims=True)


def _mix_out_kernel(oa_ref, gb_ref, gc_ref, xc_ref, gcp_ref, xcp_ref, gcn_ref, xcn_ref,
                    ctx_ref, x_ref, mod_ref, wout_ref, cw_ref, g2_ref, router_ref,
                    x1_ref, h2_ref, aff_ref, *, n_tiles):
    t = pl.program_id(1)
    has_prev = (t >= 2).astype(F32)
    has_next = jnp.logical_and(t >= 1, t <= n_tiles - 2).astype(F32)
    u = gc_ref[...] * xc_ref[...]
    u_prev = gcp_ref[HALO - 1:HALO, :] * xcp_ref[HALO - 1:HALO, :] * has_prev
    u_next = gcn_ref[0:1, :] * xcn_ref[0:1, :] * has_next
    down, up = _shift_rows(u, u_prev, u_next)
    o_conv = gb_ref[...] * (down * cw_ref[0:1, :] + u * cw_ref[1:2, :] + up * cw_ref[2:3, :])
    d_attn = oa_ref.shape[1]
    y = _dot(oa_ref[...], wout_ref[0:d_attn, :]) + _dot(o_conv, wout_ref[d_attn:, :])
    _residual_and_router(_stream_tile(ctx_ref, x_ref), y, mod_ref, g2_ref, router_ref, x1_ref, h2_ref,
                         aff_ref)


def _mix_out(o_attn, conv, ctx, x, modsel, w_out, conv_w, g2, router_t):
    b, t, d = o_attn.shape
    n_tiles = t // TILE
    hb = TILE // HALO
    last = t // HALO - 1
    prev_map = lambda col: (lambda i, j: (i, jnp.maximum(j * hb - 1, 0), col))
    next_map = lambda col: (lambda i, j: (i, jnp.minimum((j + 1) * hb, last), col))
    tile_map = lambda col: (lambda i, j: (i, j, col))
    n_e = router_t.shape[0]
    kern = functools.partial(_mix_out_kernel, n_tiles=n_tiles)
    return pl.pallas_call(
        kern,
        out_shape=(jax.ShapeDtypeStruct((b, t, d), F32),
                   jax.ShapeDtypeStruct((b, t, d), BF16),
                   jax.ShapeDtypeStruct((b, n_e, t), F32)),
        grid=(b, n_tiles),
        in_specs=[
            pl.BlockSpec((None, TILE, d), tile_map(0)),
            pl.BlockSpec((None, TILE, d), tile_map(0)),
            pl.BlockSpec((None, TILE, d), tile_map(1)),
            pl.BlockSpec((None, TILE, d), tile_map(2)),
            pl.BlockSpec((None, HALO, d), prev_map(1)),
            pl.BlockSpec((None, HALO, d), prev_map(2)),
            pl.BlockSpec((None, HALO, d), next_map(1)),
            pl.BlockSpec((None, HALO, d), next_map(2)),
        ] + _stream_specs(d) + [
            _mod_spec(d),
            pl.BlockSpec(w_out.shape, lambda i, j: (0, 0)),
            pl.BlockSpec(conv_w.shape, lambda i, j: (0, 0)),
            pl.BlockSpec((1, d), lambda i, j: (0, 0)),
            pl.BlockSpec((n_e, d), lambda i, j: (0, 0)),
        ],
        out_specs=(pl.BlockSpec((None, TILE, d), tile_map(0)),
                   pl.BlockSpec((None, TILE, d), tile_map(0)),
                   pl.BlockSpec((None, n_e, TILE), lambda i, j: (i, 0, j))),
        compiler_params=_cparams(("arbitrary", "arbitrary")),
        name="mix_out",
    )(o_attn, conv, conv, conv, conv, conv, conv, conv, ctx, x, modsel, w_out, conv_w, g2, router_t)


def _onehot_rows(slot_row, first, n):
    rel = slot_row - (first if isinstance(first, int) else first.astype(F32))
    rows = lax.broadcasted_iota(jnp.int32, (n, slot_row.shape[1]), 0).astype(F32)
    return jnp.where(rel == rows, 1.0, 0.0).astype(BF16)


def _slot_windows(bnd_ref, sample, tile, n_e, slot0, cap, win, group):
    starts, fits = [], []
    for e in range(n_e):
        lo = bnd_ref[sample, e, tile]
        hi = bnd_ref[sample, n_e + e, tile]
        w0 = jnp.minimum((lo >> 4) << 4, slot0 + cap - win)
        ok = hi <= w0 + win
        if e % group == 0:
            fits.append(ok)
        else:
            fits[-1] = jnp.logical_and(fits[-1], ok)
        starts.append(pl.multiple_of(w0, 16))
    return starts, fits


def _moe_gather_kernel(bnd_ref, slot_ref, h_ref, x_ref, *, sets):
    n_e = x_ref.shape[0]
    sample = pl.program_id(0)
    tile = pl.program_id(1)

    @pl.when(tile == 0)
    def _():
        x_ref[...] = jnp.zeros_like(x_ref)

    def onehot(e, first, n):
        return _onehot_rows(slot_ref[e:e + 1, :], first, n)

    for start, length, slot0, cap in sets:
        first_tile, last_tile = start // TILE, (start + length) // TILE
        win = min(GATHER_WIN, cap)

        @pl.when(jnp.logical_and(tile >= first_tile, tile < last_tile))
        def _():
            starts, (fits,) = _slot_windows(bnd_ref, sample, tile, n_e, slot0, cap, win, group=n_e)

            @pl.when(fits)
            def _():
                oh = jnp.concatenate([onehot(e, starts[e], win) for e in range(n_e)], axis=0)
                part = jnp.dot(oh, h_ref[...], preferred_element_type=F32).astype(BF16)
                for e in range(n_e):
                    x_ref[e, pl.ds(starts[e], win), :] += part[e * win:(e + 1) * win]

            @pl.when(jnp.logical_not(fits))
            def _():
                for e in range(n_e):
                    x_ref[e, slot0:slot0 + cap, :] += jnp.dot(
                        onehot(e, slot0, cap), h_ref[...], preferred_element_type=F32).astype(BF16)


def _moe_gather(slots, bounds, h2, sets, n_slots):
    b, n_e, t = slots.shape
    d = h2.shape[2]
    grid_spec = pltpu.PrefetchScalarGridSpec(
        num_scalar_prefetch=1,
        grid=(b, t // TILE),
        in_specs=[
            pl.BlockSpec((None, n_e, TILE), lambda i, j, bnd: (i, 0, j)),
            pl.BlockSpec((None, TILE, d), lambda i, j, bnd: (i, j, 0)),
        ],
        out_specs=pl.BlockSpec((None, n_e, n_slots, d), lambda i, j, bnd: (i, 0, 0, 0)),
    )
    return pl.pallas_call(
        functools.partial(_moe_gather_kernel, sets=sets),
        out_shape=jax.ShapeDtypeStruct((b, n_e, n_slots, d), BF16),
        grid_spec=grid_spec,
        compiler_params=_cparams(("arbitrary", "arbitrary")),
        name="moe_gather",
    )(bounds, slots, h2)


def _moe_ffn_kernel(slot_ref, aff_ref, x_ref, wg_ref, wu_ref, wd_ref, o_ref, wg_s, wu_s, wd_s):
    @pl.when(pl.program_id(1) == 0)
    def _():
        wg_s[...] = wg_ref[...].astype(BF16)
        wu_s[...] = wu_ref[...].astype(BF16)
        wd_s[...] = wd_ref[...].astype(BF16)

    nb, r, d = x_ref.shape
    rows = lax.broadcasted_iota(jnp.int32, (r, slot_ref.shape[2]), 0).astype(F32)
    a_row = jnp.concatenate(
        [jnp.sum(jnp.where(slot_ref[i] == rows, aff_ref[i], 0.0), axis=-1, keepdims=True) for i in range(nb)],
        axis=0)
    x = x_ref[...].reshape(nb * r, d)
    gate = jnp.dot(x, wg_s[...], preferred_element_type=F32)
    up = jnp.dot(x, wu_s[...], preferred_element_type=F32)
    act = (gate * _sigmoid(gate) * up).astype(BF16)
    out = jnp.dot(act, wd_s[...], preferred_element_type=F32) * a_row
    o_ref[...] = out.astype(BF16).reshape(nb, r, d)


def _moe_ffn(slots, aff_t, xs, w_gate, w_up, w_down, layer):
    b, n_e, t = slots.shape
    _, _, r, d = xs.shape
    f = w_gate.shape[3]
    nb = MOE_FFN_SAMPLES if b % MOE_FFN_SAMPLES == 0 else 1
    row = pl.BlockSpec((nb, None, 1, t), lambda e, i: (i, e, 0, 0))
    tok = pl.BlockSpec((nb, None, r, d), lambda e, i: (i, e, 0, 0))
    return pl.pallas_call(
        _moe_ffn_kernel,
        out_shape=jax.ShapeDtypeStruct((b, n_e, r, d), BF16),
        grid=(n_e, b // nb),
        in_specs=[
            row, row, tok,
            pl.BlockSpec((None, None, d, f), lambda e, i: (layer, e, 0, 0)),
            pl.BlockSpec((None, None, d, f), lambda e, i: (layer, e, 0, 0)),
            pl.BlockSpec((None, None, f, d), lambda e, i: (layer, e, 0, 0)),
        ],
        out_specs=tok,
        scratch_shapes=[pltpu.VMEM((d, f), BF16), pltpu.VMEM((d, f), BF16), pltpu.VMEM((f, d), BF16)],
        compiler_params=_cparams(("arbitrary", "arbitrary")),
        name="moe_ffn",
    )(slots.reshape(b, n_e, 1, t), aff_t.reshape(b, n_e, 1, t), xs, w_gate, w_up, w_down)


def _prefix_count(mask, tri):
    return _prefix_count_blocks(mask, tri)[0]


def _prefix_count_blocks(mask, tri):
    blk = tri.shape[0]
    carry = jnp.zeros((mask.shape[0], 1), F32)
    parts, carries = [], [carry]
    for j in range(mask.shape[1] // blk):
        m = mask[:, j * blk:(j + 1) * blk]
        parts.append(jnp.dot(m.astype(BF16), tri, preferred_element_type=F32) + carry)
        carry = carry + jnp.sum(m, axis=-1, keepdims=True)
        carries.append(carry)
    return jnp.concatenate(parts, axis=-1), carries


def _route_kernel(aff_ref, slot_ref, bound_ref, *, sets):
    blk = TILE
    n_e = aff_ref.shape[0]
    tri = jnp.where(lax.broadcasted_iota(jnp.int32, (blk, blk), 0)
                    <= lax.broadcasted_iota(jnp.int32, (blk, blk), 1), 1.0, 0.0).astype(BF16)
    lane = lax.broadcasted_iota(jnp.int32, (n_e, LANES), 1)
    lo = jnp.zeros((n_e, LANES), F32)
    hi = jnp.zeros((n_e, LANES), F32)
    for start, length, slot0, cap in sets:
        cap = float(cap)
        x = aff_ref[:, start:start + length]
        bits = pltpu.bitcast(x, jnp.int32)
        thr = jnp.zeros((bits.shape[0], 1), jnp.int32)
        for bit in range(30, -1, -1):
            cand = thr | (1 << bit)
            cnt = jnp.sum(jnp.where(bits >= cand, 1.0, 0.0), axis=-1, keepdims=True)
            thr = jnp.where(cnt >= cap, cand, thr)
        pivot = jnp.max(jnp.where(bits <= thr, x, -1.0), axis=-1, keepdims=True)

        def counts(p):
            return (jnp.sum(jnp.where(x > p, 1.0, 0.0), axis=-1, keepdims=True),
                    jnp.sum(jnp.where(x >= p, 1.0, 0.0), axis=-1, keepdims=True))

        def misplaced(p):
            n_gt, n_ge = counts(p)
            return jnp.max(jnp.where(jnp.logical_or(n_gt >= cap, n_ge < cap), 1.0, 0.0)) > 0.0

        def walk(p):
            n_gt, n_ge = counts(p)
            higher = jnp.min(jnp.where(x > p, x, jnp.inf), axis=-1, keepdims=True)
            lower = jnp.max(jnp.where(x < p, x, -1.0), axis=-1, keepdims=True)
            return jnp.where(n_gt >= cap, higher, jnp.where(n_ge < cap, lower, p))

        pivot = lax.while_loop(misplaced, walk, pivot)
        above = jnp.where(x > pivot, 1.0, 0.0)
        tied = jnp.where(x == pivot, 1.0, 0.0)
        need = cap - jnp.sum(above, axis=-1, keepdims=True)
        chosen = above + jnp.where(_prefix_count(tied, tri) <= need, tied, 0.0)
        count, before = _prefix_count_blocks(chosen, tri)
        slot_ref[:, start:start + length] = jnp.where(chosen > 0.0, count - 1.0 + slot0, -1.0)
        for j in range(length // blk):
            here = lane == (start // blk + j)
            lo = jnp.where(here, before[j] + slot0, lo)
            hi = jnp.where(here, before[j + 1] + slot0, hi)
    bound_ref[0:n_e, :] = lo
    bound_ref[n_e:, :] = hi


def _route(aff_t, sets):
    b, n_e, t = aff_t.shape
    assert t // TILE <= LANES
    kern = functools.partial(_route_kernel, sets=tuple(sets))
    spec = pl.BlockSpec((None, n_e, t), lambda i: (i, 0, 0))
    bspec = pl.BlockSpec((None, 2 * n_e, LANES), lambda i: (i, 0, 0))
    return pl.pallas_call(
        kern,
        out_shape=(jax.ShapeDtypeStruct((b, n_e, t), F32),
                   jax.ShapeDtypeStruct((b, 2 * n_e, LANES), F32)),
        grid=(b,),
        in_specs=[spec],
        out_specs=(spec, bspec),
        compiler_params=_cparams(("arbitrary",)),
        name="moe_route",
    )(aff_t)


def _combine_kernel(bnd_ref, slot_ref, o_ref, x_ref, mod_ref, g_ref, out_ref, *, sets, final_norm):
    n_e = o_ref.shape[0]
    sample = pl.program_id(0)
    tile = pl.program_id(1)
    tn = lambda a, w: lax.dot_general(a, w, (((0,), (0,)), ((), ())), preferred_element_type=F32)

    def onehot(e, first, n):
        return _onehot_rows(slot_ref[e:e + 1, :], first, n)

    def scatter_full(experts, slot0, cap):
        y = None
        for e in experts:
            term = tn(onehot(e, slot0, cap), o_ref[e, slot0:slot0 + cap, :])
            y = term if y is None else y + term
        return y

    for start, length, slot0, cap in sets:
        first_tile, last_tile = start // TILE, (start + length) // TILE

        def finish(y):
            x2 = x_ref[...] + mod_ref[5:6, :] * y
            out_ref[...] = _rms_rows(x2, g_ref[...]) if final_norm else x2

        @pl.when(jnp.logical_and(tile >= first_tile, tile < last_tile))
        def _():
            win = min(SCATTER_WIN, cap)
            starts, (fits,) = _slot_windows(bnd_ref, sample, tile, n_e, slot0, cap, win, group=n_e)

            @pl.when(fits)
            def _():
                oh = jnp.concatenate([onehot(e, starts[e], win) for e in range(n_e)], axis=0)
                rows = jnp.concatenate([o_ref[e, pl.ds(starts[e], win), :] for e in range(n_e)], axis=0)
                finish(tn(oh, rows))

            @pl.when(jnp.logical_not(fits))
            def _():
                finish(scatter_full(range(n_e), slot0, cap))


def _combine(slots, bounds, outs, x1, modsel, g, sets, *, tile_offset, final_norm):
    b, t, d = x1.shape
    _, n_e, r, _ = outs.shape
    assert all(s[2] % 16 == 0 and s[3] % 16 == 0 for s in sets)
    kern = functools.partial(_combine_kernel, sets=sets, final_norm=final_norm)
    grid_spec = pltpu.PrefetchScalarGridSpec(
        num_scalar_prefetch=1,
        grid=(b, t // TILE),
        in_specs=[
            pl.BlockSpec((None, n_e, TILE), lambda i, j, bnd: (i, 0, j)),
            pl.BlockSpec((None, n_e, r, d), lambda i, j, bnd: (i, 0, 0, 0)),
            pl.BlockSpec((None, TILE, d), lambda i, j, bnd: (i, j, 0)),
            pl.BlockSpec((None, None, 6, d), lambda i, j, bnd: (i, jnp.minimum(j + tile_offset, 1), 0, 0)),
            pl.BlockSpec((1, d), lambda i, j, bnd: (0, 0)),
        ],
        out_specs=pl.BlockSpec((None, TILE, d), lambda i, j, bnd: (i, j, 0)),
    )
    return pl.pallas_call(
        kern,
        out_shape=jax.ShapeDtypeStruct((b, t, d), F32),
        grid_spec=grid_spec,
        compiler_params=_cparams(("arbitrary", "arbitrary")),
        name="moe_combine",
    )(bounds, slots, outs, x1, modsel, g)


def _rwkv_feat_kernel(x_ref, xp_ref, xn_ref, mod_ref, g_ref, mu_ref, wr_ref, wk_ref, wv_ref,
                      w1_ref, a1_ref, g1_ref, w2_ref, a2_ref, g2_ref, w0_ref, a0_ref,
                      kk_ref, ka_ref, segr_ref, segb_ref,
                      r_out, v_out, kkn_out, g_out, lw_out, kd_out, bb_out, *, n_tiles):
    t = pl.program_id(1)
    has_prev = (t >= 2).astype(F32)
    has_next = jnp.logical_and(t >= 1, t <= n_tiles - 2).astype(F32)
    g = g_ref[...]
    shift = mod_ref[0:1, :]
    scale = mod_ref[1:2, :]
    h = _ada_rows(x_ref[...], g, shift, scale)
    h_prev = _ada_rows(xp_ref[HALO - 1:HALO, :], g, shift, scale) * has_prev
    h_next = _ada_rows(xn_ref[0:1, :], g, shift, scale) * has_next
    down, up = _shift_rows(h, h_prev, h_next)
    xx = 0.5 * (down + up) - h
    d = h.shape[1]

    r_out[...] = _dot(h + xx * mu_ref[0:1, :], wr_ref[...])
    k = _dot(h + xx * mu_ref[2:3, :], wk_ref[...])
    v_out[...] = _dot(h + xx * mu_ref[3:4, :], wv_ref[...])
    gate = _sigmoid(_dot(h + xx * mu_ref[5:6, :], g1_ref[...]))
    g_out[...] = _dot(gate, g2_ref[...])

    kkp = k * kk_ref[...]
    ss = _segsum(kkp * kkp, segr_ref[...], segb_ref[...])
    kkn = kkp * lax.rsqrt(jnp.maximum(ss, 1e-24))
    kkn_out[...] = kkn

    lora_w = _dot(jnp.tanh(_dot(h + xx * mu_ref[1:2, :], w1_ref[...])), w2_ref[...])
    lora_a = _dot(_dot(h + xx * mu_ref[4:5, :], a1_ref[...]), a2_ref[...])
    for di in range(2):
        sl = slice(di * d, (di + 1) * d)
        lw_out[di] = -DECAY_SCALE * _sigmoid(w0_ref[di:di + 1, :] + lora_w[:, sl])
        a = _sigmoid(a0_ref[di:di + 1, :] + lora_a[:, sl])
        kd_out[di] = k * (1.0 + (a - 1.0) * ka_ref[...])
        bb_out[di] = a * kkn


def _rwkv_feat(xs, modsel, g, p):
    b, t, d = xs.shape
    n_tiles = t // TILE
    hb = TILE // HALO
    last = t // HALO - 1
    full = lambda a: pl.BlockSpec(a.shape, lambda i, j: (0,) * a.ndim)
    weights = [p["mu"], p["w_r"], p["w_k"], p["w_v"], p["w1"], p["a1"], p["g1"], p["w2"], p["a2"],
               p["g2"], p["w0"], p["a0"], p["k_k"], p["k_a"], p["seg_r"], p["seg_b"]]
    tile_spec = pl.BlockSpec((None, TILE, d), lambda i, j: (i, j, 0))
    dir_spec = pl.BlockSpec((2, None, TILE, d), lambda i, j: (0, i, j, 0))
    kern = functools.partial(_rwkv_feat_kernel, n_tiles=n_tiles)
    one = jax.ShapeDtypeStruct((b, t, d), F32)
    two = jax.ShapeDtypeStruct((2, b, t, d), F32)
    return pl.pallas_call(
        kern,
        out_shape=(one, one, one, one, two, two, two),
        grid=(b, n_tiles),
        in_specs=[
            tile_spec,
            pl.BlockSpec((None, HALO, d), lambda i, j: (i, jnp.maximum(j * hb - 1, 0), 0)),
            pl.BlockSpec((None, HALO, d), lambda i, j: (i, jnp.minimum((j + 1) * hb, last), 0)),
            _mod_spec(d),
            pl.BlockSpec((1, d), lambda i, j: (0, 0)),
        ] + [full(a) for a in weights],
        out_specs=(tile_spec, tile_spec, tile_spec, tile_spec, dir_spec, dir_spec, dir_spec),
        compiler_params=_cparams(("arbitrary", "arbitrary")),
        name="rwkv_feat",
    )(xs, xs, xs, modsel, g, *weights)


def _scan_kernel(lw_ref, kd_ref, bb_ref, kk_ref, r_ref, v_ref, *rest, rev):
    add_ref = rest[0] if len(rest) == 3 else None
    y_ref, s_ref = rest[-2:]

    @pl.when(pl.program_id(1) == 0)
    def _():
        s_ref[...] = jnp.zeros_like(s_ref)

    L = CHUNK
    ti = lax.broadcasted_iota(jnp.int32, (L, LANES), 0)
    ii = lax.broadcasted_iota(jnp.int32, (L, LANES), 1) & (L - 1)
    before = (ii > ti) if rev else (ii < ti)
    before_eq = (ii >= ti) if rev else (ii <= ti)
    t2 = lax.broadcasted_iota(jnp.int32, (L, L), 0)
    i2 = lax.broadcasted_iota(jnp.int32, (L, L), 1)
    tri = jnp.where((i2 >= t2) if rev else (i2 <= t2), 1.0, 0.0).astype(BF16)
    r128 = lax.broadcasted_iota(jnp.int32, (LANES, LANES), 0)
    c128 = lax.broadcasted_iota(jnp.int32, (LANES, LANES), 1)
    head_eq = (r128 >> 6) == (c128 >> 6)
    eye = r128 == c128

    def same_block(log2s):
        return (ti >> log2s) == (ii >> log2s)

    def bdiag(y):
        yb = y.astype(BF16)
        return jnp.where(head_eq, jnp.concatenate([yb, yb], axis=0), jnp.zeros((), BF16))

    def hmul(x, y):
        return jnp.dot(x.astype(BF16), bdiag(y), preferred_element_type=F32)

    def hmul2(x, y, z):
        return jnp.dot(x.astype(BF16), jnp.concatenate([bdiag(y), bdiag(z)], axis=1),
                       preferred_element_type=F32)

    n_sub = lw_ref.shape[0] // L
    n_pair = lw_ref.shape[1] // LANES
    sub_order = list(range(n_sub))[::-1] if rev else list(range(n_sub))
    chains = [(slice(s * L, (s + 1) * L), slice(p * LANES, (p + 1) * LANES))
              for s in sub_order for p in range(n_pair)]

    def each(f, *cols):
        return [f(*args) for args in zip(*cols)]

    lam_sub = {s: _split_dot_left(tri, lw_ref[s * L:(s + 1) * L, :]) for s in sub_order}
    lw = [lw_ref[rs_, cs] for rs_, cs in chains]
    lam = [lam_sub[rs_.start // L][:, cs] for rs_, cs in chains]
    kk = [kk_ref[rs_, cs] for rs_, cs in chains]
    bb = [bb_ref[rs_, cs] for rs_, cs in chains]
    kd = [kd_ref[rs_, cs] for rs_, cs in chains]
    r = [r_ref[rs_, cs] for rs_, cs in chains]
    v = [v_ref[rs_, cs] for rs_, cs in chains]

    tot = each(lambda a: jnp.sum(a, axis=0, keepdims=True), lw)
    rho = each(lambda a: a[L // 2:L // 2 + 1, :], lam)
    ks = each(lambda a, b, c: a * jnp.exp(b - c), kk, lam, lw)
    rs = each(lambda a, b: a * jnp.exp(b), r, lam)
    e_neg = each(lambda a: jnp.exp(-a), rho)
    e_inv = each(lambda a, b: jnp.exp(a - b), rho, lam)
    e_end = each(lambda a, b: jnp.exp(a - b), tot, lam)

    lhs = each(lambda a, b, c: jnp.concatenate([a * c, b * c], axis=0).astype(BF16), ks, rs, e_neg)
    nt = lambda a, w: lax.dot_general(a, w, (((1,), (1,)), ((), ())), preferred_element_type=F32)
    g_bk = each(lambda a, b, c, d: nt(a, jnp.concatenate([bdiag(b * d), bdiag(c * d)], axis=0)),
                lhs, bb, kd, e_inv)
    a_b = each(lambda a: jnp.where(before, a[0:L, 0:LANES], 0.0), g_bk)
    m_b = each(lambda a: jnp.where(before_eq, a[L:, 0:LANES], 0.0), g_bk)
    mask_am = jnp.concatenate([before, before_eq], axis=0)
    amv = each(lambda a, b: hmul(jnp.where(mask_am, a[:, LANES:], 0.0), b), g_bk, v)
    akv = each(lambda a: a[0:L], amv)
    mkv = each(lambda a: a[L:], amv)

    ident = jnp.where(ti == ii, 1.0, 0.0)
    blk8 = same_block(3)
    n1 = each(lambda a: jnp.where(blk8, -a, 0.0), a_b)
    n2 = each(hmul, n1, n1)
    n34 = each(lambda a, b: hmul(jnp.concatenate([a, b], axis=0), b), n1, n2)
    n3 = each(lambda a: a[0:L], n34)
    n4 = each(lambda a: a[L:], n34)
    x = each(lambda a, b, c: ident + a + b + c, n1, n2, n3)
    tinv = each(lambda a, b: a + hmul(a, b), x, n4)
    for log2s in (3, 4, 5):
        off_mask = jnp.logical_and(same_block(log2s + 1), jnp.logical_not(same_block(log2s)))
        off = each(lambda a: jnp.where(off_mask, a, 0.0), a_b)
        corr = each(hmul, off, tinv)
        tinv = each(lambda a, b: a - hmul(a, b), tinv, corr)

    qp = each(lambda a, b, c: -hmul2(a, b, c), tinv, ks, akv)
    q1 = each(lambda a: a[:, 0:LANES], qp)
    p0 = each(lambda a: a[:, LANES:], qp)
    mqp = each(hmul2, m_b, q1, p0)
    q2 = each(lambda a, b: a + b[:, 0:LANES], rs, mqp)
    y0 = each(lambda a, b: a[:, LANES:] + b, mqp, mkv)
    end = each(lambda a, b, c: jnp.concatenate([a * c, b * c], axis=0), bb, kd, e_end)
    zero = jnp.zeros((L, LANES), F32)
    mn = each(lambda a, b, c, d: _dot_tn(a, jnp.concatenate(
        [jnp.concatenate([b, c], axis=1), jnp.concatenate([zero, d], axis=1)], axis=0)),
        end, q1, p0, v)
    m_t = each(lambda a, c: jnp.where(head_eq, a[:, 0:LANES], 0.0) + jnp.where(eye, jnp.exp(c), 0.0),
               mn, tot)
    n0 = each(lambda a: jnp.where(head_eq, a[:, LANES:], 0.0), mn)
    qm = each(lambda a, b: jnp.concatenate([a, b], axis=0).astype(BF16), q2, m_t)

    state = [s_ref[p] for p in range(n_pair)]
    for idx, (rs_, cs) in enumerate(chains):
        p = idx % n_pair
        out = jnp.dot(qm[idx], state[p].astype(BF16), preferred_element_type=F32)
        y = out[0:L] + y0[idx]
        y_ref[rs_, cs] = y if add_ref is None else add_ref[rs_, cs] + y
        state[p] = out[L:] + n0[idx]
    for p in range(n_pair):
        s_ref[p] = state[p]


def _split_dot_left(tri, x):
    acc = None
    rem = x
    for _ in range(2):
        hi = rem.astype(BF16)
        term = jnp.dot(tri, hi, preferred_element_type=F32)
        acc = term if acc is None else acc + term
        rem = rem - hi.astype(F32)
    return acc


def _rwkv_scan(lw, kd, bb, kk, r, v, n_ctx, rev, add=None):
    _, b, t, d = lw.shape
    assert n_ctx % SCAN_BLOCK == 0 and t % SCAN_BLOCK == 0
    nb_ctx = n_ctx // SCAN_BLOCK
    nb = t // SCAN_BLOCK
    di = 1 if rev else 0

    def block(j):
        return jnp.where(j < nb_ctx, nb_ctx - 1 - j, nb + nb_ctx - 1 - j) if rev else j

    dir_spec = pl.BlockSpec((None, None, SCAN_BLOCK, d), lambda i, j: (di, i, block(j), 0))
    one_spec = pl.BlockSpec((None, SCAN_BLOCK, d), lambda i, j: (i, block(j), 0))
    extra = [] if add is None else [add]
    return pl.pallas_call(
        functools.partial(_scan_kernel, rev=rev),
        out_shape=jax.ShapeDtypeStruct((b, t, d), F32),
        grid=(b, nb),
        in_specs=[dir_spec, dir_spec, dir_spec, one_spec, one_spec, one_spec] + ([one_spec] if extra else []),
        out_specs=one_spec,
        scratch_shapes=[pltpu.VMEM((d // LANES, LANES, LANES), F32)],
        compiler_params=_cparams(("arbitrary", "arbitrary")),
        name="rwkv_scan_bwd" if rev else "rwkv_scan_fwd",
    )(lw, kd, bb, kk, r, v, *extra)


def _rwkv_out_kernel(y_ref, r_ref, v_ref, g_ref, kdf_ref, kdb_ref, x_ref, mod_ref,
                     wo_ref, lnw_ref, lnb_ref, rk_ref, segr_ref, segb_ref, g2_ref, router_ref,
                     x1_ref, h2_ref, aff_ref):
    seg_r = segr_ref[...]
    seg_b = segb_ref[...]
    inv_n = 1.0 / RW_N
    y = y_ref[...]
    mean = _segsum(y, seg_r, seg_b) * inv_n
    dev = y - mean
    var = _segsum(dev * dev, seg_r, seg_b) * inv_n
    yn = dev * lax.rsqrt(var + GN_EPS) * lnw_ref[...] + lnb_ref[...]
    r = r_ref[...]
    rk = rk_ref[...]
    bonus = _segsum(r * kdf_ref[...] * rk + r * kdb_ref[...] * rk, seg_r, seg_b) * v_ref[...]
    out = _dot((yn + bonus) * g_ref[...], wo_ref[...])
    _residual_and_router(x_ref[...], out, mod_ref, g2_ref, router_ref, x1_ref, h2_ref, aff_ref)


def _rwkv_out(y, r, v, g, kd, xs, modsel, p, g2, router_t, n_ctx):
    b, t, d = xs.shape
    off = n_ctx // TILE
    n = t - n_ctx
    n_e = router_t.shape[0]
    tile = pl.BlockSpec((None, TILE, d), lambda i, j: (i, j + off, 0))
    dir0 = pl.BlockSpec((None, None, TILE, d), lambda i, j: (0, i, j + off, 0))
    dir1 = pl.BlockSpec((None, None, TILE, d), lambda i, j: (1, i, j + off, 0))
    vec = pl.BlockSpec((1, d), lambda i, j: (0, 0))
    out_tile = pl.BlockSpec((None, TILE, d), lambda i, j: (i, j, 0))
    return pl.pallas_call(
        _rwkv_out_kernel,
        out_shape=(jax.ShapeDtypeStruct((b, n, d), F32),
                   jax.ShapeDtypeStruct((b, n, d), BF16),
                   jax.ShapeDtypeStruct((b, n_e, n), F32)),
        grid=(b, n // TILE),
        in_specs=[tile, tile, tile, tile, dir0, dir1, tile,
                  pl.BlockSpec((None, None, 6, d), lambda i, j: (i, 1, 0, 0)),
                  pl.BlockSpec((d, d), lambda i, j: (0, 0)),
                  vec, vec, vec,
                  pl.BlockSpec((d, LANES), lambda i, j: (0, 0)),
                  pl.BlockSpec((LANES, d), lambda i, j: (0, 0)),
                  vec,
                  pl.BlockSpec((n_e, d), lambda i, j: (0, 0))],
        out_specs=(out_tile, out_tile, pl.BlockSpec((None, n_e, TILE), lambda i, j: (i, 0, j))),
        compiler_params=_cparams(("arbitrary", "arbitrary")),
        name="rwkv_out",
    )(y, r, v, g, kd, kd, xs, modsel, p["w_o"], p["ln_w"], p["ln_b"], p["r_k"], p["seg_r"],
      p["seg_b"], g2, router_t)


def _moe_block(x1, h2, aff_t, token_sets, modsel, g, w_gate, w_up, w_down, layer, *, tile_offset,
               final_norm):
    sets, n_slots = [], 0
    for start, length in token_sets:
        cap = EC_CAPACITY * length // N_EXPERTS
        sets.append((start, length, n_slots, cap))
        n_slots += cap
    sets = tuple(sets)
    slots, bounds = _route(aff_t, sets)
    bounds = bounds[:, :, :aff_t.shape[2] // TILE].astype(jnp.int32)
    xs = _moe_gather(slots, bounds, h2, sets, n_slots)
    outs = _moe_ffn(slots, aff_t, xs, w_gate, w_up, w_down, layer)
    return _combine(slots, bounds, outs, x1, modsel, g, sets, tile_offset=tile_offset,
                    final_norm=final_norm)


def _rope_tables(n_ctx, n):
    rows = n // GRID_W
    row = jnp.broadcast_to(jnp.arange(rows)[:, None], (rows, GRID_W)).reshape(-1).astype(F32)
    col = jnp.broadcast_to(jnp.arange(GRID_W)[None, :], (rows, GRID_W)).reshape(-1).astype(F32)
    pairs = HEAD_DIM // 4
    inv = ROPE_THETA ** (-jnp.arange(pairs, dtype=F32) / pairs)
    ang = jnp.concatenate([row[:, None] * inv, col[:, None] * inv], axis=-1)
    cos, sin = jnp.cos(ang), jnp.sin(ang)
    cos2 = jnp.concatenate([cos, cos], axis=-1)
    sin2 = jnp.concatenate([-sin, sin], axis=-1)
    cos2 = jnp.concatenate([jnp.ones((n_ctx, HEAD_DIM), F32), cos2], axis=0)
    sin2 = jnp.concatenate([jnp.zeros((n_ctx, HEAD_DIM), F32), sin2], axis=0)
    return cos2, sin2


def _block_diag2(a, b):
    z = jnp.zeros_like(a)
    return jnp.concatenate([jnp.concatenate([a, z], axis=1), jnp.concatenate([z, b], axis=1)], axis=0)


def kernel(x, c, ctx, c_ctx, ada_w, ada_b, norm_g, final_g, mix_w_in, mix_w_out, q_norm_g, k_norm_g, conv_w, rw_mu, rw_w_r, rw_w_k, rw_w_v, rw_w_o, rw_w0, rw_w1, rw_w2, rw_a0, rw_a1, rw_a2, rw_g1, rw_g2, rw_k_k, rw_k_a, rw_r_k, rw_ln_w, rw_ln_b, moe_router, moe_w_gate, moe_w_up, moe_w_down):
    b, n, d = x.shape
    n_ctx = ctx.shape[1]
    depth = ada_w.shape[0]
    assert depth == 2 and n_ctx == TILE and n % TILE == 0 and d % LANES == 0

    rows = -(-(b + 1) // HALO) * HALO
    cc = jnp.concatenate([c, c_ctx[None, :], jnp.zeros((rows - b - 1, d), F32)], axis=0)
    mod = _ada_mod(cc, ada_w, ada_b)
    mod_lat = mod[:, :b].reshape(depth, b, 1, 6, d)
    mod_ctx = jnp.broadcast_to(mod[:, b].reshape(depth, 1, 1, 6, d), (depth, b, 1, 6, d))
    modsel = jnp.concatenate([mod_ctx, mod_lat], axis=2)

    router_t = jnp.swapaxes(moe_router, 1, 2).astype(BF16)

    cos2, sin2 = _rope_tables(n_ctx, n)
    qkv, conv = _in_proj(ctx, x, modsel[0], norm_g[0, 0][None], mix_w_in[0].astype(BF16),
                         q_norm_g[0][None], k_norm_g[0][None], cos2, sin2)
    o_attn = _attention(qkv, n_ctx)
    x1, h2, aff_t = _mix_out(o_attn, conv, ctx, x, modsel[0], mix_w_out[0].astype(BF16), conv_w[0],
                             norm_g[0, 1][None], router_t[0])
    xs = _moe_block(x1, h2, aff_t, [(n_ctx, n), (0, n_ctx)], modsel[0], final_g[None],
                    moe_w_gate, moe_w_up, moe_w_down, 0, tile_offset=0, final_norm=False)

    heads = d // RW_N
    assert heads <= LANES
    seg_r = (jnp.arange(d)[:, None] // RW_N == jnp.arange(LANES)[None, :]).astype(BF16)
    p = {
        "mu": rw_mu[0], "w_r": rw_w_r[0].astype(BF16), "w_k": rw_w_k[0].astype(BF16),
        "w_v": rw_w_v[0].astype(BF16), "w_o": rw_w_o[0].astype(BF16),
        "w1": jnp.concatenate([rw_w1[0, 0], rw_w1[0, 1]], axis=1).astype(BF16),
        "a1": jnp.concatenate([rw_a1[0, 0], rw_a1[0, 1]], axis=1).astype(BF16),
        "g1": rw_g1[0].astype(BF16),
        "w2": _block_diag2(rw_w2[0, 0], rw_w2[0, 1]).astype(BF16),
        "a2": _block_diag2(rw_a2[0, 0], rw_a2[0, 1]).astype(BF16),
        "g2": rw_g2[0].astype(BF16),
        "w0": rw_w0[0], "a0": rw_a0[0], "k_k": rw_k_k[0][None], "k_a": rw_k_a[0][None],
        "r_k": rw_r_k[0].reshape(1, d), "ln_w": rw_ln_w[0][None], "ln_b": rw_ln_b[0][None],
        "seg_r": seg_r, "seg_b": seg_r.T,
    }
    r, v, kk, g, lw, kd, bb = _rwkv_feat(xs, modsel[1], norm_g[1, 0][None], p)
    y_fwd = _rwkv_scan(lw, kd, bb, kk, r, v, n_ctx, rev=False)
    y = _rwkv_scan(lw, kd, bb, kk, r, v, n_ctx, rev=True, add=y_fwd)
    x1, h2, aff_t = _rwkv_out(y, r, v, g, kd, xs, modsel[1], p, norm_g[1, 1][None], router_t[1], n_ctx)
    return _moe_block(x1, h2, aff_t, [(0, n)], modsel[1], final_g[None],
                      moe_w_gate, moe_w_up, moe_w_down, 1, tile_offset=1, final_norm=True)
```

```python
import functools

import jax
import jax.numpy as jnp
from jax import lax
from jax.experimental import pallas as pl
from jax.experimental.pallas import tpu as pltpu

F32 = jnp.float32
BF16 = jnp.bfloat16

NORM_EPS = 1e-6
GN_EPS = 64e-5
GRID_W = 64
ROPE_THETA = 10000.0
HEAD_DIM = 128
N_Q_HEADS = 8
N_KV_HEADS = 2
Q_PER_KV = N_Q_HEADS // N_KV_HEADS
RW_N = 64
N_EXPERTS = 16
EC_CAPACITY = 2
LOG2_E = 1.4426950408889634
DECAY_SCALE = 0.6065306597126334

LANES = 128
TILE = 256
HALO = 8
CHUNK = 64
SCAN_BLOCK = 256
GATHER_WIN = 64
MOE_FFN_SAMPLES = 4
SCATTER_WIN = 64
ROW_SAMPLES = 2
VMEM_LIMIT = 56 * 1024 * 1024


def _cparams(sem):
    return pltpu.CompilerParams(dimension_semantics=sem, vmem_limit_bytes=VMEM_LIMIT)


def _dot(a, b):
    return jnp.dot(a.astype(BF16), b.astype(BF16), preferred_element_type=F32)


def _dot_nt(a, b):
    return lax.dot_general(a.astype(BF16), b.astype(BF16), (((1,), (1,)), ((), ())),
                           preferred_element_type=F32)


def _dot_tn(a, b):
    return lax.dot_general(a.astype(BF16), b.astype(BF16), (((0,), (0,)), ((), ())),
                           preferred_element_type=F32)


def _split_dot(x, w, parts):
    acc = None
    rem = x
    for _ in range(parts):
        hi = rem.astype(BF16)
        term = jnp.dot(hi, w, preferred_element_type=F32)
        acc = term if acc is None else acc + term
        rem = rem - hi.astype(F32)
    return acc


def _segsum(x, seg_r, seg_b):
    return _split_dot(_split_dot(x, seg_r, 2), seg_b, 3)


def _rms_rows(x, g):
    ms = jnp.mean(x * x, axis=-1, keepdims=True)
    return x * lax.rsqrt(ms + NORM_EPS) * g


def _ada_rows(x, g, shift, scale):
    return _rms_rows(x, g) * (1.0 + scale) + shift


def _sigmoid(x):
    return 1.0 / (1.0 + jnp.exp(-x))


def _shift_rows(h, prev_row, next_row):
    n = h.shape[0]
    rows = lax.broadcasted_iota(jnp.int32, h.shape, 0)
    down = jnp.where(rows == 0, prev_row, pltpu.roll(h, 1, 0))
    up = jnp.where(rows == n - 1, next_row, pltpu.roll(h, n - 1, 0))
    return down, up


def _ada_mod_kernel(c_ref, w_ref, b_ref, o_ref):
    c = c_ref[...]
    o_ref[...] = _dot(c * _sigmoid(c), w_ref[...]) + b_ref[...]


def _ada_mod(cc, ada_w, ada_b):
    depth, d, n6 = ada_w.shape
    rows = cc.shape[0]
    nb = 1536
    return pl.pallas_call(
        _ada_mod_kernel,
        out_shape=jax.ShapeDtypeStruct((depth, rows, n6), F32),
        grid=(depth, n6 // nb),
        in_specs=[
            pl.BlockSpec((rows, d), lambda l, j: (0, 0)),
            pl.BlockSpec((None, d, nb), lambda l, j: (l, 0, j)),
            pl.BlockSpec((None, 1, nb), lambda l, j: (l, 0, j)),
        ],
        out_specs=pl.BlockSpec((None, rows, nb), lambda l, j: (l, 0, j)),
        compiler_params=_cparams(("arbitrary", "arbitrary")),
        name="ada_mod",
    )(cc, ada_w, ada_b.reshape(depth, 1, n6))


def _mod_spec(d):
    return pl.BlockSpec((None, None, 6, d), lambda b, t: (b, jnp.minimum(t, 1), 0, 0))


def _stream_tile(ctx_ref, x_ref):
    return jnp.where(pl.program_id(1) == 0, ctx_ref[...], x_ref[...])


def _stream_specs(d):
    return [pl.BlockSpec((None, TILE, d), lambda i, j: (i, 0, 0)),
            pl.BlockSpec((None, TILE, d), lambda i, j: (i, jnp.maximum(j - 1, 0), 0))]


def _in_proj_kernel(ctx_ref, x_ref, mod_ref, g_ref, w_ref, qg_ref, kg_ref, cos_ref, sin_ref,
                    qkv_ref, conv_ref, *, n_qkv, n_rot_heads, n_q_heads):
    ns = x_ref.shape[0]
    rows = [slice(s * TILE, (s + 1) * TILE) for s in range(ns)]
    h = jnp.concatenate(
        [_ada_rows(_stream_tile(ctx_ref.at[s], x_ref.at[s]), g_ref[...], mod_ref[s, 0:1, :], mod_ref[s, 1:2, :])
         for s in range(ns)], axis=0).astype(BF16)
    nb = 512
    n_all = w_ref.shape[1]
    cos2 = jnp.concatenate([cos_ref[...]] * ns, axis=0)
    sin2 = jnp.concatenate([sin_ref[...]] * ns, axis=0)
    for j in range(n_all // nb):
        p = jnp.dot(h, w_ref[:, j * nb:(j + 1) * nb], preferred_element_type=F32)
        if j * nb >= n_qkv:
            for s in range(ns):
                conv_ref[s, :, j * nb - n_qkv:(j + 1) * nb - n_qkv] = p[rows[s]]
            continue
        for i in range(nb // HEAD_DIM):
            hd = j * (nb // HEAD_DIM) + i
            t = p[:, i * HEAD_DIM:(i + 1) * HEAD_DIM]
            if hd < n_rot_heads:
                t = _rms_rows(t, qg_ref[...] if hd < n_q_heads else kg_ref[...])
                t = t * cos2 + pltpu.roll(t, HEAD_DIM // 2, 1) * sin2
            if hd < n_q_heads:
                t = t * ((HEAD_DIM ** -0.5) * LOG2_E)
            t = t.astype(BF16)
            for s in range(ns):
                qkv_ref[s, :, hd * HEAD_DIM:(hd + 1) * HEAD_DIM] = t[rows[s]]


def _in_proj(ctx, x, modsel, g, w_in, q_g, k_g, cos2, sin2):
    b, n, d = x.shape
    t = ctx.shape[1] + n
    n_all = w_in.shape[1]
    n_qkv = (N_Q_HEADS + 2 * N_KV_HEADS) * HEAD_DIM
    ns = ROW_SAMPLES if b % ROW_SAMPLES == 0 else 1
    kern = functools.partial(_in_proj_kernel, n_qkv=n_qkv, n_rot_heads=N_Q_HEADS + N_KV_HEADS,
                             n_q_heads=N_Q_HEADS)
    return pl.pallas_call(
        kern,
        out_shape=(jax.ShapeDtypeStruct((b, t, n_qkv), BF16),
                   jax.ShapeDtypeStruct((b, t, n_all - n_qkv), F32)),
        grid=(b // ns, t // TILE),
        in_specs=[
            pl.BlockSpec((ns, TILE, d), lambda i, j: (i, 0, 0)),
            pl.BlockSpec((ns, TILE, d), lambda i, j: (i, jnp.maximum(j - 1, 0), 0)),
            pl.BlockSpec((ns, None, 6, d), lambda i, j: (i, jnp.minimum(j, 1), 0, 0)),
            pl.BlockSpec((1, d), lambda i, j: (0, 0)),
            pl.BlockSpec((d, n_all), lambda i, j: (0, 0)),
            pl.BlockSpec((1, HEAD_DIM), lambda i, j: (0, 0)),
            pl.BlockSpec((1, HEAD_DIM), lambda i, j: (0, 0)),
            pl.BlockSpec((TILE, HEAD_DIM), lambda i, j: (j, 0)),
            pl.BlockSpec((TILE, HEAD_DIM), lambda i, j: (j, 0)),
        ],
        out_specs=(pl.BlockSpec((ns, TILE, n_qkv), lambda i, j: (i, j, 0)),
                   pl.BlockSpec((ns, TILE, n_all - n_qkv), lambda i, j: (i, j, 0))),
        compiler_params=_cparams(("arbitrary", "arbitrary")),
        name="in_proj",
    )(ctx, x, modsel, g, w_in, q_g, k_g, cos2, sin2)


def _attn_kernel(q_ref, k_ref, v_ref, o_ref, *, n_ctx):
    n_heads = q_ref.shape[1] // HEAD_DIM
    heads = [slice(g * HEAD_DIM, (g + 1) * HEAD_DIM) for g in range(n_heads)]
    group = [slice((g // Q_PER_KV) * HEAD_DIM, (g // Q_PER_KV + 1) * HEAD_DIM) for g in range(n_heads)]

    def attend(nk):
        s = _dot_nt(q_ref[:, heads[0]], k_ref[0:nk, group[0]])
        for g in range(n_heads):
            s_next = _dot_nt(q_ref[:, heads[g + 1]], k_ref[0:nk, group[g + 1]]) if g + 1 < n_heads else None
            p = jnp.exp2(s - jnp.max(s, axis=-1, keepdims=True))
            l = jnp.sum(p, axis=-1, keepdims=True)
            o = jnp.dot(p.astype(BF16), v_ref[0:nk, group[g]], preferred_element_type=F32) / l
            o_ref[:, heads[g]] = o.astype(BF16)
            s = s_next

    is_ctx = pl.program_id(1) == 0

    @pl.when(is_ctx)
    def _():
        attend(n_ctx)

    @pl.when(jnp.logical_not(is_ctx))
    def _():
        attend(k_ref.shape[0])


def _attention(qkv, n_ctx):
    b, t, _ = qkv.shape
    q_w, kv_w = N_Q_HEADS * HEAD_DIM, N_KV_HEADS * HEAD_DIM
    assert q_w % kv_w == 0
    kern = functools.partial(_attn_kernel, n_ctx=n_ctx)
    return pl.pallas_call(
        kern,
        out_shape=jax.ShapeDtypeStruct((b, t, N_Q_HEADS * HEAD_DIM), BF16),
        grid=(b, t // TILE),
        in_specs=[
            pl.BlockSpec((None, TILE, q_w), lambda i, j: (i, j, 0)),
            pl.BlockSpec((None, t, kv_w), lambda i, j: (i, 0, q_w // kv_w)),
            pl.BlockSpec((None, t, kv_w), lambda i, j: (i, 0, q_w // kv_w + 1)),
        ],
        out_specs=pl.BlockSpec((None, TILE, q_w), lambda i, j: (i, j, 0)),
        compiler_params=_cparams(("arbitrary",) * 2),
        name="attention",
    )(qkv, qkv, qkv)


def _residual_and_router(x, y, mod_ref, g2_ref, router_ref, x1_ref, h2_ref, aff_ref):
    x1 = x + mod_ref[2:3, :] * y
    x1_ref[...] = x1
    h2 = _ada_rows(x1, g2_ref[...], mod_ref[3:4, :], mod_ref[4:5, :]).astype(BF16)
    h2_ref[...] = h2
    logits = _dot_nt(router_ref[...], h2)
    e = jnp.exp(logits - jnp.max(logits, axis=0, keepdims=True))
    aff_ref[...] = e / jnp.sum(e, axis=0, keepdims=True)


def _mix_out_kernel(oa_ref, gb_ref, gc_ref, xc_ref, gcp_ref, xcp_ref, gcn_ref, xcn_ref,
                    ctx_ref, x_ref, mod_ref, wout_ref, cw_ref, g2_ref, router_ref,
                    x1_ref, h2_ref, aff_ref, *, n_tiles):
    t = pl.program_id(1)
    has_prev = (t >= 2).astype(F32)
    has_next = jnp.logical_and(t >= 1, t <= n_tiles - 2).astype(F32)
    u = gc_ref[...] * xc_ref[...]
    u_prev = gcp_ref[HALO - 1:HALO, :] * xcp_ref[HALO - 1:HALO, :] * has_prev
    u_next = gcn_ref[0:1, :] * xcn_ref[0:1, :] * has_next
    down, up = _shift_rows(u, u_prev, u_next)
    o_conv = gb_ref[...] * (down * cw_ref[0:1, :] + u * cw_ref[1:2, :] + up * cw_ref[2:3, :])
    d_attn = oa_ref.shape[1]
    y = _dot(oa_ref[...], wout_ref[0:d_attn, :]) + _dot(o_conv, wout_ref[d_attn:, :])
    _residual_and_router(_stream_tile(ctx_ref, x_ref), y, mod_ref, g2_ref, router_ref, x1_ref, h2_ref,
                         aff_ref)


def _mix_out(o_attn, conv, ctx, x, modsel, w_out, conv_w, g2, router_t):
    b, t, d = o_attn.shape
    n_tiles = t // TILE
    hb = TILE // HALO
    last = t // HALO - 1
    prev_map = lambda col: (lambda i, j: (i, jnp.maximum(j * hb - 1, 0), col))
    next_map = lambda col: (lambda i, j: (i, jnp.minimum((j + 1) * hb, last), col))
    tile_map = lambda col: (lambda i, j: (i, j, col))
    n_e = router_t.shape[0]
    kern = functools.partial(_mix_out_kernel, n_tiles=n_tiles)
    return pl.pallas_call(
        kern,
        out_shape=(jax.ShapeDtypeStruct((b, t, d), F32),
                   jax.ShapeDtypeStruct((b, t, d), BF16),
                   jax.ShapeDtypeStruct((b, n_e, t), F32)),
        grid=(b, n_tiles),
        in_specs=[
            pl.BlockSpec((None, TILE, d), tile_map(0)),
            pl.BlockSpec((None, TILE, d), tile_map(0)),
            pl.BlockSpec((None, TILE, d), tile_map(1)),
            pl.BlockSpec((None, TILE, d), tile_map(2)),
            pl.BlockSpec((None, HALO, d), prev_map(1)),
            pl.BlockSpec((None, HALO, d), prev_map(2)),
            pl.BlockSpec((None, HALO, d), next_map(1)),
            pl.BlockSpec((None, HALO, d), next_map(2)),
        ] + _stream_specs(d) + [
            _mod_spec(d),
            pl.BlockSpec(w_out.shape, lambda i, j: (0, 0)),
            pl.BlockSpec(conv_w.shape, lambda i, j: (0, 0)),
            pl.BlockSpec((1, d), lambda i, j: (0, 0)),
            pl.BlockSpec((n_e, d), lambda i, j: (0, 0)),
        ],
        out_specs=(pl.BlockSpec((None, TILE, d), tile_map(0)),
                   pl.BlockSpec((None, TILE, d), tile_map(0)),
                   pl.BlockSpec((None, n_e, TILE), lambda i, j: (i, 0, j))),
        compiler_params=_cparams(("arbitrary", "arbitrary")),
        name="mix_out",
    )(o_attn, conv, conv, conv, conv, conv, conv, conv, ctx, x, modsel, w_out, conv_w, g2, router_t)


def _onehot_rows(slot_row, first, n):
    rel = slot_row - (first if isinstance(first, int) else first.astype(F32))
    rows = lax.broadcasted_iota(jnp.int32, (n, slot_row.shape[1]), 0).astype(F32)
    return jnp.where(rel == rows, 1.0, 0.0).astype(BF16)


def _slot_windows(bnd_ref, sample, tile, n_e, slot0, cap, win, group):
    starts, fits = [], []
    for e in range(n_e):
        lo = bnd_ref[sample, e, tile]
        hi = bnd_ref[sample, n_e + e, tile]
        w0 = jnp.minimum((lo >> 4) << 4, slot0 + cap - win)
        ok = hi <= w0 + win
        if e % group == 0:
            fits.append(ok)
        else:
            fits[-1] = jnp.logical_and(fits[-1], ok)
        starts.append(pl.multiple_of(w0, 16))
    return starts, fits


def _moe_gather_kernel(bnd_ref, slot_ref, h_ref, x_ref, *, sets):
    n_e = x_ref.shape[0]
    sample = pl.program_id(0)
    tile = pl.program_id(1)

    @pl.when(tile == 0)
    def _():
        x_ref[...] = jnp.zeros_like(x_ref)

    def onehot(e, first, n):
        return _onehot_rows(slot_ref[e:e + 1, :], first, n)

    for start, length, slot0, cap in sets:
        first_tile, last_tile = start // TILE, (start + length) // TILE
        win = min(GATHER_WIN, cap)

        @pl.when(jnp.logical_and(tile >= first_tile, tile < last_tile))
        def _():
            starts, (fits,) = _slot_windows(bnd_ref, sample, tile, n_e, slot0, cap, win, group=n_e)

            @pl.when(fits)
            def _():
                oh = jnp.concatenate([onehot(e, starts[e], win) for e in range(n_e)], axis=0)
                part = jnp.dot(oh, h_ref[...], preferred_element_type=F32).astype(BF16)
                for e in range(n_e):
                    x_ref[e, pl.ds(starts[e], win), :] += part[e * win:(e + 1) * win]

            @pl.when(jnp.logical_not(fits))
            def _():
                for e in range(n_e):
                    x_ref[e, slot0:slot0 + cap, :] += jnp.dot(
                        onehot(e, slot0, cap), h_ref[...], preferred_element_type=F32).astype(BF16)


def _moe_gather(slots, bounds, h2, sets, n_slots):
    b, n_e, t = slots.shape
    d = h2.shape[2]
    grid_spec = pltpu.PrefetchScalarGridSpec(
        num_scalar_prefetch=1,
        grid=(b, t // TILE),
        in_specs=[
            pl.BlockSpec((None, n_e, TILE), lambda i, j, bnd: (i, 0, j)),
            pl.BlockSpec((None, TILE, d), lambda i, j, bnd: (i, j, 0)),
        ],
        out_specs=pl.BlockSpec((None, n_e, n_slots, d), lambda i, j, bnd: (i, 0, 0, 0)),
    )
    return pl.pallas_call(
        functools.partial(_moe_gather_kernel, sets=sets),
        out_shape=jax.ShapeDtypeStruct((b, n_e, n_slots, d), BF16),
        grid_spec=grid_spec,
        compiler_params=_cparams(("arbitrary", "arbitrary")),
        name="moe_gather",
    )(bounds, slots, h2)


def _moe_ffn_kernel(slot_ref, aff_ref, x_ref, wg_ref, wu_ref, wd_ref, o_ref, wg_s, wu_s, wd_s):
    @pl.when(pl.program_id(1) == 0)
    def _():
        wg_s[...] = wg_ref[...].astype(BF16)
        wu_s[...] = wu_ref[...].astype(BF16)
        wd_s[...] = wd_ref[...].astype(BF16)

    nb, r, d = x_ref.shape
    rows = lax.broadcasted_iota(jnp.int32, (r, slot_ref.shape[2]), 0).astype(F32)
    a_row = jnp.concatenate(
        [jnp.sum(jnp.where(slot_ref[i] == rows, aff_ref[i], 0.0), axis=-1, keepdims=True) for i in range(nb)],
        axis=0)
    x = x_ref[...].reshape(nb * r, d)
    gate = jnp.dot(x, wg_s[...], preferred_element_type=F32)
    up = jnp.dot(x, wu_s[...], preferred_element_type=F32)
    act = (gate * _sigmoid(gate) * up).astype(BF16)
    out = jnp.dot(act, wd_s[...], preferred_element_type=F32) * a_row
    o_ref[...] = out.astype(BF16).reshape(nb, r, d)


def _moe_ffn(slots, aff_t, xs, w_gate, w_up, w_down, layer):
    b, n_e, t = slots.shape
    _, _, r, d = xs.shape
    f = w_gate.shape[3]
    nb = MOE_FFN_SAMPLES if b % MOE_FFN_SAMPLES == 0 else 1
    row = pl.BlockSpec((nb, None, 1, t), lambda e, i: (i, e, 0, 0))
    tok = pl.BlockSpec((nb, None, r, d), lambda e, i: (i, e, 0, 0))
    return pl.pallas_call(
        _moe_ffn_kernel,
        out_shape=jax.ShapeDtypeStruct((b, n_e, r, d), BF16),
        grid=(n_e, b // nb),
        in_specs=[
            row, row, tok,
            pl.BlockSpec((None, None, d, f), lambda e, i: (layer, e, 0, 0)),
            pl.BlockSpec((None, None, d, f), lambda e, i: (layer, e, 0, 0)),
            pl.BlockSpec((None, None, f, d), lambda e, i: (layer, e, 0, 0)),
        ],
        out_specs=tok,
        scratch_shapes=[pltpu.VMEM((d, f), BF16), pltpu.VMEM((d, f), BF16), pltpu.VMEM((f, d), BF16)],
        compiler_params=_cparams(("arbitrary", "arbitrary")),
        name="moe_ffn",
    )(slots.reshape(b, n_e, 1, t), aff_t.reshape(b, n_e, 1, t), xs, w_gate, w_up, w_down)


def _prefix_count(mask, tri):
    return _prefix_count_blocks(mask, tri)[0]


def _prefix_count_blocks(mask, tri):
    blk = tri.shape[0]
    carry = jnp.zeros((mask.shape[0], 1), F32)
    parts, carries = [], [carry]
    for j in range(mask.shape[1] // blk):
        m = mask[:, j * blk:(j + 1) * blk]
        parts.append(jnp.dot(m.astype(BF16), tri, preferred_element_type=F32) + carry)
        carry = carry + jnp.sum(m, axis=-1, keepdims=True)
        carries.append(carry)
    return jnp.concatenate(parts, axis=-1), carries


def _route_kernel(aff_ref, slot_ref, bound_ref, *, sets):
    blk = TILE
    n_e = aff_ref.shape[0]
    tri = jnp.where(lax.broadcasted_iota(jnp.int32, (blk, blk), 0)
                    <= lax.broadcasted_iota(jnp.int32, (blk, blk), 1), 1.0, 0.0).astype(BF16)
    lane = lax.broadcasted_iota(jnp.int32, (n_e, LANES), 1)
    lo = jnp.zeros((n_e, LANES), F32)
    hi = jnp.zeros((n_e, LANES), F32)
    for start, length, slot0, cap in sets:
        cap = float(cap)
        x = aff_ref[:, start:start + length]
        bits = pltpu.bitcast(x, jnp.int32)
        thr = jnp.zeros((bits.shape[0], 1), jnp.int32)

        def enough(cand):
            return jnp.sum(jnp.where(bits >= cand, 1.0, 0.0), axis=-1, keepdims=True) >= cap

        thr = jnp.where(enough(thr | (1 << 30)), thr | (1 << 30), thr)
        for bit in range(29, 0, -2):
            hi_bit, lo_bit = 1 << bit, 1 << (bit - 1)
            both, high, low = thr | hi_bit | lo_bit, thr | hi_bit, thr | lo_bit
            thr = jnp.where(enough(both), both,
                            jnp.where(enough(high), high, jnp.where(enough(low), low, thr)))
        pivot = jnp.max(jnp.where(bits <= thr, x, -1.0), axis=-1, keepdims=True)

        def counts(p):
            return (jnp.sum(jnp.where(x > p, 1.0, 0.0), axis=-1, keepdims=True),
                    jnp.sum(jnp.where(x >= p, 1.0, 0.0), axis=-1, keepdims=True))

        def misplaced(p):
            n_gt, n_ge = counts(p)
            return jnp.max(jnp.where(jnp.logical_or(n_gt >= cap, n_ge < cap), 1.0, 0.0)) > 0.0

        def walk(p):
            n_gt, n_ge = counts(p)
            higher = jnp.min(jnp.where(x > p, x, jnp.inf), axis=-1, keepdims=True)
            lower = jnp.max(jnp.where(x < p, x, -1.0), axis=-1, keepdims=True)
            return jnp.where(n_gt >= cap, higher, jnp.where(n_ge < cap, lower, p))

        pivot = lax.while_loop(misplaced, walk, pivot)
        above = jnp.where(x > pivot, 1.0, 0.0)
        tied = jnp.where(x == pivot, 1.0, 0.0)
        need = cap - jnp.sum(above, axis=-1, keepdims=True)
        chosen = above + jnp.where(_prefix_count(tied, tri) <= need, tied, 0.0)
        count, before = _prefix_count_blocks(chosen, tri)
        slot_ref[:, start:start + length] = jnp.where(chosen > 0.0, count - 1.0 + slot0, -1.0)
        for j in range(length // blk):
            here = lane == (start // blk + j)
            lo = jnp.where(here, before[j] + slot0, lo)
            hi = jnp.where(here, before[j + 1] + slot0, hi)
    bound_ref[0:n_e, :] = lo
    bound_ref[n_e:, :] = hi


def _route(aff_t, sets):
    b, n_e, t = aff_t.shape
    assert t // TILE <= LANES
    kern = functools.partial(_route_kernel, sets=tuple(sets))
    spec = pl.BlockSpec((None, n_e, t), lambda i: (i, 0, 0))
    bspec = pl.BlockSpec((None, 2 * n_e, LANES), lambda i: (i, 0, 0))
    return pl.pallas_call(
        kern,
        out_shape=(jax.ShapeDtypeStruct((b, n_e, t), F32),
                   jax.ShapeDtypeStruct((b, 2 * n_e, LANES), F32)),
        grid=(b,),
        in_specs=[spec],
        out_specs=(spec, bspec),
        compiler_params=_cparams(("arbitrary",)),
        name="moe_route",
    )(aff_t)


def _combine_kernel(bnd_ref, slot_ref, o_ref, x_ref, mod_ref, g_ref, out_ref, *, sets, final_norm):
    n_e = o_ref.shape[0]
    sample = pl.program_id(0)
    tile = pl.program_id(1)
    tn = lambda a, w: lax.dot_general(a, w, (((0,), (0,)), ((), ())), preferred_element_type=F32)

    def onehot(e, first, n):
        return _onehot_rows(slot_ref[e:e + 1, :], first, n)

    def scatter_full(experts, slot0, cap):
        y = None
        for e in experts:
            term = tn(onehot(e, slot0, cap), o_ref[e, slot0:slot0 + cap, :])
            y = term if y is None else y + term
        return y

    for start, length, slot0, cap in sets:
        first_tile, last_tile = start // TILE, (start + length) // TILE

        def finish(y):
            x2 = x_ref[...] + mod_ref[5:6, :] * y
            out_ref[...] = _rms_rows(x2, g_ref[...]) if final_norm else x2

        @pl.when(jnp.logical_and(tile >= first_tile, tile < last_tile))
        def _():
            win = min(SCATTER_WIN, cap)
            starts, (fits,) = _slot_windows(bnd_ref, sample, tile, n_e, slot0, cap, win, group=n_e)

            @pl.when(fits)
            def _():
                oh = jnp.concatenate([onehot(e, starts[e], win) for e in range(n_e)], axis=0)
                rows = jnp.concatenate([o_ref[e, pl.ds(starts[e], win), :] for e in range(n_e)], axis=0)
                finish(tn(oh, rows))

            @pl.when(jnp.logical_not(fits))
            def _():
                finish(scatter_full(range(n_e), slot0, cap))


def _combine(slots, bounds, outs, x1, modsel, g, sets, *, tile_offset, final_norm):
    b, t, d = x1.shape
    _, n_e, r, _ = outs.shape
    assert all(s[2] % 16 == 0 and s[3] % 16 == 0 for s in sets)
    kern = functools.partial(_combine_kernel, sets=sets, final_norm=final_norm)
    grid_spec = pltpu.PrefetchScalarGridSpec(
        num_scalar_prefetch=1,
        grid=(b, t // TILE),
        in_specs=[
            pl.BlockSpec((None, n_e, TILE), lambda i, j, bnd: (i, 0, j)),
            pl.BlockSpec((None, n_e, r, d), lambda i, j, bnd: (i, 0, 0, 0)),
            pl.BlockSpec((None, TILE, d), lambda i, j, bnd: (i, j, 0)),
            pl.BlockSpec((None, None, 6, d), lambda i, j, bnd: (i, jnp.minimum(j + tile_offset, 1), 0, 0)),
            pl.BlockSpec((1, d), lambda i, j, bnd: (0, 0)),
        ],
        out_specs=pl.BlockSpec((None, TILE, d), lambda i, j, bnd: (i, j, 0)),
    )
    return pl.pallas_call(
        kern,
        out_shape=jax.ShapeDtypeStruct((b, t, d), F32),
        grid_spec=grid_spec,
        compiler_params=_cparams(("arbitrary", "arbitrary")),
        name="moe_combine",
    )(bounds, slots, outs, x1, modsel, g)


def _rwkv_feat_kernel(x_ref, xp_ref, xn_ref, mod_ref, g_ref, mu_ref, wr_ref, wk_ref, wv_ref,
                      w1_ref, a1_ref, g1_ref, w2_ref, a2_ref, g2_ref, w0_ref, a0_ref,
                      kk_ref, ka_ref, segr_ref, segb_ref,
                      r_out, v_out, kkn_out, g_out, lw_out, kd_out, bb_out, *, n_tiles):
    t = pl.program_id(1)
    has_prev = (t >= 2).astype(F32)
    has_next = jnp.logical_and(t >= 1, t <= n_tiles - 2).astype(F32)
    g = g_ref[...]
    shift = mod_ref[0:1, :]
    scale = mod_ref[1:2, :]
    h = _ada_rows(x_ref[...], g, shift, scale)
    h_prev = _ada_rows(xp_ref[HALO - 1:HALO, :], g, shift, scale) * has_prev
    h_next = _ada_rows(xn_ref[0:1, :], g, shift, scale) * has_next
    down, up = _shift_rows(h, h_prev, h_next)
    xx = 0.5 * (down + up) - h
    d = h.shape[1]

    r_out[...] = _dot(h + xx * mu_ref[0:1, :], wr_ref[...])
    k = _dot(h + xx * mu_ref[2:3, :], wk_ref[...])
    v_out[...] = _dot(h + xx * mu_ref[3:4, :], wv_ref[...])
    gate = _sigmoid(_dot(h + xx * mu_ref[5:6, :], g1_ref[...]))
    g_out[...] = _dot(gate, g2_ref[...])

    kkp = k * kk_ref[...]
    ss = _segsum(kkp * kkp, segr_ref[...], segb_ref[...])
    kkn = kkp * lax.rsqrt(jnp.maximum(ss, 1e-24))
    kkn_out[...] = kkn

    lora_w = _dot(jnp.tanh(_dot(h + xx * mu_ref[1:2, :], w1_ref[...])), w2_ref[...])
    lora_a = _dot(_dot(h + xx * mu_ref[4:5, :], a1_ref[...]), a2_ref[...])
    for di in range(2):
        sl = slice(di * d, (di + 1) * d)
        lw_out[di] = -DECAY_SCALE * _sigmoid(w0_ref[di:di + 1, :] + lora_w[:, sl])
        a = _sigmoid(a0_ref[di:di + 1, :] + lora_a[:, sl])
        kd_out[di] = k * (1.0 + (a - 1.0) * ka_ref[...])
        bb_out[di] = a * kkn


def _rwkv_feat(xs, modsel, g, p):
    b, t, d = xs.shape
    n_tiles = t // TILE
    hb = TILE // HALO
    last = t // HALO - 1
    full = lambda a: pl.BlockSpec(a.shape, lambda i, j: (0,) * a.ndim)
    weights = [p["mu"], p["w_r"], p["w_k"], p["w_v"], p["w1"], p["a1"], p["g1"], p["w2"], p["a2"],
               p["g2"], p["w0"], p["a0"], p["k_k"], p["k_a"], p["seg_r"], p["seg_b"]]
    tile_spec = pl.BlockSpec((None, TILE, d), lambda i, j: (i, j, 0))
    dir_spec = pl.BlockSpec((2, None, TILE, d), lambda i, j: (0, i, j, 0))
    kern = functools.partial(_rwkv_feat_kernel, n_tiles=n_tiles)
    one = jax.ShapeDtypeStruct((b, t, d), F32)
    two = jax.ShapeDtypeStruct((2, b, t, d), F32)
    return pl.pallas_call(
        kern,
        out_shape=(one, one, one, one, two, two, two),
        grid=(b, n_tiles),
        in_specs=[
            tile_spec,
            pl.BlockSpec((None, HALO, d), lambda i, j: (i, jnp.maximum(j * hb - 1, 0), 0)),
            pl.BlockSpec((None, HALO, d), lambda i, j: (i, jnp.minimum((j + 1) * hb, last), 0)),
            _mod_spec(d),
            pl.BlockSpec((1, d), lambda i, j: (0, 0)),
        ] + [full(a) for a in weights],
        out_specs=(tile_spec, tile_spec, tile_spec, tile_spec, dir_spec, dir_spec, dir_spec),
        compiler_params=_cparams(("arbitrary", "arbitrary")),
        name="rwkv_feat",
    )(xs, xs, xs, modsel, g, *weights)


def _scan_kernel(lw_ref, kd_ref, bb_ref, kk_ref, r_ref, v_ref, *rest, rev):
    add_ref = rest[0] if len(rest) == 3 else None
    y_ref, s_ref = rest[-2:]

    @pl.when(pl.program_id(1) == 0)
    def _():
        s_ref[...] = jnp.zeros_like(s_ref)

    L = CHUNK
    ti = lax.broadcasted_iota(jnp.int32, (L, LANES), 0)
    ii = lax.broadcasted_iota(jnp.int32, (L, LANES), 1) & (L - 1)
    before = (ii > ti) if rev else (ii < ti)
    before_eq = (ii >= ti) if rev else (ii <= ti)
    t2 = lax.broadcasted_iota(jnp.int32, (L, L), 0)
    i2 = lax.broadcasted_iota(jnp.int32, (L, L), 1)
    tri = jnp.where((i2 >= t2) if rev else (i2 <= t2), 1.0, 0.0).astype(BF16)
    r128 = lax.broadcasted_iota(jnp.int32, (LANES, LANES), 0)
    c128 = lax.broadcasted_iota(jnp.int32, (LANES, LANES), 1)
    head_eq = (r128 >> 6) == (c128 >> 6)
    eye = r128 == c128

    def same_block(log2s):
        return (ti >> log2s) == (ii >> log2s)

    def bdiag(y):
        yb = y.astype(BF16)
        return jnp.where(head_eq, jnp.concatenate([yb, yb], axis=0), jnp.zeros((), BF16))

    def hmul(x, y):
        return jnp.dot(x.astype(BF16), bdiag(y), preferred_element_type=F32)

    def hmul2(x, y, z):
        return jnp.dot(x.astype(BF16), jnp.concatenate([bdiag(y), bdiag(z)], axis=1),
                       preferred_element_type=F32)

    n_sub = lw_ref.shape[0] // L
    n_pair = lw_ref.shape[1] // LANES
    sub_order = list(range(n_sub))[::-1] if rev else list(range(n_sub))
    chains = [(slice(s * L, (s + 1) * L), slice(p * LANES, (p + 1) * LANES))
              for s in sub_order for p in range(n_pair)]

    def each(f, *cols):
        return [f(*args) for args in zip(*cols)]

    lam_sub = {s: _split_dot_left(tri, lw_ref[s * L:(s + 1) * L, :]) for s in sub_order}
    lw = [lw_ref[rs_, cs] for rs_, cs in chains]
    lam = [lam_sub[rs_.start // L][:, cs] for rs_, cs in chains]
    kk = [kk_ref[rs_, cs] for rs_, cs in chains]
    bb = [bb_ref[rs_, cs] for rs_, cs in chains]
    kd = [kd_ref[rs_, cs] for rs_, cs in chains]
    r = [r_ref[rs_, cs] for rs_, cs in chains]
    v = [v_ref[rs_, cs] for rs_, cs in chains]

    tot = each(lambda a: jnp.sum(a, axis=0, keepdims=True), lw)
    rho = each(lambda a: a[L // 2:L // 2 + 1, :], lam)
    ks = each(lambda a, b, c: a * jnp.exp(b - c), kk, lam, lw)
    rs = each(lambda a, b: a * jnp.exp(b), r, lam)
    e_neg = each(lambda a: jnp.exp(-a), rho)
    e_inv = each(lambda a, b: jnp.exp(a - b), rho, lam)
    e_end = each(lambda a, b: jnp.exp(a - b), tot, lam)

    lhs = each(lambda a, b, c: jnp.concatenate([a * c, b * c], axis=0).astype(BF16), ks, rs, e_neg)
    nt = lambda a, w: lax.dot_general(a, w, (((1,), (1,)), ((), ())), preferred_element_type=F32)
    g_bk = each(lambda a, b, c, d: nt(a, jnp.concatenate([bdiag(b * d), bdiag(c * d)], axis=0)),
                lhs, bb, kd, e_inv)
    a_b = each(lambda a: jnp.where(before, a[0:L, 0:LANES], 0.0), g_bk)
    m_b = each(lambda a: jnp.where(before_eq, a[L:, 0:LANES], 0.0), g_bk)
    mask_am = jnp.concatenate([before, before_eq], axis=0)
    amv = each(lambda a, b: hmul(jnp.where(mask_am, a[:, LANES:], 0.0), b), g_bk, v)
    akv = each(lambda a: a[0:L], amv)
    mkv = each(lambda a: a[L:], amv)

    ident = jnp.where(ti == ii, 1.0, 0.0)
    blk8 = same_block(3)
    n1 = each(lambda a: jnp.where(blk8, -a, 0.0), a_b)
    n2 = each(hmul, n1, n1)
    n34 = each(lambda a, b: hmul(jnp.concatenate([a, b], axis=0), b), n1, n2)
    n3 = each(lambda a: a[0:L], n34)
    n4 = each(lambda a: a[L:], n34)
    x = each(lambda a, b, c: ident + a + b + c, n1, n2, n3)
    tinv = each(lambda a, b: a + hmul(a, b), x, n4)
    for log2s in (3, 4, 5):
        off_mask = jnp.logical_and(same_block(log2s + 1), jnp.logical_not(same_block(log2s)))
        off = each(lambda a: jnp.where(off_mask, a, 0.0), a_b)
        corr = each(hmul, off, tinv)
        tinv = each(lambda a, b: a - hmul(a, b), tinv, corr)

    qp = each(lambda a, b, c: -hmul2(a, b, c), tinv, ks, akv)
    q1 = each(lambda a: a[:, 0:LANES], qp)
    p0 = each(lambda a: a[:, LANES:], qp)
    mqp = each(hmul2, m_b, q1, p0)
    q2 = each(lambda a, b: a + b[:, 0:LANES], rs, mqp)
    y0 = each(lambda a, b: a[:, LANES:] + b, mqp, mkv)
    end = each(lambda a, b, c: jnp.concatenate([a * c, b * c], axis=0), bb, kd, e_end)
    zero = jnp.zeros((L, LANES), F32)
    mn = each(lambda a, b, c, d: _dot_tn(a, jnp.concatenate(
        [jnp.concatenate([b, c], axis=1), jnp.concatenate([zero, d], axis=1)], axis=0)),
        end, q1, p0, v)
    m_t = each(lambda a, c: jnp.where(head_eq, a[:, 0:LANES], 0.0) + jnp.where(eye, jnp.exp(c), 0.0),
               mn, tot)
    n0 = each(lambda a: jnp.where(head_eq, a[:, LANES:], 0.0), mn)
    qm = each(lambda a, b: jnp.concatenate([a, b], axis=0).astype(BF16), q2, m_t)

    state = [s_ref[p] for p in range(n_pair)]
    for idx, (rs_, cs) in enumerate(chains):
        p = idx % n_pair
        out = jnp.dot(qm[idx], state[p].astype(BF16), preferred_element_type=F32)
        y = out[0:L] + y0[idx]
        y_ref[rs_, cs] = y if add_ref is None else add_ref[rs_, cs] + y
        state[p] = out[L:] + n0[idx]
    for p in range(n_pair):
        s_ref[p] = state[p]


def _split_dot_left(tri, x):
    acc = None
    rem = x
    for _ in range(2):
        hi = rem.astype(BF16)
        term = jnp.dot(tri, hi, preferred_element_type=F32)
        acc = term if acc is None else acc + term
        rem = rem - hi.astype(F32)
    return acc


def _rwkv_scan(lw, kd, bb, kk, r, v, n_ctx, rev, add=None):
    _, b, t, d = lw.shape
    assert n_ctx % SCAN_BLOCK == 0 and t % SCAN_BLOCK == 0
    nb_ctx = n_ctx // SCAN_BLOCK
    nb = t // SCAN_BLOCK
    di = 1 if rev else 0

    def block(j):
        return jnp.where(j < nb_ctx, nb_ctx - 1 - j, nb + nb_ctx - 1 - j) if rev else j

    dir_spec = pl.BlockSpec((None, None, SCAN_BLOCK, d), lambda i, j: (di, i, block(j), 0))
    one_spec = pl.BlockSpec((None, SCAN_BLOCK, d), lambda i, j: (i, block(j), 0))
    extra = [] if add is None else [add]
    return pl.pallas_call(
        functools.partial(_scan_kernel, rev=rev),
        out_shape=jax.ShapeDtypeStruct((b, t, d), F32),
        grid=(b, nb),
        in_specs=[dir_spec, dir_spec, dir_spec, one_spec, one_spec, one_spec] + ([one_spec] if extra else []),
        out_specs=one_spec,
        scratch_shapes=[pltpu.VMEM((d // LANES, LANES, LANES), F32)],
        compiler_params=_cparams(("arbitrary", "arbitrary")),
        name="rwkv_scan_bwd" if rev else "rwkv_scan_fwd",
    )(lw, kd, bb, kk, r, v, *extra)


def _rwkv_out_kernel(y_ref, r_ref, v_ref, g_ref, kdf_ref, kdb_ref, x_ref, mod_ref,
                     wo_ref, lnw_ref, lnb_ref, rk_ref, segr_ref, segb_ref, g2_ref, router_ref,
                     x1_ref, h2_ref, aff_ref):
    seg_r = segr_ref[...]
    seg_b = segb_ref[...]
    inv_n = 1.0 / RW_N
    y = y_ref[...]
    mean = _segsum(y, seg_r, seg_b) * inv_n
    dev = y - mean
    var = _segsum(dev * dev, seg_r, seg_b) * inv_n
    yn = dev * lax.rsqrt(var + GN_EPS) * lnw_ref[...] + lnb_ref[...]
    r = r_ref[...]
    rk = rk_ref[...]
    bonus = _segsum(r * kdf_ref[...] * rk + r * kdb_ref[...] * rk, seg_r, seg_b) * v_ref[...]
    out = _dot((yn + bonus) * g_ref[...], wo_ref[...])
    _residual_and_router(x_ref[...], out, mod_ref, g2_ref, router_ref, x1_ref, h2_ref, aff_ref)


def _rwkv_out(y, r, v, g, kd, xs, modsel, p, g2, router_t, n_ctx):
    b, t, d = xs.shape
    off = n_ctx // TILE
    n = t - n_ctx
    n_e = router_t.shape[0]
    tile = pl.BlockSpec((None, TILE, d), lambda i, j: (i, j + off, 0))
    dir0 = pl.BlockSpec((None, None, TILE, d), lambda i, j: (0, i, j + off, 0))
    dir1 = pl.BlockSpec((None, None, TILE, d), lambda i, j: (1, i, j + off, 0))
    vec = pl.BlockSpec((1, d), lambda i, j: (0, 0))
    out_tile = pl.BlockSpec((None, TILE, d), lambda i, j: (i, j, 0))
    return pl.pallas_call(
        _rwkv_out_kernel,
        out_shape=(jax.ShapeDtypeStruct((b, n, d), F32),
                   jax.ShapeDtypeStruct((b, n, d), BF16),
                   jax.ShapeDtypeStruct((b, n_e, n), F32)),
        grid=(b, n // TILE),
        in_specs=[tile, tile, tile, tile, dir0, dir1, tile,
                  pl.BlockSpec((None, None, 6, d), lambda i, j: (i, 1, 0, 0)),
                  pl.BlockSpec((d, d), lambda i, j: (0, 0)),
                  vec, vec, vec,
                  pl.BlockSpec((d, LANES), lambda i, j: (0, 0)),
                  pl.BlockSpec((LANES, d), lambda i, j: (0, 0)),
                  vec,
                  pl.BlockSpec((n_e, d), lambda i, j: (0, 0))],
        out_specs=(out_tile, out_tile, pl.BlockSpec((None, n_e, TILE), lambda i, j: (i, 0, j))),
        compiler_params=_cparams(("arbitrary", "arbitrary")),
        name="rwkv_out",
    )(y, r, v, g, kd, kd, xs, modsel, p["w_o"], p["ln_w"], p["ln_b"], p["r_k"], p["seg_r"],
      p["seg_b"], g2, router_t)


def _moe_block(x1, h2, aff_t, token_sets, modsel, g, w_gate, w_up, w_down, layer, *, tile_offset,
               final_norm):
    sets, n_slots = [], 0
    for start, length in token_sets:
        cap = EC_CAPACITY * length // N_EXPERTS
        sets.append((start, length, n_slots, cap))
        n_slots += cap
    sets = tuple(sets)
    slots, bounds = _route(aff_t, sets)
    bounds = bounds[:, :, :aff_t.shape[2] // TILE].astype(jnp.int32)
    xs = _moe_gather(slots, bounds, h2, sets, n_slots)
    outs = _moe_ffn(slots, aff_t, xs, w_gate, w_up, w_down, layer)
    return _combine(slots, bounds, outs, x1, modsel, g, sets, tile_offset=tile_offset,
                    final_norm=final_norm)


def _rope_tables(n_ctx, n):
    rows = n // GRID_W
    row = jnp.broadcast_to(jnp.arange(rows)[:, None], (rows, GRID_W)).reshape(-1).astype(F32)
    col = jnp.broadcast_to(jnp.arange(GRID_W)[None, :], (rows, GRID_W)).reshape(-1).astype(F32)
    pairs = HEAD_DIM // 4
    inv = ROPE_THETA ** (-jnp.arange(pairs, dtype=F32) / pairs)
    ang = jnp.concatenate([row[:, None] * inv, col[:, None] * inv], axis=-1)
    cos, sin = jnp.cos(ang), jnp.sin(ang)
    cos2 = jnp.concatenate([cos, cos], axis=-1)
    sin2 = jnp.concatenate([-sin, sin], axis=-1)
    cos2 = jnp.concatenate([jnp.ones((n_ctx, HEAD_DIM), F32), cos2], axis=0)
    sin2 = jnp.concatenate([jnp.zeros((n_ctx, HEAD_DIM), F32), sin2], axis=0)
    return cos2, sin2


def _block_diag2(a, b):
    z = jnp.zeros_like(a)
    return jnp.concatenate([jnp.concatenate([a, z], axis=1), jnp.concatenate([z, b], axis=1)], axis=0)


def kernel(x, c, ctx, c_ctx, ada_w, ada_b, norm_g, final_g, mix_w_in, mix_w_out, q_norm_g, k_norm_g, conv_w, rw_mu, rw_w_r, rw_w_k, rw_w_v, rw_w_o, rw_w0, rw_w1, rw_w2, rw_a0, rw_a1, rw_a2, rw_g1, rw_g2, rw_k_k, rw_k_a, rw_r_k, rw_ln_w, rw_ln_b, moe_router, moe_w_gate, moe_w_up, moe_w_down):
    b, n, d = x.shape
    n_ctx = ctx.shape[1]
    depth = ada_w.shape[0]
    assert depth == 2 and n_ctx == TILE and n % TILE == 0 and d % LANES == 0

    rows = -(-(b + 1) // HALO) * HALO
    cc = jnp.concatenate([c, c_ctx[None, :], jnp.zeros((rows - b - 1, d), F32)], axis=0)
    mod = _ada_mod(cc, ada_w, ada_b)
    mod_lat = mod[:, :b].reshape(depth, b, 1, 6, d)
    mod_ctx = jnp.broadcast_to(mod[:, b].reshape(depth, 1, 1, 6, d), (depth, b, 1, 6, d))
    modsel = jnp.concatenate([mod_ctx, mod_lat], axis=2)

    router_t = jnp.swapaxes(moe_router, 1, 2).astype(BF16)

    cos2, sin2 = _rope_tables(n_ctx, n)
    qkv, conv = _in_proj(ctx, x, modsel[0], norm_g[0, 0][None], mix_w_in[0].astype(BF16),
                         q_norm_g[0][None], k_norm_g[0][None], cos2, sin2)
    o_attn = _attention(qkv, n_ctx)
    x1, h2, aff_t = _mix_out(o_attn, conv, ctx, x, modsel[0], mix_w_out[0].astype(BF16), conv_w[0],
                             norm_g[0, 1][None], router_t[0])
    xs = _moe_block(x1, h2, aff_t, [(n_ctx, n), (0, n_ctx)], modsel[0], final_g[None],
                    moe_w_gate, moe_w_up, moe_w_down, 0, tile_offset=0, final_norm=False)

    heads = d // RW_N
    assert heads <= LANES
    seg_r = (jnp.arange(d)[:, None] // RW_N == jnp.arange(LANES)[None, :]).astype(BF16)
    p = {
        "mu": rw_mu[0], "w_r": rw_w_r[0].astype(BF16), "w_k": rw_w_k[0].astype(BF16),
        "w_v": rw_w_v[0].astype(BF16), "w_o": rw_w_o[0].astype(BF16),
        "w1": jnp.concatenate([rw_w1[0, 0], rw_w1[0, 1]], axis=1).astype(BF16),
        "a1": jnp.concatenate([rw_a1[0, 0], rw_a1[0, 1]], axis=1).astype(BF16),
        "g1": rw_g1[0].astype(BF16),
        "w2": _block_diag2(rw_w2[0, 0], rw_w2[0, 1]).astype(BF16),
        "a2": _block_diag2(rw_a2[0, 0], rw_a2[0, 1]).astype(BF16),
        "g2": rw_g2[0].astype(BF16),
        "w0": rw_w0[0], "a0": rw_a0[0], "k_k": rw_k_k[0][None], "k_a": rw_k_a[0][None],
        "r_k": rw_r_k[0].reshape(1, d), "ln_w": rw_ln_w[0][None], "ln_b": rw_ln_b[0][None],
        "seg_r": seg_r, "seg_b": seg_r.T,
    }
    r, v, kk, g, lw, kd, bb = _rwkv_feat(xs, modsel[1], norm_g[1, 0][None], p)
    y_fwd = _rwkv_scan(lw, kd, bb, kk, r, v, n_ctx, rev=False)
    y = _rwkv_scan(lw, kd, bb, kk, r, v, n_ctx, rev=True, add=y_fwd)
    x1, h2, aff_t = _rwkv_out(y, r, v, g, kd, xs, modsel[1], p, norm_g[1, 1][None], router_t[1], n_ctx)
    return _moe_block(x1, h2, aff_t, [(0, n)], modsel[1], final_g[None],
                      moe_w_gate, moe_w_up, moe_w_down, 1, tile_offset=1, final_norm=True)
```
